```python
import jax, jax.numpy as jnp
from jax import lax
import numpy as np

D_MODEL = 1024
BATCH = 16
SEQ = 256
DEPTH = 4
DEC_BATCH = 4
DEC_SEQ = 2048
PAST_LEN = 512

GRID_W = 64
N_MIXERS = 2
N_GLA = (DEPTH + 1) // 2
N_MLA = DEPTH // 2
GLA_HEADS = 4
GLA_DK = D_MODEL // 2 // GLA_HEADS
GLA_DV = D_MODEL // GLA_HEADS
GLA_DK_TOT = GLA_HEADS * GLA_DK
GLA_DV_TOT = GLA_HEADS * GLA_DV
GATE_RANK = 16
GATE_TEMP = 16.0
GLA_CHUNK = 64
GLA_SPLITS = (GLA_DK_TOT, 2 * GLA_DK_TOT, 2 * GLA_DK_TOT + GLA_DV_TOT, 2 * GLA_DK_TOT + 2 * GLA_DV_TOT)
GLA_IN = 2 * GLA_DK_TOT + 2 * GLA_DV_TOT + 2 * GATE_RANK
MLA_HEADS = 8
Q_LORA = 384
KV_LORA = 256
QK_NOPE = 128
QK_ROPE = 64
V_HEAD = 128
MLA_SCALE = (QK_NOPE + QK_ROPE) ** -0.5
ROPE_BASE = 10000.0
Q_BLOCK = 128
D_FF = 2816
N_EXPERTS = 8
TOP_K = 2
D_EXPERT = 3584
EPS = 1e-6

kernel_name = "hybrid_gla_mla_prefix_diffusion_step"


def rms_norm(x, g):
    xf = x.astype(jnp.float32)
    y = xf * lax.rsqrt(jnp.mean(xf * xf, axis=-1, keepdims=True) + EPS)
    return (y * g.astype(jnp.float32)).astype(x.dtype)


def ada_mod(cond, w, b):
    m = jax.nn.silu(cond) @ w + b
    return jnp.split(m[..., None, :], 6, axis=-1)


def modulate(x, g, shift, scale):
    return rms_norm(x, g) * (1 + scale) + shift


def gla_chunk_scan(q, k, v, log_a, s0):
    dtype = q.dtype
    B, T, H, _ = q.shape
    n = T // GLA_CHUNK

    def to_chunks(t):
        return jnp.moveaxis(t.astype(jnp.float32).reshape(B, n, GLA_CHUNK, H, -1), 1, 0)

    mask = jnp.tril(jnp.ones((GLA_CHUNK, GLA_CHUNK), dtype=bool))[None, :, :, None, None]

    def step(s, inp):
        qi, ki, vi, ai = inp
        b = jnp.cumsum(ai, axis=1)
        diff = b[:, :, None] - b[:, None]
        decay = jnp.exp(jnp.where(mask, diff, -jnp.inf))
        scores = jnp.einsum('bthd,btshd,bshd->bhts', qi, decay, ki)
        o_intra = jnp.einsum('bhts,bshv->bthv', scores, vi)
        o_inter = jnp.einsum('bthd,bhdv->bthv', qi * jnp.exp(b), s)
        b_last = b[:, -1]
        k_dec = ki * jnp.exp(b_last[:, None] - b)
        s_new = jnp.exp(b_last)[..., None] * s + jnp.einsum('bshd,bshv->bhdv', k_dec, vi)
        return s_new, o_intra + o_inter

    s_fin, o = lax.scan(step, s0.astype(jnp.float32), tuple(map(to_chunks, (q, k, v, log_a))))
    o = jnp.moveaxis(o, 0, 1).reshape(B, T, H, -1)
    return o.astype(dtype), s_fin.astype(dtype)


def gla_mixer(h, s0, w_in, w_gate_up, b_gate, g_norm, w_out):
    B, T, _ = h.shape
    q, k, v, r, gd = jnp.split(h @ w_in, GLA_SPLITS, axis=-1)
    la = jax.nn.log_sigmoid(
        jnp.einsum('btzr,zrk->btzk', gd.reshape(B, T, 2, GATE_RANK), w_gate_up) + b_gate) / GATE_TEMP
    heads = lambda t: t.reshape(B, T, GLA_HEADS, -1)
    flip = lambda t: jnp.flip(t, axis=1)
    q = heads(q) * GLA_DK ** -0.5
    k, v = heads(k), heads(v)
    la_f, la_b = heads(la[:, :, 0]), heads(la[:, :, 1])
    o_f, s_f = gla_chunk_scan(q, k, v, la_f, s0[:, 0])
    o_b, s_b = gla_chunk_scan(flip(q), flip(k), flip(v), flip(la_b), s0[:, 1])
    o = rms_norm(o_f + flip(o_b), g_norm).reshape(B, T, GLA_DV_TOT) * jax.nn.silu(r)
    return o @ w_out, jnp.stack([s_f, s_b], axis=1)


def axial_rope(row, col):
    half = QK_ROPE // 2
    inv = ROPE_BASE ** (-jnp.arange(0, half, 2, dtype=jnp.float32) / half)
    ang = jnp.concatenate([row[:, None] * inv, col[:, None] * inv], axis=-1)
    return jnp.cos(ang), jnp.sin(ang)


def apply_rope(x, cos, sin):
    xr = x.reshape(*x.shape[:-1], -1, 2)
    x0, x1 = xr[..., 0], xr[..., 1]
    out = jnp.stack([x0 * cos - x1 * sin, x0 * sin + x1 * cos], axis=-1)
    return out.reshape(x.shape).astype(x.dtype)


def mla_down(h, w_down, q_norm, kv_norm):
    cq, ckv, kr = jnp.split(h @ w_down, (Q_LORA, Q_LORA + KV_LORA), axis=-1)
    return rms_norm(cq, q_norm), rms_norm(ckv, kv_norm), kr


def mla_queries(cq, w_uq):
    B, T, _ = cq.shape
    q = (cq @ w_uq).reshape(B, T, MLA_HEADS, QK_NOPE + QK_ROPE)
    return q[..., :QK_NOPE], q[..., QK_NOPE:]


def mla_up_kv(ckv, w_ukv):
    B, S, _ = ckv.shape
    kv = (ckv @ w_ukv).reshape(B, S, MLA_HEADS, QK_NOPE + V_HEAD)
    return kv[..., :QK_NOPE], kv[..., QK_NOPE:]


def attend_blocks(q_nope, q_rope, k_nope, k_rope, v):
    B, T, H, _ = q_nope.shape
    n = T // Q_BLOCK
    blocks = lambda t: jnp.moveaxis(t.reshape(B, n, Q_BLOCK, *t.shape[2:]), 1, 0)

    def one(qb):
        qn, qr = qb
        s = (jnp.einsum('bqhd,bkhd->bhqk', qn, k_nope)
             + jnp.einsum('bqhr,bkr->bhqk', qr, k_rope)).astype(jnp.float32) * MLA_SCALE
        p = jax.nn.softmax(s, axis=-1).astype(v.dtype)
        return jnp.einsum('bhqk,bkhv->bqhv', p, v)

    o = lax.map(one, (blocks(q_nope), blocks(q_rope)))
    return jnp.moveaxis(o, 0, 1).reshape(B, T, H * V_HEAD)


def mla_context(h, w_down, q_norm, kv_norm, w_uq, w_ukv, w_out):
    cq, ckv, kr = mla_down(h, w_down, q_norm, kv_norm)
    q_nope, q_rope = mla_queries(cq, w_uq)
    k_nope, v = mla_up_kv(ckv, w_ukv)
    o = attend_blocks(q_nope, q_rope, k_nope, kr, v)
    return o @ w_out, ckv, kr


def mla_latent(h, ckv_ctx, kr_ctx, cos, sin, w_down, q_norm, kv_norm, w_uq, w_ukv, w_out):
    cq, ckv, kr = mla_down(h, w_down, q_norm, kv_norm)
    q_nope, q_rope = mla_queries(cq, w_uq)
    q_rope = apply_rope(q_rope, cos[:, None], sin[:, None])
    kr = apply_rope(kr, cos, sin)
    k_nope, v = mla_up_kv(jnp.concatenate([ckv_ctx, ckv], axis=1), w_ukv)
    o = attend_blocks(q_nope, q_rope, k_nope, jnp.concatenate([kr_ctx, kr], axis=1), v)
    return o @ w_out


def swiglu(h, w_gate_up, w_down):
    g, u = jnp.split(h @ w_gate_up, 2, axis=-1)
    return (jax.nn.silu(g) * u) @ w_down


def moe_swiglu(h, router, w_gate_up, w_down):
    logits = (h @ router).astype(jnp.float32)
    top_v, top_i = lax.top_k(logits, TOP_K)
    w = jax.nn.softmax(top_v, axis=-1)
    gates = jnp.einsum('btk,btke->bte', w, jax.nn.one_hot(top_i, N_EXPERTS, dtype=w.dtype)).astype(h.dtype)
    out = jnp.zeros_like(h)
    for e in range(N_EXPERTS):
        out = out + gates[..., e:e + 1] * swiglu(h, w_gate_up[e], w_down[e])
    return out


def setup_inputs(seed: int = 0) -> dict:
    key = jax.random.key(seed)
    ks = iter(jax.random.split(key, 40))
    nrm = lambda shape, scale: jax.random.normal(next(ks), shape, jnp.float32) * scale
    gain = lambda shape: 1.0 + nrm(shape, 0.02)
    D = D_MODEL
    H_QK = MLA_HEADS * (QK_NOPE + QK_ROPE)
    H_KV = MLA_HEADS * (QK_NOPE + V_HEAD)
    return {
        "x_prompt": nrm((BATCH, SEQ, D), 1.0),
        "x_sample": nrm((DEC_BATCH, DEC_SEQ, D), 1.0),
        "c": nrm((DEC_BATCH, D), 1.0),
        "state_gla": nrm((DEC_BATCH, N_GLA, 2, GLA_HEADS, GLA_DK, GLA_DV), 1.0),
        "cache_ckv": nrm((DEC_BATCH, N_MLA, PAST_LEN, KV_LORA), 1.0),
        "cache_krope": nrm((DEC_BATCH, N_MLA, PAST_LEN, QK_ROPE), 1.0),
        "c_ctx": nrm((D,), 1.0),
        "ada_w": nrm((DEPTH, D, 6 * D), 0.5 * D ** -0.5),
        "ada_b": nrm((DEPTH, 6 * D), 0.01),
        "norm_mix": gain((DEPTH, D)),
        "norm_ffn": gain((DEPTH, D)),
        "norm_final": gain((D,)),
        "gla_w_in": nrm((N_GLA, D, GLA_IN), D ** -0.5),
        "gla_w_gate_up": nrm((N_GLA, 2, GATE_RANK, GLA_DK_TOT), GATE_RANK ** -0.5),
        "gla_b_gate": nrm((N_GLA, 2, GLA_DK_TOT), 0.1),
        "gla_norm": gain((N_GLA, GLA_DV)),
        "gla_w_out": nrm((N_GLA, GLA_DV_TOT, D), GLA_DV_TOT ** -0.5),
        "mla_w_down": nrm((N_MLA, D, Q_LORA + KV_LORA + QK_ROPE), D ** -0.5),
        "mla_q_norm": gain((N_MLA, Q_LORA)),
        "mla_kv_norm": gain((N_MLA, KV_LORA)),
        "mla_w_uq": nrm((N_MLA, Q_LORA, H_QK), Q_LORA ** -0.5),
        "mla_w_ukv": nrm((N_MLA, KV_LORA, H_KV), KV_LORA ** -0.5),
        "mla_w_out": nrm((N_MLA, MLA_HEADS * V_HEAD, D), (MLA_HEADS * V_HEAD) ** -0.5),
        "ffn_w_gate_up": nrm((N_GLA, D, 2 * D_FF), D ** -0.5),
        "ffn_w_down": nrm((N_GLA, D_FF, D), D_FF ** -0.5),
        "moe_router": nrm((N_MLA, D, N_EXPERTS), D ** -0.5),
        "moe_w_gate_up": nrm((N_MLA, N_EXPERTS, D, 2 * D_EXPERT), D ** -0.5),
        "moe_w_down": nrm((N_MLA, N_EXPERTS, D_EXPERT, D), D_EXPERT ** -0.5),
    }


def reference(x_prompt, x_sample, c, state_gla, cache_ckv, cache_krope, c_ctx,
              ada_w, ada_b, norm_mix, norm_ffn, norm_final,
              gla_w_in, gla_w_gate_up, gla_b_gate, gla_norm, gla_w_out,
              mla_w_down, mla_q_norm, mla_kv_norm, mla_w_uq, mla_w_ukv, mla_w_out,
              ffn_w_gate_up, ffn_w_down, moe_router, moe_w_gate_up, moe_w_down):
    b_p = x_prompt.shape[0]
    rows = x_sample.shape[1] // GRID_W
    row = jnp.repeat(jnp.arange(rows, dtype=jnp.float32), GRID_W)
    col = jnp.tile(jnp.arange(GRID_W, dtype=jnp.float32), rows)
    cos, sin = axial_rope(row, col)
    zero_state = jnp.zeros((b_p, 2, GLA_HEADS, GLA_DK, GLA_DV), x_prompt.dtype)
    xp, xs = x_prompt, x_sample
    new_gla, new_ckv, new_kr = [], [], []
    for i in range(DEPTH):
        j = i // N_MIXERS
        p_sh1, p_sc1, p_g1, p_sh2, p_sc2, p_g2 = ada_mod(c_ctx, ada_w[i], ada_b[i])
        s_sh1, s_sc1, s_g1, s_sh2, s_sc2, s_g2 = ada_mod(c, ada_w[i], ada_b[i])
        hp = modulate(xp, norm_mix[i], p_sh1, p_sc1)
        hs = modulate(xs, norm_mix[i], s_sh1, s_sc1)
        if i % N_MIXERS == 0:
            gla_w = (gla_w_in[j], gla_w_gate_up[j], gla_b_gate[j], gla_norm[j], gla_w_out[j])
            yp, st = gla_mixer(hp, zero_state, *gla_w)
            ys, _ = gla_mixer(hs, state_gla[:, j], *gla_w)
            new_gla.append(st)
        else:
            mla_w = (mla_w_down[j], mla_q_norm[j], mla_kv_norm[j], mla_w_uq[j], mla_w_ukv[j], mla_w_out[j])
            yp, ckv, kr = mla_context(hp, *mla_w)
            ys = mla_latent(hs, cache_ckv[:, j], cache_krope[:, j], cos, sin, *mla_w)
            new_ckv.append(ckv)
            new_kr.append(kr)
        xp = xp + p_g1 * yp
        xs = xs + s_g1 * ys
        hp = modulate(xp, norm_ffn[i], p_sh2, p_sc2)
        hs = modulate(xs, norm_ffn[i], s_sh2, s_sc2)
        if i % 2 == 0:
            yp = swiglu(hp, ffn_w_gate_up[j], ffn_w_down[j])
            ys = swiglu(hs, ffn_w_gate_up[j], ffn_w_down[j])
        else:
            yp = moe_swiglu(hp, moe_router[j], moe_w_gate_up[j], moe_w_down[j])
            ys = moe_swiglu(hs, moe_router[j], moe_w_gate_up[j], moe_w_down[j])
        xp = xp + p_g2 * yp
        xs = xs + s_g2 * ys
    y_prompt = rms_norm(xp, norm_final)
    y_sample = rms_norm(xs, norm_final)
    state_gla_new = jnp.stack(new_gla, axis=1)
    cache_ckv_new = jnp.stack(new_ckv, axis=1)
    cache_krope_new = jnp.stack(new_kr, axis=1)
    return (y_prompt, y_sample, state_gla_new, cache_ckv_new, cache_krope_new)
```

```python
import functools

import jax
import jax.numpy as jnp
import numpy as np
from jax import lax
from jax.experimental import pallas as pl
from jax.experimental.pallas import tpu as pltpu

D_MODEL = 1024
BATCH = 16
SEQ = 256
DEPTH = 4
DEC_BATCH = 4
DEC_SEQ = 2048
PAST_LEN = 512
GRID_W = 64
GLA_HEADS = 4
GLA_DK = 128
GLA_DV = 256
GLA_DK_TOT = GLA_HEADS * GLA_DK
GLA_DV_TOT = GLA_HEADS * GLA_DV
GATE_RANK = 16
GATE_TEMP = 16.0
MLA_HEADS = 8
Q_LORA = 384
KV_LORA = 256
QK_NOPE = 128
QK_ROPE = 64
V_HEAD = 128
MLA_SCALE = (QK_NOPE + QK_ROPE) ** -0.5
ROPE_BASE = 10000.0
D_FF = 2816
N_EXPERTS = 8
D_EXPERT = 3584
EPS = 1e-6

N_PROMPT = BATCH * SEQ
N_SAMPLE = DEC_BATCH * DEC_SEQ
N_TOK = N_PROMPT + N_SAMPLE
N_COND = 8
LANES = 128
GLA_PROJ_W = 3200
GLA_CHUNK = 64
VMEM_LIMIT = 56 * 1024 * 1024

F32 = jnp.float32
BF16 = jnp.bfloat16
HIGHEST = lax.Precision.HIGHEST
NT_DIMS = (((1,), (1,)), ((), ()))
TN_DIMS = (((0,), (0,)), ((), ()))


def _params(*sem):
    return pltpu.CompilerParams(dimension_semantics=sem, vmem_limit_bytes=VMEM_LIMIT)


def _cond_row(i, tm):
    return jnp.where(i * tm < N_PROMPT, 0, 1 + jnp.maximum(i * tm - N_PROMPT, 0) // DEC_SEQ)


def _mod_spec(which, tm, tn=D_MODEL, col=None):
    if col is None:
        return pl.BlockSpec((None, None, 1, tn), lambda i, *_: (_cond_row(i, tm), which, 0, 0))
    return pl.BlockSpec((None, None, 1, tn), lambda i, j: (_cond_row(i, tm), which, 0, j))


def _silu(x):
    return x / (1.0 + jnp.exp(-x))


def _rms(x):
    return x * lax.rsqrt(jnp.mean(x * x, axis=-1, keepdims=True) + EPS)


def _ada_kernel(c_ref, w_ref, b_ref, o_ref):
    o_ref[...] = jnp.dot(_silu(c_ref[...]), w_ref[...], precision=HIGHEST,
                         preferred_element_type=F32) + b_ref[...]


def _ada_mods(cond, ada_w, ada_b):
    tn = 1536
    return pl.pallas_call(
        _ada_kernel,
        grid=(DEPTH, 6 * D_MODEL // tn),
        in_specs=[pl.BlockSpec((N_COND, D_MODEL), lambda l, j: (0, 0)),
                  pl.BlockSpec((None, D_MODEL, tn), lambda l, j: (l, 0, j)),
                  pl.BlockSpec((None, 1, tn), lambda l, j: (l, 0, j))],
        out_specs=pl.BlockSpec((None, N_COND, tn), lambda l, j: (l, 0, j)),
        out_shape=jax.ShapeDtypeStruct((DEPTH, N_COND, 6 * D_MODEL), F32),
        compiler_params=_params("arbitrary", "arbitrary"),
        name="ada_mods",
    )(cond, ada_w, ada_b.reshape(DEPTH, 1, 6 * D_MODEL))


def _modulate_kernel(x_ref, g_ref, sh_ref, sc_ref, o_ref):
    y = _rms(x_ref[...]) * g_ref[...]
    o_ref[...] = (y * (1.0 + sc_ref[...]) + sh_ref[...]).astype(o_ref.dtype)


def _modulate(x, g, mods, shift_idx, scale_idx, tm=1024):
    return pl.pallas_call(
        _modulate_kernel,
        grid=(N_TOK // tm,),
        in_specs=[pl.BlockSpec((tm, D_MODEL), lambda i: (i, 0)),
                  pl.BlockSpec((1, D_MODEL), lambda i: (0, 0)),
                  _mod_spec(shift_idx, tm), _mod_spec(scale_idx, tm)],
        out_specs=pl.BlockSpec((tm, D_MODEL), lambda i: (i, 0)),
        out_shape=jax.ShapeDtypeStruct((N_TOK, D_MODEL), BF16),
        compiler_params=_params("arbitrary"),
        name="modulate",
    )(x, g.reshape(1, D_MODEL), mods, mods)


def _modulate_router_kernel(x_ref, g_ref, sh_ref, sc_ref, r_ref, o_ref, gates_ref):
    y = _rms(x_ref[...]) * g_ref[...]
    h = y * (1.0 + sc_ref[...]) + sh_ref[...]
    o_ref[...] = h.astype(o_ref.dtype)
    logits = jnp.dot(h, r_ref[...], precision=HIGHEST, preferred_element_type=F32)
    lane = lax.broadcasted_iota(jnp.int32, logits.shape, 1)
    l1 = jnp.where(lane < N_EXPERTS, logits, -jnp.inf)
    m1 = jnp.max(l1, axis=1, keepdims=True)
    i1 = jnp.min(jnp.where(l1 == m1, lane, LANES), axis=1, keepdims=True)
    l2 = jnp.where(lane == i1, -jnp.inf, l1)
    m2 = jnp.max(l2, axis=1, keepdims=True)
    i2 = jnp.min(jnp.where(l2 == m2, lane, LANES), axis=1, keepdims=True)
    e = jnp.exp(m2 - m1)
    w1 = 1.0 / (1.0 + e)
    w2 = e / (1.0 + e)
    gates_ref[...] = jnp.where(lane == i1, w1, 0.0) + jnp.where(lane == i2, w2, 0.0)


def _modulate_router(x, g, mods, shift_idx, scale_idx, router_pad, tm=512):
    return pl.pallas_call(
        _modulate_router_kernel,
        grid=(N_TOK // tm,),
        in_specs=[pl.BlockSpec((tm, D_MODEL), lambda i: (i, 0)),
                  pl.BlockSpec((1, D_MODEL), lambda i: (0, 0)),
                  _mod_spec(shift_idx, tm), _mod_spec(scale_idx, tm),
                  pl.BlockSpec((D_MODEL, LANES), lambda i: (0, 0))],
        out_specs=[pl.BlockSpec((tm, D_MODEL), lambda i: (i, 0)),
                   pl.BlockSpec((tm, LANES), lambda i: (i, 0))],
        out_shape=[jax.ShapeDtypeStruct((N_TOK, D_MODEL), BF16),
                   jax.ShapeDtypeStruct((N_TOK, LANES), F32)],
        compiler_params=_params("arbitrary"),
        name="modulate_router",
    )(x, g.reshape(1, D_MODEL), mods, mods, router_pad)


def _mm_kernel(a_ref, w_ref, o_ref):
    o_ref[...] = jnp.dot(a_ref[...], w_ref[...], preferred_element_type=F32).astype(o_ref.dtype)


def _matmul(a, w, out_dtype, tm, tn):
    m, k = a.shape
    n = w.shape[1]
    return pl.pallas_call(
        _mm_kernel,
        grid=(m // tm, n // tn),
        in_specs=[pl.BlockSpec((tm, k), lambda i, j: (i, 0)),
                  pl.BlockSpec((k, tn), lambda i, j: (0, j))],
        out_specs=pl.BlockSpec((tm, tn), lambda i, j: (i, j)),
        out_shape=jax.ShapeDtypeStruct((m, n), out_dtype),
        compiler_params=_params("arbitrary", "arbitrary"),
        name="matmul",
    )(a, w)


def _swiglu_kernel(a_ref, wg_ref, wu_ref, o_ref):
    a = a_ref[...]
    g = jnp.dot(a, wg_ref[...], preferred_element_type=F32)
    u = jnp.dot(a, wu_ref[...], preferred_element_type=F32)
    o_ref[...] = (_silu(g) * u).astype(o_ref.dtype)


def _swiglu_up(a, w_gu, tm, tn):
    m, k = a.shape
    f = w_gu.shape[1] // 2
    nj = f // tn
    return pl.pallas_call(
        _swiglu_kernel,
        grid=(m // tm, nj),
        in_specs=[pl.BlockSpec((tm, k), lambda i, j: (i, 0)),
                  pl.BlockSpec((k, tn), lambda i, j: (0, j)),
                  pl.BlockSpec((k, tn), lambda i, j: (0, j + nj))],
        out_specs=pl.BlockSpec((tm, tn), lambda i, j: (i, j)),
        out_shape=jax.ShapeDtypeStruct((m, f), BF16),
        compiler_params=_params("arbitrary", "arbitrary"),
        name="swiglu_up",
    )(a, w_gu, w_gu)


def _mm_residual_kernel(a_ref, w_ref, x_ref, g_ref, o_ref):
    y = jnp.dot(a_ref[...], w_ref[...], preferred_element_type=F32)
    o_ref[...] = x_ref[...] + g_ref[...] * y


def _matmul_residual(a, w, x, mods, gate_idx, tm=1024, tn=512):
    k = a.shape[1]
    return pl.pallas_call(
        _mm_residual_kernel,
        grid=(N_TOK // tm, D_MODEL // tn),
        in_specs=[pl.BlockSpec((tm, k), lambda i, j: (i, 0)),
                  pl.BlockSpec((k, tn), lambda i, j: (0, j)),
                  pl.BlockSpec((tm, tn), lambda i, j: (i, j)),
                  _mod_spec(gate_idx, tm, tn, col=True)],
        out_specs=pl.BlockSpec((tm, tn), lambda i, j: (i, j)),
        out_shape=jax.ShapeDtypeStruct((N_TOK, D_MODEL), F32),
        compiler_params=_params("arbitrary", "arbitrary"),
        name="matmul_residual",
    )(a, w, x, mods)


def _mm_expert_acc_kernel(a_ref, w_ref, acc_ref, gates_ref, o_ref, *, expert):
    y = jnp.dot(a_ref[...], w_ref[...], preferred_element_type=F32)
    o_ref[...] = acc_ref[...] + gates_ref[:, expert:expert + 1] * y


def _matmul_expert_acc(a, w, acc, gates, expert, tm=1024, tn=512):
    k = a.shape[1]
    return pl.pallas_call(
        functools.partial(_mm_expert_acc_kernel, expert=expert),
        grid=(N_TOK // tm, D_MODEL // tn),
        in_specs=[pl.BlockSpec((tm, k), lambda i, j: (i, 0)),
                  pl.BlockSpec((k, tn), lambda i, j: (0, j)),
                  pl.BlockSpec((tm, tn), lambda i, j: (i, j)),
                  pl.BlockSpec((tm, LANES), lambda i, j: (i, 0))],
        out_specs=pl.BlockSpec((tm, tn), lambda i, j: (i, j)),
        out_shape=jax.ShapeDtypeStruct((N_TOK, D_MODEL), F32),
        compiler_params=_params("arbitrary", "arbitrary"),
        name="matmul_expert_acc",
    )(a, w, acc, gates)


def _gated_add_kernel(x_ref, y_ref, g_ref, o_ref):
    o_ref[...] = x_ref[...] + g_ref[...] * y_ref[...]


def _gated_add(x, y, mods, gate_idx, tm=1024):
    return pl.pallas_call(
        _gated_add_kernel,
        grid=(N_TOK // tm,),
        in_specs=[pl.BlockSpec((tm, D_MODEL), lambda i: (i, 0)),
                  pl.BlockSpec((tm, D_MODEL), lambda i: (i, 0)),
                  _mod_spec(gate_idx, tm)],
        out_specs=pl.BlockSpec((tm, D_MODEL), lambda i: (i, 0)),
        out_shape=jax.ShapeDtypeStruct((N_TOK, D_MODEL), F32),
        compiler_params=_params("arbitrary"),
        name="gated_add",
    )(x, y, mods)


def _final_norm_kernel(x_ref, g_ref, o_ref):
    o_ref[...] = _rms(x_ref[...]) * g_ref[...]


def _final_norm(x, g, tm=1024):
    return pl.pallas_call(
        _final_norm_kernel,
        grid=(N_TOK // tm,),
        in_specs=[pl.BlockSpec((tm, D_MODEL), lambda i: (i, 0)),
                  pl.BlockSpec((1, D_MODEL), lambda i: (0, 0))],
        out_specs=pl.BlockSpec((tm, D_MODEL), lambda i: (i, 0)),
        out_shape=jax.ShapeDtypeStruct((N_TOK, D_MODEL), F32),
        compiler_params=_params("arbitrary"),
        name="final_norm",
    )(x, g.reshape(1, D_MODEL))


def _log_sigmoid(x):
    return jnp.minimum(x, 0.0) - jnp.log(1.0 + jnp.exp(-jnp.abs(x)))


def _gla_gate_kernel(gd_ref, w0_ref, w1_ref, b_ref, o_ref):
    gd = gd_ref[...]
    x0 = jnp.dot(gd, w0_ref[...], precision=HIGHEST, preferred_element_type=F32) + b_ref[0:1, :]
    x1 = jnp.dot(gd, w1_ref[...], precision=HIGHEST, preferred_element_type=F32) + b_ref[1:2, :]
    o_ref[:, :GLA_DK_TOT] = _log_sigmoid(x0) * (1.0 / GATE_TEMP)
    o_ref[:, GLA_DK_TOT:] = _log_sigmoid(x1) * (1.0 / GATE_TEMP)


def _gla_gates(proj, w0, w1, b_gate, tm=1024):
    gd_col = (GLA_PROJ_W - LANES) // LANES
    return pl.pallas_call(
        _gla_gate_kernel,
        grid=(N_TOK // tm,),
        in_specs=[pl.BlockSpec((tm, LANES), lambda i: (i, gd_col)),
                  pl.BlockSpec((LANES, GLA_DK_TOT), lambda i: (0, 0)),
                  pl.BlockSpec((LANES, GLA_DK_TOT), lambda i: (0, 0)),
                  pl.BlockSpec((2, GLA_DK_TOT), lambda i: (0, 0))],
        out_specs=pl.BlockSpec((tm, 2 * GLA_DK_TOT), lambda i: (i, 0)),
        out_shape=jax.ShapeDtypeStruct((N_TOK, 2 * GLA_DK_TOT), F32),
        compiler_params=_params("arbitrary"),
        name="gla_gates",
    )(proj, w0, w1, b_gate)


def _gla_direction(q_ref, k_ref, v_ref, la_ref, o_ref, s_ref, d, causal):
    c = GLA_CHUNK
    row = lax.broadcasted_iota(jnp.int32, (c, c), 0)
    col = lax.broadcasted_iota(jnp.int32, (c, c), 1)
    keep = (col <= row) if causal else (col >= row)
    b = jnp.dot(keep.astype(F32), la_ref[...], precision=HIGHEST, preferred_element_type=F32)
    end = c - 1 if causal else 0
    b_end = b[end:end + 1, :]
    q_dec = q_ref[...] * (GLA_DK ** -0.5) * jnp.exp(b)
    k_inv = k_ref[...] * jnp.exp(-b)
    k_end = k_ref[...] * jnp.exp(b_end - b)
    e_end = jnp.exp(b_end)
    eye = (lax.broadcasted_iota(jnp.int32, (GLA_DK, GLA_DK), 0)
           == lax.broadcasted_iota(jnp.int32, (GLA_DK, GLA_DK), 1))
    for h in range(GLA_HEADS):
        ks = slice(h * GLA_DK, (h + 1) * GLA_DK)
        vs = slice(h * GLA_DV, (h + 1) * GLA_DV)
        state = s_ref[d, h]
        qh = q_dec[:, ks].astype(BF16)
        vh = v_ref[:, vs].astype(BF16)
        scores = lax.dot_general(qh, k_inv[:, ks].astype(BF16), NT_DIMS, preferred_element_type=F32)
        scores = jnp.where(keep, scores, 0.0)
        o_ref[:, vs] = (jnp.dot(scores.astype(BF16), vh, preferred_element_type=F32)
                        + jnp.dot(qh, state.astype(BF16), preferred_element_type=F32))
        e_col = jnp.sum(jnp.where(eye, jnp.broadcast_to(e_end[:, ks], (GLA_DK, GLA_DK)), 0.0),
                        axis=1, keepdims=True)
        s_ref[d, h] = state * e_col + lax.dot_general(
            k_end[:, ks].astype(BF16), vh, TN_DIMS, preferred_element_type=F32)


def _gla_scan_kernel(*refs, n_chunks, has_s0, write_state):
    qf, kf, vf, laf, qb, kb, vb, lab = refs[:8]
    rest = list(refs[8:])
    s0_ref = rest.pop(0) if has_s0 else None
    of_ref, ob_ref = rest.pop(0), rest.pop(0)
    sout_ref = rest.pop(0) if write_state else None
    s_ref = rest.pop(0)
    i = pl.program_id(1)

    @pl.when(i == 0)
    def _():
        s_ref[...] = s0_ref[...] if has_s0 else jnp.zeros(s_ref.shape, F32)

    _gla_direction(qf, kf, vf, laf, of_ref, s_ref, 0, True)
    _gla_direction(qb, kb, vb, lab, ob_ref, s_ref, 1, False)

    if write_state:
        @pl.when(i == n_chunks - 1)
        def _():
            sout_ref[...] = s_ref[...]


def _gla_scan(proj, la, row0, n_batch, seq, s0):
    c = GLA_CHUNK
    n = seq // c
    base = row0 // c
    has_s0 = s0 is not None
    write_state = not has_s0
    fwd = lambda b, i: base + b * n + i
    bwd = lambda b, i: base + b * n + (n - 1 - i)

    def specs(blk):
        return [pl.BlockSpec((c, GLA_DK_TOT), lambda b, i: (blk(b, i), 0)),
                pl.BlockSpec((c, GLA_DK_TOT), lambda b, i: (blk(b, i), 1)),
                pl.BlockSpec((c, GLA_DV_TOT), lambda b, i: (blk(b, i), 1))]
    state_block = (None, 2, GLA_HEADS, GLA_DK, GLA_DV)
    state_spec = pl.BlockSpec(state_block, lambda b, i: (b, 0, 0, 0, 0))
    in_specs = (specs(fwd) + [pl.BlockSpec((c, GLA_DK_TOT), lambda b, i: (fwd(b, i), 0))]
                + specs(bwd) + [pl.BlockSpec((c, GLA_DK_TOT), lambda b, i: (bwd(b, i), 1))])
    args = [proj, proj, proj, la, proj, proj, proj, la]
    if has_s0:
        in_specs.append(state_spec)
        args.append(s0)
    out_rows = n_batch * seq
    out_specs = [pl.BlockSpec((c, GLA_DV_TOT), lambda b, i: (b * n + i, 0)),
                 pl.BlockSpec((c, GLA_DV_TOT), lambda b, i: (b * n + (n - 1 - i), 0))]
    out_shape = [jax.ShapeDtypeStruct((out_rows, GLA_DV_TOT), F32)] * 2
    if write_state:
        out_specs.append(state_spec)
        out_shape.append(jax.ShapeDtypeStruct((n_batch, 2, GLA_HEADS, GLA_DK, GLA_DV), F32))
    return pl.pallas_call(
        functools.partial(_gla_scan_kernel, n_chunks=n, has_s0=has_s0, write_state=write_state),
        grid=(n_batch, n),
        in_specs=in_specs,
        out_specs=out_specs,
        out_shape=out_shape,
        scratch_shapes=[pltpu.VMEM((2, GLA_HEADS, GLA_DK, GLA_DV), F32)],
        compiler_params=_params("arbitrary", "arbitrary"),
        name="gla_scan",
    )(*args)


def _gla_post_kernel(of_ref, ob_ref, r_ref, g_ref, o_ref):
    o = of_ref[...] + ob_ref[...]
    normed = jnp.concatenate(
        [_rms(o[:, h * GLA_DV:(h + 1) * GLA_DV]) for h in range(GLA_HEADS)], axis=1)
    o_ref[...] = (normed * g_ref[...] * _silu(r_ref[...])).astype(o_ref.dtype)


def _gla_post(o_f, o_b, proj, g_tiled, tm=1024):
    return pl.pallas_call(
        _gla_post_kernel,
        grid=(N_TOK // tm,),
        in_specs=[pl.BlockSpec((tm, GLA_DV_TOT), lambda i: (i, 0)),
                  pl.BlockSpec((tm, GLA_DV_TOT), lambda i: (i, 0)),
                  pl.BlockSpec((tm, GLA_DV_TOT), lambda i: (i, 2)),
                  pl.BlockSpec((1, GLA_DV_TOT), lambda i: (0, 0))],
        out_specs=pl.BlockSpec((tm, GLA_DV_TOT), lambda i: (i, 0)),
        out_shape=jax.ShapeDtypeStruct((N_TOK, GLA_DV_TOT), BF16),
        compiler_params=_params("arbitrary"),
        name="gla_post",
    )(o_f, o_b, proj, g_tiled)


def _rope_pair(u, cs):
    t = u * cs
    return t + pltpu.roll(t, QK_ROPE, axis=1)


def _mla_down_kernel(a_ref, w_ref, qn_ref, kvn_ref, cs_ref, cq_ref, ckv_ref, kr_ref, krp_ref):
    acc = jnp.dot(a_ref[...], w_ref[...], preferred_element_type=F32)
    cq_ref[...] = (_rms(acc[:, :Q_LORA]) * qn_ref[...]).astype(cq_ref.dtype)
    ckv_ref[...] = _rms(acc[:, Q_LORA:Q_LORA + KV_LORA]) * kvn_ref[...]
    u = acc[:, Q_LORA + KV_LORA:]
    kr_ref[...] = u
    lane = lax.broadcasted_iota(jnp.int32, u.shape, 1)
    krp_ref[...] = jnp.where(lane < QK_ROPE, _rope_pair(u, cs_ref[...]), 0.0).astype(krp_ref.dtype)


def _mla_down(h, w_ext, q_norm, kv_norm, cs, tm=512):
    wn = w_ext.shape[1]
    row = lambda i: (i, 0)
    fixed = lambda i: (0, 0)
    return pl.pallas_call(
        _mla_down_kernel,
        grid=(N_TOK // tm,),
        in_specs=[pl.BlockSpec((tm, D_MODEL), row), pl.BlockSpec((D_MODEL, wn), fixed),
                  pl.BlockSpec((1, Q_LORA), fixed), pl.BlockSpec((1, KV_LORA), fixed),
                  pl.BlockSpec((tm, LANES), row)],
        out_specs=[pl.BlockSpec((tm, Q_LORA), row), pl.BlockSpec((tm, KV_LORA), row),
                   pl.BlockSpec((tm, LANES), row), pl.BlockSpec((tm, LANES), row)],
        out_shape=[jax.ShapeDtypeStruct((N_TOK, Q_LORA), BF16),
                   jax.ShapeDtypeStruct((N_TOK, KV_LORA), F32),
                   jax.ShapeDtypeStruct((N_TOK, LANES), F32),
                   jax.ShapeDtypeStruct((N_TOK, LANES), BF16)],
        compiler_params=_params("arbitrary"),
        name="mla_down",
    )(h, w_ext, q_norm.reshape(1, Q_LORA), kv_norm.reshape(1, KV_LORA), cs)


def _mla_q_kernel(a_ref, w_ref, cs_ref, o_ref):
    acc = jnp.dot(a_ref[...], w_ref[...], preferred_element_type=F32)
    cs = cs_ref[...]
    for h in range(MLA_HEADS):
        lo = h * 2 * LANES
        o_ref[:, lo:lo + LANES] = acc[:, lo:lo + LANES].astype(o_ref.dtype)
        o_ref[:, lo + LANES:lo + 2 * LANES] = _rope_pair(
            acc[:, lo + LANES:lo + 2 * LANES], cs).astype(o_ref.dtype)


def _mla_q(cq, w_ext, cs, tm=512):
    wn = w_ext.shape[1]
    return pl.pallas_call(
        _mla_q_kernel,
        grid=(N_TOK // tm,),
        in_specs=[pl.BlockSpec((tm, Q_LORA), lambda i: (i, 0)),
                  pl.BlockSpec((Q_LORA, wn), lambda i: (0, 0)),
                  pl.BlockSpec((tm, LANES), lambda i: (i, 0))],
        out_specs=pl.BlockSpec((tm, wn), lambda i: (i, 0)),
        out_shape=jax.ShapeDtypeStruct((N_TOK, wn), BF16),
        compiler_params=_params("arbitrary"),
        name="mla_q",
    )(cq, w_ext, cs)


def _attn_kernel(q_ref, kv_ref, kr_ref, o_ref):
    kv = kv_ref[...]
    k = jnp.concatenate([kv[:, :QK_NOPE], kr_ref[...]], axis=1)
    s = lax.dot_general(q_ref[...], k, NT_DIMS, preferred_element_type=F32) * MLA_SCALE
    p = jnp.exp(s - jnp.max(s, axis=1, keepdims=True))
    l = jnp.sum(p, axis=1, keepdims=True)
    o = jnp.dot(p.astype(BF16), kv[:, QK_NOPE:], preferred_element_type=F32)
    o_ref[...] = (o / l).astype(o_ref.dtype)


def _attention(q, kv, krp, row0, n_batch, seq, n_keys, tq):
    nq = seq // tq
    base = row0 // tq
    hw = QK_NOPE + V_HEAD
    return pl.pallas_call(
        _attn_kernel,
        grid=(n_batch, MLA_HEADS, nq),
        in_specs=[pl.BlockSpec((tq, hw), lambda b, h, i: (base + b * nq + i, h)),
                  pl.BlockSpec((n_keys, hw), lambda b, h, i: (b, h)),
                  pl.BlockSpec((n_keys, LANES), lambda b, h, i: (b, 0))],
        out_specs=pl.BlockSpec((tq, V_HEAD), lambda b, h, i: (b * nq + i, h)),
        out_shape=jax.ShapeDtypeStruct((n_batch * seq, MLA_HEADS * V_HEAD), BF16),
        compiler_params=_params("arbitrary", "arbitrary", "arbitrary"),
        name="mla_attention",
    )(q, kv, krp)


def _rope_tables():
    half = QK_ROPE // 2
    pos = np.arange(DEC_SEQ)
    inv = ROPE_BASE ** (-np.arange(0, half, 2, dtype=np.float64) / half)
    ang = np.concatenate([(pos // GRID_W)[:, None] * inv, (pos % GRID_W)[:, None] * inv], axis=-1)
    ang = ang.astype(np.float32)
    cos = np.repeat(np.cos(ang), 2, axis=1)
    sin = np.repeat(np.sin(ang), 2, axis=1) * np.tile(np.array([-1.0, 1.0], np.float32), half)
    cs = np.concatenate([cos, sin], axis=1).astype(np.float32)
    ident = np.concatenate([np.ones((N_PROMPT, QK_ROPE), np.float32),
                            np.zeros((N_PROMPT, QK_ROPE), np.float32)], axis=1)
    return jnp.asarray(np.concatenate([ident, np.tile(cs, (DEC_BATCH, 1))], axis=0))


def kernel(x_prompt, x_sample, c, state_gla, cache_ckv, cache_krope, c_ctx, ada_w, ada_b, norm_mix, norm_ffn, norm_final, gla_w_in, gla_w_gate_up, gla_b_gate, gla_norm, gla_w_out, mla_w_down, mla_q_norm, mla_kv_norm, mla_w_uq, mla_w_ukv, mla_w_out, ffn_w_gate_up, ffn_w_down, moe_router, moe_w_gate_up, moe_w_down):
    x = jnp.concatenate([x_prompt.reshape(N_PROMPT, D_MODEL), x_sample.reshape(N_SAMPLE, D_MODEL)])
    cond = jnp.concatenate([c_ctx[None], c, jnp.zeros((N_COND - 1 - DEC_BATCH, D_MODEL), F32)])
    mods_all = _ada_mods(cond, ada_w, ada_b).reshape(DEPTH, N_COND, 6, 1, D_MODEL)
    cs = _rope_tables()
    swap = np.arange(QK_ROPE) ^ 1

    new_gla, new_ckv, new_kr = [], [], []
    for layer in range(DEPTH):
        j = layer // 2
        mods = mods_all[layer]
        h = _modulate(x, norm_mix[layer], mods, 0, 1)
        if layer % 2 == 0:
            w_in = jnp.pad(gla_w_in[j], ((0, 0), (0, GLA_PROJ_W - gla_w_in.shape[2]))).astype(BF16)
            proj = _matmul(h, w_in, F32, tm=1024, tn=640)
            pad_lo = jnp.zeros((GATE_RANK, GLA_DK_TOT), F32)
            pad_hi = jnp.zeros((LANES - 2 * GATE_RANK, GLA_DK_TOT), F32)
            w0 = jnp.concatenate([gla_w_gate_up[j, 0], pad_lo, pad_hi])
            w1 = jnp.concatenate([pad_lo, gla_w_gate_up[j, 1], pad_hi])
            la = _gla_gates(proj, w0, w1, gla_b_gate[j])
            of_p, ob_p, st = _gla_scan(proj, la, 0, BATCH, SEQ, None)
            of_s, ob_s = _gla_scan(proj, la, N_PROMPT, DEC_BATCH, DEC_SEQ, state_gla[:, j])
            new_gla.append(st)
            mixed = _gla_post(jnp.concatenate([of_p, of_s]), jnp.concatenate([ob_p, ob_s]), proj,
                              jnp.tile(gla_norm[j], GLA_HEADS).reshape(1, GLA_DV_TOT))
            x = _matmul_residual(mixed, gla_w_out[j].astype(BF16), x, mods, 2)
        else:
            wd = mla_w_down[j]
            w_down = jnp.concatenate([wd, wd[:, Q_LORA + KV_LORA:][:, swap]], axis=1).astype(BF16)
            cq, ckv, kr, krp = _mla_down(h, w_down, mla_q_norm[j], mla_kv_norm[j], cs)
            wq = mla_w_uq[j].reshape(Q_LORA, MLA_HEADS, QK_NOPE + QK_ROPE)
            w_uq = jnp.concatenate([wq, wq[:, :, QK_NOPE:][:, :, swap]], axis=2)
            q = _mla_q(cq, w_uq.reshape(Q_LORA, -1).astype(BF16), cs)
            w_ukv = mla_w_ukv[j].astype(BF16)
            ckv_b = ckv.astype(BF16)
            kv_p = _matmul(ckv_b[:N_PROMPT], w_ukv, BF16, tm=1024, tn=1024)
            o_p = _attention(q, kv_p, krp[:N_PROMPT], 0, BATCH, SEQ, SEQ, tq=SEQ)
            n_keys = PAST_LEN + DEC_SEQ
            ckv_keys = jnp.concatenate(
                [cache_ckv[:, j].astype(BF16), ckv_b[N_PROMPT:].reshape(DEC_BATCH, DEC_SEQ, KV_LORA)],
                axis=1).reshape(DEC_BATCH * n_keys, KV_LORA)
            kr_ctx = jnp.pad(cache_krope[:, j], ((0, 0), (0, 0), (0, LANES - QK_ROPE))).astype(BF16)
            krp_keys = jnp.concatenate(
                [kr_ctx, krp[N_PROMPT:].reshape(DEC_BATCH, DEC_SEQ, LANES)],
                axis=1).reshape(DEC_BATCH * n_keys, LANES)
            kv_s = _matmul(ckv_keys, w_ukv, BF16, tm=1024, tn=1024)
            o_s = _attention(q, kv_s, krp_keys, N_PROMPT, DEC_BATCH, DEC_SEQ, n_keys, tq=512)
            new_ckv.append(ckv[:N_PROMPT].reshape(BATCH, SEQ, KV_LORA))
            new_kr.append(kr[:N_PROMPT, :QK_ROPE].reshape(BATCH, SEQ, QK_ROPE))
            x = _matmul_residual(jnp.concatenate([o_p, o_s]), mla_w_out[j].astype(BF16), x, mods, 2)

        if layer % 2 == 0:
            h = _modulate(x, norm_ffn[layer], mods, 3, 4)
            act = _swiglu_up(h, ffn_w_gate_up[j].astype(BF16), tm=1024, tn=1408)
            x = _matmul_residual(act, ffn_w_down[j].astype(BF16), x, mods, 5)
        else:
            router_pad = jnp.pad(moe_router[j], ((0, 0), (0, LANES - N_EXPERTS)))
            h, gates = _modulate_router(x, norm_ffn[layer], mods, 3, 4, router_pad)
            y = jnp.zeros((N_TOK, D_MODEL), F32)
            for e in range(N_EXPERTS):
                act = _swiglu_up(h, moe_w_gate_up[j, e].astype(BF16), tm=1024, tn=896)
                y = _matmul_expert_acc(act, moe_w_down[j, e].astype(BF16), y, gates, e)
            x = _gated_add(x, y, mods, 5)

    y = _final_norm(x, norm_final)
    return (y[:N_PROMPT].reshape(BATCH, SEQ, D_MODEL),
            y[N_PROMPT:].reshape(DEC_BATCH, DEC_SEQ, D_MODEL),
            jnp.stack(new_gla, axis=1),
            jnp.stack(new_ckv, axis=1),
            jnp.stack(new_kr, axis=1))
```

```python
import functools

import jax
import jax.numpy as jnp
import numpy as np
from jax import lax
from jax.experimental import pallas as pl
from jax.experimental.pallas import tpu as pltpu

D_MODEL = 1024
BATCH = 16
SEQ = 256
DEPTH = 4
DEC_BATCH = 4
DEC_SEQ = 2048
PAST_LEN = 512
GRID_W = 64
GLA_HEADS = 4
GLA_DK = 128
GLA_DV = 256
GLA_DK_TOT = GLA_HEADS * GLA_DK
GLA_DV_TOT = GLA_HEADS * GLA_DV
GATE_RANK = 16
GATE_TEMP = 16.0
MLA_HEADS = 8
Q_LORA = 384
KV_LORA = 256
QK_NOPE = 128
QK_ROPE = 64
V_HEAD = 128
MLA_SCALE = (QK_NOPE + QK_ROPE) ** -0.5
ROPE_BASE = 10000.0
D_FF = 2816
N_EXPERTS = 8
D_EXPERT = 3584
EPS = 1e-6

N_PROMPT = BATCH * SEQ
N_SAMPLE = DEC_BATCH * DEC_SEQ
N_TOK = N_PROMPT + N_SAMPLE
N_COND = 8
LANES = 128
GLA_PROJ_W = 3200
GLA_CHUNK = 64
MOE_BLOCK = 1024
MOE_TILE = 128
LOG2E = 1.4426950408889634
VMEM_LIMIT = 56 * 1024 * 1024

F32 = jnp.float32
BF16 = jnp.bfloat16
HIGHEST = lax.Precision.HIGHEST
NT_DIMS = (((1,), (1,)), ((), ()))
TN_DIMS = (((0,), (0,)), ((), ()))


def _params(*sem):
    return pltpu.CompilerParams(dimension_semantics=sem, vmem_limit_bytes=VMEM_LIMIT)


def _cond_row(i, tm):
    return jnp.where(i * tm < N_PROMPT, 0, 1 + jnp.maximum(i * tm - N_PROMPT, 0) // DEC_SEQ)


def _mod_spec(which, tm, tn=D_MODEL, col=None):
    if col is None:
        return pl.BlockSpec((None, None, 1, tn), lambda i, *_: (_cond_row(i, tm), which, 0, 0))
    return pl.BlockSpec((None, None, 1, tn), lambda i, j: (_cond_row(i, tm), which, 0, j))


def _silu(x):
    return x / (1.0 + jnp.exp(-x))


def _rms(x):
    return x * lax.rsqrt(jnp.mean(x * x, axis=-1, keepdims=True) + EPS)


def _ada_kernel(c_ref, w_ref, b_ref, o_ref):
    o_ref[...] = jnp.dot(_silu(c_ref[...]), w_ref[...], precision=HIGHEST,
                         preferred_element_type=F32) + b_ref[...]


def _ada_mods(cond, ada_w, ada_b):
    tn = 1536
    return pl.pallas_call(
        _ada_kernel,
        grid=(DEPTH, 6 * D_MODEL // tn),
        in_specs=[pl.BlockSpec((N_COND, D_MODEL), lambda l, j: (0, 0)),
                  pl.BlockSpec((None, D_MODEL, tn), lambda l, j: (l, 0, j)),
                  pl.BlockSpec((None, 1, tn), lambda l, j: (l, 0, j))],
        out_specs=pl.BlockSpec((None, N_COND, tn), lambda l, j: (l, 0, j)),
        out_shape=jax.ShapeDtypeStruct((DEPTH, N_COND, 6 * D_MODEL), F32),
        compiler_params=_params("arbitrary", "arbitrary"),
        name="ada_mods",
    )(cond, ada_w, ada_b.reshape(DEPTH, 1, 6 * D_MODEL))


def _modulate_kernel(x_ref, g_ref, sh_ref, sc_ref, o_ref):
    y = _rms(x_ref[...]) * g_ref[...]
    o_ref[...] = (y * (1.0 + sc_ref[...]) + sh_ref[...]).astype(o_ref.dtype)


def _modulate(x, g, mods, shift_idx, scale_idx, tm=1024):
    return pl.pallas_call(
        _modulate_kernel,
        grid=(N_TOK // tm,),
        in_specs=[pl.BlockSpec((tm, D_MODEL), lambda i: (i, 0)),
                  pl.BlockSpec((1, D_MODEL), lambda i: (0, 0)),
                  _mod_spec(shift_idx, tm), _mod_spec(scale_idx, tm)],
        out_specs=pl.BlockSpec((tm, D_MODEL), lambda i: (i, 0)),
        out_shape=jax.ShapeDtypeStruct((N_TOK, D_MODEL), BF16),
        compiler_params=_params("arbitrary"),
        name="modulate",
    )(x, g.reshape(1, D_MODEL), mods, mods)


def _modulate_router_kernel(x_ref, g_ref, sh_ref, sc_ref, r_ref, o_ref, gates_ref, slot_ref, cnt_ref):
    y = _rms(x_ref[...]) * g_ref[...]
    h = y * (1.0 + sc_ref[...]) + sh_ref[...]
    o_ref[...] = h.astype(o_ref.dtype)
    logits = jnp.dot(h, r_ref[...], precision=HIGHEST, preferred_element_type=F32)
    lane = lax.broadcasted_iota(jnp.int32, logits.shape, 1)
    l1 = jnp.where(lane < N_EXPERTS, logits, -jnp.inf)
    m1 = jnp.max(l1, axis=1, keepdims=True)
    i1 = jnp.min(jnp.where(l1 == m1, lane, LANES), axis=1, keepdims=True)
    l2 = jnp.where(lane == i1, -jnp.inf, l1)
    m2 = jnp.max(l2, axis=1, keepdims=True)
    i2 = jnp.min(jnp.where(l2 == m2, lane, LANES), axis=1, keepdims=True)
    e = jnp.exp(m2 - m1)
    w1 = 1.0 / (1.0 + e)
    w2 = e / (1.0 + e)
    gates_ref[...] = jnp.where(lane == i1, w1, 0.0) + jnp.where(lane == i2, w2, 0.0)
    sel = (lane == i1) | (lane == i2)
    tb = logits.shape[0]
    earlier = (lax.broadcasted_iota(jnp.int32, (tb, tb), 1)
               < lax.broadcasted_iota(jnp.int32, (tb, tb), 0)).astype(BF16)
    before = jnp.dot(earlier, sel.astype(BF16), preferred_element_type=F32)
    slot_ref[...] = jnp.where(sel, before + 1.0, 0.0)
    cnt_ref[...] = jnp.broadcast_to(jnp.sum(sel.astype(F32), axis=0, keepdims=True), cnt_ref.shape)


def _modulate_router(x, g, mods, shift_idx, scale_idx, router_pad):
    tm = MOE_BLOCK
    nb = N_TOK // tm
    return pl.pallas_call(
        _modulate_router_kernel,
        grid=(nb,),
        in_specs=[pl.BlockSpec((tm, D_MODEL), lambda i: (i, 0)),
                  pl.BlockSpec((1, D_MODEL), lambda i: (0, 0)),
                  _mod_spec(shift_idx, tm), _mod_spec(scale_idx, tm),
                  pl.BlockSpec((D_MODEL, LANES), lambda i: (0, 0))],
        out_specs=[pl.BlockSpec((tm, D_MODEL), lambda i: (i, 0)),
                   pl.BlockSpec((tm, LANES), lambda i: (i, 0)),
                   pl.BlockSpec((tm, LANES), lambda i: (i, 0)),
                   pl.BlockSpec((None, 8, LANES), lambda i: (i, 0, 0))],
        out_shape=[jax.ShapeDtypeStruct((N_TOK, D_MODEL), BF16),
                   jax.ShapeDtypeStruct((N_TOK, LANES), F32),
                   jax.ShapeDtypeStruct((N_TOK, LANES), F32),
                   jax.ShapeDtypeStruct((nb, 8, LANES), F32)],
        compiler_params=_params("arbitrary"),
        name="modulate_router",
    )(x, g.reshape(1, D_MODEL), mods, mods, router_pad)


def _moe_expert_kernel(cnt_ref, h_ref, gates_ref, slot_ref, wgu_ref, wd_ref, *rest, expert, first):
    y_ref = rest[-1]
    if first:
        y_ref[...] = jnp.zeros(y_ref.shape, F32)
    else:
        y_ref[...] = rest[0][...]
    tb = h_ref.shape[0]
    gate_col = gates_ref[:, expert:expert + 1]
    slot_col = slot_ref[:, expert:expert + 1]
    lane = lax.broadcasted_iota(jnp.int32, (tb, MOE_TILE), 1).astype(F32)
    n_tiles = (cnt_ref[pl.program_id(0) * N_EXPERTS + expert] + (MOE_TILE - 1)) // MOE_TILE
    cw = 256

    def tile(k, carry):
        onehot = (slot_col == lane + (k * MOE_TILE + 1).astype(F32)).astype(BF16)
        hg = lax.dot_general(onehot, h_ref[...], TN_DIMS, preferred_element_type=F32).astype(BF16)
        gu = jnp.dot(hg, wgu_ref[...], preferred_element_type=F32)
        act = (_silu(gu[:, :D_EXPERT]) * gu[:, D_EXPERT:]).astype(BF16)
        out = jnp.dot(act, wd_ref[...], preferred_element_type=F32).astype(BF16)
        for c0 in range(0, D_MODEL, cw):
            y_ref[:, c0:c0 + cw] += gate_col * jnp.dot(onehot, out[:, c0:c0 + cw],
                                                       preferred_element_type=F32)
        return carry

    lax.fori_loop(0, n_tiles, tile, 0)


def _moe_expert(h, gates, slots, counts, w_gu, w_down, y, expert):
    tb = MOE_BLOCK
    first = y is None
    blk = lambda b, cnt: (b, 0)
    fixed = lambda b, cnt: (0, 0)
    in_specs = [pl.BlockSpec((tb, D_MODEL), blk), pl.BlockSpec((tb, LANES), blk),
                pl.BlockSpec((tb, LANES), blk),
                pl.BlockSpec((D_MODEL, 2 * D_EXPERT), fixed, pipeline_mode=pl.Buffered(1)),
                pl.BlockSpec((D_EXPERT, D_MODEL), fixed, pipeline_mode=pl.Buffered(1))]
    args = [h, gates, slots, w_gu, w_down]
    if not first:
        in_specs.append(pl.BlockSpec((tb, D_MODEL), blk))
        args.append(y)
    return pl.pallas_call(
        functools.partial(_moe_expert_kernel, expert=expert, first=first),
        grid_spec=pltpu.PrefetchScalarGridSpec(
            num_scalar_prefetch=1, grid=(N_TOK // tb,), in_specs=in_specs,
            out_specs=pl.BlockSpec((tb, D_MODEL), blk)),
        out_shape=jax.ShapeDtypeStruct((N_TOK, D_MODEL), F32),
        compiler_params=_params("arbitrary"),
        name="moe_expert",
    )(counts, *args)


def _mm_kernel(a_ref, w_ref, o_ref):
    o_ref[...] = jnp.dot(a_ref[...], w_ref[...], preferred_element_type=F32).astype(o_ref.dtype)


def _matmul(a, w, out_dtype, tm, tn):
    m, k = a.shape
    n = w.shape[1]
    return pl.pallas_call(
        _mm_kernel,
        grid=(m // tm, n // tn),
        in_specs=[pl.BlockSpec((tm, k), lambda i, j: (i, 0)),
                  pl.BlockSpec((k, tn), lambda i, j: (0, j))],
        out_specs=pl.BlockSpec((tm, tn), lambda i, j: (i, j)),
        out_shape=jax.ShapeDtypeStruct((m, n), out_dtype),
        compiler_params=_params("arbitrary", "arbitrary"),
        name="matmul",
    )(a, w)


def _swiglu_kernel(a_ref, wg_ref, wu_ref, o_ref):
    a = a_ref[...]
    g = jnp.dot(a, wg_ref[...], preferred_element_type=F32)
    u = jnp.dot(a, wu_ref[...], preferred_element_type=F32)
    o_ref[...] = (_silu(g) * u).astype(o_ref.dtype)


def _swiglu_up(a, w_gu, tm, tn):
    m, k = a.shape
    f = w_gu.shape[1] // 2
    nj = f // tn
    return pl.pallas_call(
        _swiglu_kernel,
        grid=(m // tm, nj),
        in_specs=[pl.BlockSpec((tm, k), lambda i, j: (i, 0)),
                  pl.BlockSpec((k, tn), lambda i, j: (0, j)),
                  pl.BlockSpec((k, tn), lambda i, j: (0, j + nj))],
        out_specs=pl.BlockSpec((tm, tn), lambda i, j: (i, j)),
        out_shape=jax.ShapeDtypeStruct((m, f), BF16),
        compiler_params=_params("arbitrary", "arbitrary"),
        name="swiglu_up",
    )(a, w_gu, w_gu)


def _mm_residual_kernel(a_ref, w_ref, x_ref, g_ref, o_ref):
    y = jnp.dot(a_ref[...], w_ref[...], preferred_element_type=F32)
    o_ref[...] = x_ref[...] + g_ref[...] * y


def _matmul_residual(a, w, x, mods, gate_idx, tm=1024, tn=512):
    k = a.shape[1]
    return pl.pallas_call(
        _mm_residual_kernel,
        grid=(N_TOK // tm, D_MODEL // tn),
        in_specs=[pl.BlockSpec((tm, k), lambda i, j: (i, 0)),
                  pl.BlockSpec((k, tn), lambda i, j: (0, j)),
                  pl.BlockSpec((tm, tn), lambda i, j: (i, j)),
                  _mod_spec(gate_idx, tm, tn, col=True)],
        out_specs=pl.BlockSpec((tm, tn), lambda i, j: (i, j)),
        out_shape=jax.ShapeDtypeStruct((N_TOK, D_MODEL), F32),
        compiler_params=_params("arbitrary", "arbitrary"),
        name="matmul_residual",
    )(a, w, x, mods)


def _gated_add_kernel(x_ref, y_ref, g_ref, o_ref):
    o_ref[...] = x_ref[...] + g_ref[...] * y_ref[...]


def _gated_add(x, y, mods, gate_idx, tm=1024):
    return pl.pallas_call(
        _gated_add_kernel,
        grid=(N_TOK // tm,),
        in_specs=[pl.BlockSpec((tm, D_MODEL), lambda i: (i, 0)),
                  pl.BlockSpec((tm, D_MODEL), lambda i: (i, 0)),
                  _mod_spec(gate_idx, tm)],
        out_specs=pl.BlockSpec((tm, D_MODEL), lambda i: (i, 0)),
        out_shape=jax.ShapeDtypeStruct((N_TOK, D_MODEL), F32),
        compiler_params=_params("arbitrary"),
        name="gated_add",
    )(x, y, mods)


def _final_norm_kernel(x_ref, g_ref, o_ref):
    o_ref[...] = _rms(x_ref[...]) * g_ref[...]


def _final_norm(x, g, tm=1024):
    return pl.pallas_call(
        _final_norm_kernel,
        grid=(N_TOK // tm,),
        in_specs=[pl.BlockSpec((tm, D_MODEL), lambda i: (i, 0)),
                  pl.BlockSpec((1, D_MODEL), lambda i: (0, 0))],
        out_specs=pl.BlockSpec((tm, D_MODEL), lambda i: (i, 0)),
        out_shape=jax.ShapeDtypeStruct((N_TOK, D_MODEL), F32),
        compiler_params=_params("arbitrary"),
        name="final_norm",
    )(x, g.reshape(1, D_MODEL))


def _log_sigmoid(x):
    return jnp.minimum(x, 0.0) - jnp.log(1.0 + jnp.exp(-jnp.abs(x)))


def _gla_gate_kernel(gd_ref, w0_ref, w1_ref, b_ref, o_ref):
    gd = gd_ref[...]
    x0 = jnp.dot(gd, w0_ref[...], precision=HIGHEST, preferred_element_type=F32) + b_ref[0:1, :]
    x1 = jnp.dot(gd, w1_ref[...], precision=HIGHEST, preferred_element_type=F32) + b_ref[1:2, :]
    o_ref[:, :GLA_DK_TOT] = _log_sigmoid(x0) * (1.0 / GATE_TEMP)
    o_ref[:, GLA_DK_TOT:] = _log_sigmoid(x1) * (1.0 / GATE_TEMP)


def _gla_gates(proj, w0, w1, b_gate, tm=1024):
    gd_col = (GLA_PROJ_W - LANES) // LANES
    return pl.pallas_call(
        _gla_gate_kernel,
        grid=(N_TOK // tm,),
        in_specs=[pl.BlockSpec((tm, LANES), lambda i: (i, gd_col)),
                  pl.BlockSpec((LANES, GLA_DK_TOT), lambda i: (0, 0)),
                  pl.BlockSpec((LANES, GLA_DK_TOT), lambda i: (0, 0)),
                  pl.BlockSpec((2, GLA_DK_TOT), lambda i: (0, 0))],
        out_specs=pl.BlockSpec((tm, 2 * GLA_DK_TOT), lambda i: (i, 0)),
        out_shape=jax.ShapeDtypeStruct((N_TOK, 2 * GLA_DK_TOT), F32),
        compiler_params=_params("arbitrary"),
        name="gla_gates",
    )(proj, w0, w1, b_gate)


def _gla_direction(q_ref, k_ref, v_ref, la_ref, o_ref, s_ref, d, causal):
    c = GLA_CHUNK
    row = lax.broadcasted_iota(jnp.int32, (c, c), 0)
    col = lax.broadcasted_iota(jnp.int32, (c, c), 1)
    keep = (col <= row) if causal else (col >= row)
    b = jnp.dot(keep.astype(F32), la_ref[...], precision=HIGHEST, preferred_element_type=F32)
    end = c - 1 if causal else 0
    b_end = b[end:end + 1, :]
    q_dec = q_ref[...] * (GLA_DK ** -0.5) * jnp.exp(b)
    k_inv = k_ref[...] * jnp.exp(-b)
    k_end = k_ref[...] * jnp.exp(b_end - b)
    e_end = jnp.exp(b_end)
    eye = (lax.broadcasted_iota(jnp.int32, (GLA_DK, GLA_DK), 0)
           == lax.broadcasted_iota(jnp.int32, (GLA_DK, GLA_DK), 1))
    for h in range(GLA_HEADS):
        ks = slice(h * GLA_DK, (h + 1) * GLA_DK)
        vs = slice(h * GLA_DV, (h + 1) * GLA_DV)
        state = s_ref[d, h]
        qh = q_dec[:, ks].astype(BF16)
        vh = v_ref[:, vs].astype(BF16)
        scores = lax.dot_general(qh, k_inv[:, ks].astype(BF16), NT_DIMS, preferred_element_type=F32)
        scores = jnp.where(keep, scores, 0.0)
        o_ref[:, vs] = (jnp.dot(scores.astype(BF16), vh, preferred_element_type=F32)
                        + jnp.dot(qh, state.astype(BF16), preferred_element_type=F32))
        e_col = jnp.sum(jnp.where(eye, jnp.broadcast_to(e_end[:, ks], (GLA_DK, GLA_DK)), 0.0),
                        axis=1, keepdims=True)
        s_ref[d, h] = state * e_col + lax.dot_general(
            k_end[:, ks].astype(BF16), vh, TN_DIMS, preferred_element_type=F32)


def _gla_scan_kernel(*refs, n_chunks, has_s0, write_state):
    qf, kf, vf, laf, qb, kb, vb, lab = refs[:8]
    rest = list(refs[8:])
    s0_ref = rest.pop(0) if has_s0 else None
    of_ref, ob_ref = rest.pop(0), rest.pop(0)
    sout_ref = rest.pop(0) if write_state else None
    s_ref = rest.pop(0)
    i = pl.program_id(1)

    @pl.when(i == 0)
    def _():
        s_ref[...] = s0_ref[...] if has_s0 else jnp.zeros(s_ref.shape, F32)

    _gla_direction(qf, kf, vf, laf, of_ref, s_ref, 0, True)
    _gla_direction(qb, kb, vb, lab, ob_ref, s_ref, 1, False)

    if write_state:
        @pl.when(i == n_chunks - 1)
        def _():
            sout_ref[...] = s_ref[...]


def _gla_scan(proj, la, row0, n_batch, seq, s0):
    c = GLA_CHUNK
    n = seq // c
    base = row0 // c
    has_s0 = s0 is not None
    write_state = not has_s0
    fwd = lambda b, i: base + b * n + i
    bwd = lambda b, i: base + b * n + (n - 1 - i)

    def specs(blk):
        return [pl.BlockSpec((c, GLA_DK_TOT), lambda b, i: (blk(b, i), 0)),
                pl.BlockSpec((c, GLA_DK_TOT), lambda b, i: (blk(b, i), 1)),
                pl.BlockSpec((c, GLA_DV_TOT), lambda b, i: (blk(b, i), 1))]
    state_block = (None, 2, GLA_HEADS, GLA_DK, GLA_DV)
    state_spec = pl.BlockSpec(state_block, lambda b, i: (b, 0, 0, 0, 0))
    in_specs = (specs(fwd) + [pl.BlockSpec((c, GLA_DK_TOT), lambda b, i: (fwd(b, i), 0))]
                + specs(bwd) + [pl.BlockSpec((c, GLA_DK_TOT), lambda b, i: (bwd(b, i), 1))])
    args = [proj, proj, proj, la, proj, proj, proj, la]
    if has_s0:
        in_specs.append(state_spec)
        args.append(s0)
    out_rows = n_batch * seq
    out_specs = [pl.BlockSpec((c, GLA_DV_TOT), lambda b, i: (b * n + i, 0)),
                 pl.BlockSpec((c, GLA_DV_TOT), lambda b, i: (b * n + (n - 1 - i), 0))]
    out_shape = [jax.ShapeDtypeStruct((out_rows, GLA_DV_TOT), F32)] * 2
    if write_state:
        out_specs.append(state_spec)
        out_shape.append(jax.ShapeDtypeStruct((n_batch, 2, GLA_HEADS, GLA_DK, GLA_DV), F32))
    return pl.pallas_call(
        functools.partial(_gla_scan_kernel, n_chunks=n, has_s0=has_s0, write_state=write_state),
        grid=(n_batch, n),
        in_specs=in_specs,
        out_specs=out_specs,
        out_shape=out_shape,
        scratch_shapes=[pltpu.VMEM((2, GLA_HEADS, GLA_DK, GLA_DV), F32)],
        compiler_params=_params("arbitrary", "arbitrary"),
        name="gla_scan",
    )(*args)


def _gla_post_kernel(of_ref, ob_ref, r_ref, g_ref, o_ref):
    o = of_ref[...] + ob_ref[...]
    normed = jnp.concatenate(
        [_rms(o[:, h * GLA_DV:(h + 1) * GLA_DV]) for h in range(GLA_HEADS)], axis=1)
    o_ref[...] = (normed * g_ref[...] * _silu(r_ref[...])).astype(o_ref.dtype)


def _gla_post(o_f, o_b, proj, g_tiled, tm=1024):
    return pl.pallas_call(
        _gla_post_kernel,
        grid=(N_TOK // tm,),
        in_specs=[pl.BlockSpec((tm, GLA_DV_TOT), lambda i: (i, 0)),
                  pl.BlockSpec((tm, GLA_DV_TOT), lambda i: (i, 0)),
                  pl.BlockSpec((tm, GLA_DV_TOT), lambda i: (i, 2)),
                  pl.BlockSpec((1, GLA_DV_TOT), lambda i: (0, 0))],
        out_specs=pl.BlockSpec((tm, GLA_DV_TOT), lambda i: (i, 0)),
        out_shape=jax.ShapeDtypeStruct((N_TOK, GLA_DV_TOT), BF16),
        compiler_params=_params("arbitrary"),
        name="gla_post",
    )(o_f, o_b, proj, g_tiled)


def _rope_pair(u, cs):
    t = u * cs
    return t + pltpu.roll(t, QK_ROPE, axis=1)


def _mla_down_kernel(a_ref, w_ref, qn_ref, kvn_ref, cs_ref, cq_ref, ckv_ref, kr_ref, krp_ref):
    acc = jnp.dot(a_ref[...], w_ref[...], preferred_element_type=F32)
    cq_ref[...] = (_rms(acc[:, :Q_LORA]) * qn_ref[...]).astype(cq_ref.dtype)
    ckv_ref[...] = _rms(acc[:, Q_LORA:Q_LORA + KV_LORA]) * kvn_ref[...]
    u = acc[:, Q_LORA + KV_LORA:]
    kr_ref[...] = u
    lane = lax.broadcasted_iota(jnp.int32, u.shape, 1)
    krp_ref[...] = jnp.where(lane < QK_ROPE, _rope_pair(u, cs_ref[...]), 0.0).astype(krp_ref.dtype)


def _mla_down(h, w_ext, q_norm, kv_norm, cs, tm=512):
    wn = w_ext.shape[1]
    row = lambda i: (i, 0)
    fixed = lambda i: (0, 0)
    return pl.pallas_call(
        _mla_down_kernel,
        grid=(N_TOK // tm,),
        in_specs=[pl.BlockSpec((tm, D_MODEL), row), pl.BlockSpec((D_MODEL, wn), fixed),
                  pl.BlockSpec((1, Q_LORA), fixed), pl.BlockSpec((1, KV_LORA), fixed),
                  pl.BlockSpec((tm, LANES), row)],
        out_specs=[pl.BlockSpec((tm, Q_LORA), row), pl.BlockSpec((tm, KV_LORA), row),
                   pl.BlockSpec((tm, LANES), row), pl.BlockSpec((tm, LANES), row)],
        out_shape=[jax.ShapeDtypeStruct((N_TOK, Q_LORA), BF16),
                   jax.ShapeDtypeStruct((N_TOK, KV_LORA), F32),
                   jax.ShapeDtypeStruct((N_TOK, LANES), F32),
                   jax.ShapeDtypeStruct((N_TOK, LANES), BF16)],
        compiler_params=_params("arbitrary"),
        name="mla_down",
    )(h, w_ext, q_norm.reshape(1, Q_LORA), kv_norm.reshape(1, KV_LORA), cs)


def _mla_q_kernel(a_ref, w_ref, cs_ref, o_ref):
    acc = jnp.dot(a_ref[...], w_ref[...], preferred_element_type=F32)
    cs = cs_ref[...]
    for h in range(MLA_HEADS):
        lo = h * 2 * LANES
        o_ref[:, lo:lo + LANES] = acc[:, lo:lo + LANES].astype(o_ref.dtype)
        o_ref[:, lo + LANES:lo + 2 * LANES] = _rope_pair(
            acc[:, lo + LANES:lo + 2 * LANES], cs).astype(o_ref.dtype)


def _mla_q(cq, w_ext, cs, tm=512):
    wn = w_ext.shape[1]
    return pl.pallas_call(
        _mla_q_kernel,
        grid=(N_TOK // tm,),
        in_specs=[pl.BlockSpec((tm, Q_LORA), lambda i: (i, 0)),
                  pl.BlockSpec((Q_LORA, wn), lambda i: (0, 0)),
                  pl.BlockSpec((tm, LANES), lambda i: (i, 0))],
        out_specs=pl.BlockSpec((tm, wn), lambda i: (i, 0)),
        out_shape=jax.ShapeDtypeStruct((N_TOK, wn), BF16),
        compiler_params=_params("arbitrary"),
        name="mla_q",
    )(cq, w_ext, cs)


def _attn_kernel(q_ref, kv_ref, kr_ref, o_ref):
    kv = kv_ref[...]
    k = jnp.concatenate([kv[:, :QK_NOPE], kr_ref[...]], axis=1)
    s = lax.dot_general(q_ref[...], k, NT_DIMS, preferred_element_type=F32)
    p = jnp.exp2((s - jnp.max(s, axis=1, keepdims=True)) * (MLA_SCALE * LOG2E))
    l = jnp.sum(p, axis=1, keepdims=True)
    o = jnp.dot(p.astype(BF16), kv[:, QK_NOPE:], preferred_element_type=F32)
    o_ref[...] = (o / l).astype(o_ref.dtype)


def _attention(q, kv, krp, row0, n_batch, seq, n_keys, tq):
    nq = seq // tq
    base = row0 // tq
    hw = QK_NOPE + V_HEAD
    return pl.pallas_call(
        _attn_kernel,
        grid=(n_batch, MLA_HEADS, nq),
        in_specs=[pl.BlockSpec((tq, hw), lambda b, h, i: (base + b * nq + i, h)),
                  pl.BlockSpec((n_keys, hw), lambda b, h, i: (b, h)),
                  pl.BlockSpec((n_keys, LANES), lambda b, h, i: (b, 0))],
        out_specs=pl.BlockSpec((tq, V_HEAD), lambda b, h, i: (b * nq + i, h)),
        out_shape=jax.ShapeDtypeStruct((n_batch * seq, MLA_HEADS * V_HEAD), BF16),
        compiler_params=_params("arbitrary", "arbitrary", "arbitrary"),
        name="mla_attention",
    )(q, kv, krp)


def _rope_tables():
    half = QK_ROPE // 2
    pos = np.arange(DEC_SEQ)
    inv = ROPE_BASE ** (-np.arange(0, half, 2, dtype=np.float64) / half)
    ang = np.concatenate([(pos // GRID_W)[:, None] * inv, (pos % GRID_W)[:, None] * inv], axis=-1)
    ang = ang.astype(np.float32)
    cos = np.repeat(np.cos(ang), 2, axis=1)
    sin = np.repeat(np.sin(ang), 2, axis=1) * np.tile(np.array([-1.0, 1.0], np.float32), half)
    cs = np.concatenate([cos, sin], axis=1).astype(np.float32)
    ident = np.concatenate([np.ones((N_PROMPT, QK_ROPE), np.float32),
                            np.zeros((N_PROMPT, QK_ROPE), np.float32)], axis=1)
    return jnp.asarray(np.concatenate([ident, np.tile(cs, (DEC_BATCH, 1))], axis=0))


def kernel(x_prompt, x_sample, c, state_gla, cache_ckv, cache_krope, c_ctx, ada_w, ada_b, norm_mix, norm_ffn, norm_final, gla_w_in, gla_w_gate_up, gla_b_gate, gla_norm, gla_w_out, mla_w_down, mla_q_norm, mla_kv_norm, mla_w_uq, mla_w_ukv, mla_w_out, ffn_w_gate_up, ffn_w_down, moe_router, moe_w_gate_up, moe_w_down):
    x = jnp.concatenate([x_prompt.reshape(N_PROMPT, D_MODEL), x_sample.reshape(N_SAMPLE, D_MODEL)])
    cond = jnp.concatenate([c_ctx[None], c, jnp.zeros((N_COND - 1 - DEC_BATCH, D_MODEL), F32)])
    mods_all = _ada_mods(cond, ada_w, ada_b).reshape(DEPTH, N_COND, 6, 1, D_MODEL)
    cs = _rope_tables()
    swap = np.arange(QK_ROPE) ^ 1

    new_gla, new_ckv, new_kr = [], [], []
    for layer in range(DEPTH):
        j = layer // 2
        mods = mods_all[layer]
        h = _modulate(x, norm_mix[layer], mods, 0, 1)
        if layer % 2 == 0:
            w_in = jnp.pad(gla_w_in[j], ((0, 0), (0, GLA_PROJ_W - gla_w_in.shape[2]))).astype(BF16)
            proj = _matmul(h, w_in, F32, tm=1024, tn=640)
            pad_lo = jnp.zeros((GATE_RANK, GLA_DK_TOT), F32)
            pad_hi = jnp.zeros((LANES - 2 * GATE_RANK, GLA_DK_TOT), F32)
            w0 = jnp.concatenate([gla_w_gate_up[j, 0], pad_lo, pad_hi])
            w1 = jnp.concatenate([pad_lo, gla_w_gate_up[j, 1], pad_hi])
            la = _gla_gates(proj, w0, w1, gla_b_gate[j])
            of_p, ob_p, st = _gla_scan(proj, la, 0, BATCH, SEQ, None)
            of_s, ob_s = _gla_scan(proj, la, N_PROMPT, DEC_BATCH, DEC_SEQ, state_gla[:, j])
            new_gla.append(st)
            mixed = _gla_post(jnp.concatenate([of_p, of_s]), jnp.concatenate([ob_p, ob_s]), proj,
                              jnp.tile(gla_norm[j], GLA_HEADS).reshape(1, GLA_DV_TOT))
            x = _matmul_residual(mixed, gla_w_out[j].astype(BF16), x, mods, 2)
        else:
            wd = mla_w_down[j]
            w_down = jnp.concatenate([wd, wd[:, Q_LORA + KV_LORA:][:, swap]], axis=1).astype(BF16)
            cq, ckv, kr, krp = _mla_down(h, w_down, mla_q_norm[j], mla_kv_norm[j], cs)
            wq = mla_w_uq[j].reshape(Q_LORA, MLA_HEADS, QK_NOPE + QK_ROPE)
            w_uq = jnp.concatenate([wq, wq[:, :, QK_NOPE:][:, :, swap]], axis=2)
            q = _mla_q(cq, w_uq.reshape(Q_LORA, -1).astype(BF16), cs)
            w_ukv = mla_w_ukv[j].astype(BF16)
            ckv_b = ckv.astype(BF16)
            kv_p = _matmul(ckv_b[:N_PROMPT], w_ukv, BF16, tm=1024, tn=1024)
            o_p = _attention(q, kv_p, krp[:N_PROMPT], 0, BATCH, SEQ, SEQ, tq=SEQ)
            n_keys = PAST_LEN + DEC_SEQ
            ckv_keys = jnp.concatenate(
                [cache_ckv[:, j].astype(BF16), ckv_b[N_PROMPT:].reshape(DEC_BATCH, DEC_SEQ, KV_LORA)],
                axis=1).reshape(DEC_BATCH * n_keys, KV_LORA)
            kr_ctx = jnp.pad(cache_krope[:, j], ((0, 0), (0, 0), (0, LANES - QK_ROPE))).astype(BF16)
            krp_keys = jnp.concatenate(
                [kr_ctx, krp[N_PROMPT:].reshape(DEC_BATCH, DEC_SEQ, LANES)],
                axis=1).reshape(DEC_BATCH * n_keys, LANES)
            kv_s = _matmul(ckv_keys, w_ukv, BF16, tm=1024, tn=1024)
            o_s = _attention(q, kv_s, krp_keys, N_PROMPT, DEC_BATCH, DEC_SEQ, n_keys, tq=512)
            new_ckv.append(ckv[:N_PROMPT].reshape(BATCH, SEQ, KV_LORA))
            new_kr.append(kr[:N_PROMPT, :QK_ROPE].reshape(BATCH, SEQ, QK_ROPE))
            x = _matmul_residual(jnp.concatenate([o_p, o_s]), mla_w_out[j].astype(BF16), x, mods, 2)

        if layer % 2 == 0:
            h = _modulate(x, norm_ffn[layer], mods, 3, 4)
            act = _swiglu_up(h, ffn_w_gate_up[j].astype(BF16), tm=1024, tn=1408)
            x = _matmul_residual(act, ffn_w_down[j].astype(BF16), x, mods, 5)
        else:
            router_pad = jnp.pad(moe_router[j], ((0, 0), (0, LANES - N_EXPERTS)))
            h, gates, slots, cnt = _modulate_router(x, norm_ffn[layer], mods, 3, 4, router_pad)
            counts = cnt[:, 0, :N_EXPERTS].astype(jnp.int32).reshape(-1)
            y = None
            for e in range(N_EXPERTS):
                y = _moe_expert(h, gates, slots, counts, moe_w_gate_up[j, e].astype(BF16),
                                moe_w_down[j, e].astype(BF16), y, e)
            x = _gated_add(x, y, mods, 5)

    y = _final_norm(x, norm_final)
    return (y[:N_PROMPT].reshape(BATCH, SEQ, D_MODEL),
            y[N_PROMPT:].reshape(DEC_BATCH, DEC_SEQ, D_MODEL),
            jnp.stack(new_gla, axis=1),
            jnp.stack(new_ckv, axis=1),
            jnp.stack(new_kr, axis=1))
```

```python
import functools

import jax
import jax.numpy as jnp
import numpy as np
from jax import lax
from jax.experimental import pallas as pl
from jax.experimental.pallas import tpu as pltpu

D_MODEL = 1024
BATCH = 16
SEQ = 256
DEPTH = 4
DEC_BATCH = 4
DEC_SEQ = 2048
PAST_LEN = 512
GRID_W = 64
GLA_HEADS = 4
GLA_DK = 128
GLA_DV = 256
GLA_DK_TOT = GLA_HEADS * GLA_DK
GLA_DV_TOT = GLA_HEADS * GLA_DV
GATE_RANK = 16
GATE_TEMP = 16.0
MLA_HEADS = 8
Q_LORA = 384
KV_LORA = 256
QK_NOPE = 128
QK_ROPE = 64
V_HEAD = 128
MLA_SCALE = (QK_NOPE + QK_ROPE) ** -0.5
ROPE_BASE = 10000.0
D_FF = 2816
N_EXPERTS = 8
D_EXPERT = 3584
EPS = 1e-6

N_PROMPT = BATCH * SEQ
N_SAMPLE = DEC_BATCH * DEC_SEQ
N_TOK = N_PROMPT + N_SAMPLE
N_COND = 8
LANES = 128
GLA_PROJ_W = 3200
GLA_CHUNK = 256
GLA_MAX_FACTORISED_DECAY = 80.0
ATTN_KEY_CHUNK = 512
MOE_BLOCK = 1024
MOE_TILE = 128
LOG2E = 1.4426950408889634
VMEM_LIMIT = 56 * 1024 * 1024

F32 = jnp.float32
BF16 = jnp.bfloat16
HIGHEST = lax.Precision.HIGHEST
NT_DIMS = (((1,), (1,)), ((), ()))
TN_DIMS = (((0,), (0,)), ((), ()))


def _params(*sem):
    return pltpu.CompilerParams(dimension_semantics=sem, vmem_limit_bytes=VMEM_LIMIT)


def _cond_row(i, tm):
    return jnp.where(i * tm < N_PROMPT, 0, 1 + jnp.maximum(i * tm - N_PROMPT, 0) // DEC_SEQ)


def _mod_spec(which, tm, tn=D_MODEL, col=None):
    if col is None:
        return pl.BlockSpec((None, None, 1, tn), lambda i, *_: (_cond_row(i, tm), which, 0, 0))
    return pl.BlockSpec((None, None, 1, tn), lambda i, j: (_cond_row(i, tm), which, 0, j))


def _silu(x):
    return x / (1.0 + jnp.exp(-x))


def _rms(x):
    return x * lax.rsqrt(jnp.mean(x * x, axis=-1, keepdims=True) + EPS)


def _ada_kernel(c_ref, w_ref, b_ref, o_ref):
    o_ref[...] = jnp.dot(_silu(c_ref[...]), w_ref[...], precision=HIGHEST,
                         preferred_element_type=F32) + b_ref[...]


def _ada_mods(cond, ada_w, ada_b):
    tn = 1536
    return pl.pallas_call(
        _ada_kernel,
        grid=(DEPTH, 6 * D_MODEL // tn),
        in_specs=[pl.BlockSpec((N_COND, D_MODEL), lambda l, j: (0, 0)),
                  pl.BlockSpec((None, D_MODEL, tn), lambda l, j: (l, 0, j)),
                  pl.BlockSpec((None, 1, tn), lambda l, j: (l, 0, j))],
        out_specs=pl.BlockSpec((None, N_COND, tn), lambda l, j: (l, 0, j)),
        out_shape=jax.ShapeDtypeStruct((DEPTH, N_COND, 6 * D_MODEL), F32),
        compiler_params=_params("arbitrary", "arbitrary"),
        name="ada_mods",
    )(cond, ada_w, ada_b.reshape(DEPTH, 1, 6 * D_MODEL))


def _modulate_kernel(x_ref, g_ref, sh_ref, sc_ref, o_ref):
    y = _rms(x_ref[...]) * g_ref[...]
    o_ref[...] = (y * (1.0 + sc_ref[...]) + sh_ref[...]).astype(o_ref.dtype)


def _modulate(x, g, mods, shift_idx, scale_idx, tm=1024):
    return pl.pallas_call(
        _modulate_kernel,
        grid=(N_TOK // tm,),
        in_specs=[pl.BlockSpec((tm, D_MODEL), lambda i: (i, 0)),
                  pl.BlockSpec((1, D_MODEL), lambda i: (0, 0)),
                  _mod_spec(shift_idx, tm), _mod_spec(scale_idx, tm)],
        out_specs=pl.BlockSpec((tm, D_MODEL), lambda i: (i, 0)),
        out_shape=jax.ShapeDtypeStruct((N_TOK, D_MODEL), BF16),
        compiler_params=_params("arbitrary"),
        name="modulate",
    )(x, g.reshape(1, D_MODEL), mods, mods)


def _modulate_router_kernel(x_ref, g_ref, sh_ref, sc_ref, r_ref, o_ref, gates_ref, slot_ref, cnt_ref):
    y = _rms(x_ref[...]) * g_ref[...]
    h = y * (1.0 + sc_ref[...]) + sh_ref[...]
    o_ref[...] = h.astype(o_ref.dtype)
    logits = jnp.dot(h, r_ref[...], precision=HIGHEST, preferred_element_type=F32)
    lane = lax.broadcasted_iota(jnp.int32, logits.shape, 1)
    l1 = jnp.where(lane < N_EXPERTS, logits, -jnp.inf)
    m1 = jnp.max(l1, axis=1, keepdims=True)
    i1 = jnp.min(jnp.where(l1 == m1, lane, LANES), axis=1, keepdims=True)
    l2 = jnp.where(lane == i1, -jnp.inf, l1)
    m2 = jnp.max(l2, axis=1, keepdims=True)
    i2 = jnp.min(jnp.where(l2 == m2, lane, LANES), axis=1, keepdims=True)
    e = jnp.exp(m2 - m1)
    w1 = 1.0 / (1.0 + e)
    w2 = e / (1.0 + e)
    gates_ref[...] = jnp.where(lane == i1, w1, 0.0) + jnp.where(lane == i2, w2, 0.0)
    sel = (lane == i1) | (lane == i2)
    tb = logits.shape[0]
    earlier = (lax.broadcasted_iota(jnp.int32, (tb, tb), 1)
               < lax.broadcasted_iota(jnp.int32, (tb, tb), 0)).astype(BF16)
    before = jnp.dot(earlier, sel.astype(BF16), preferred_element_type=F32)
    slot_ref[...] = jnp.where(sel, before + 1.0, 0.0)
    cnt_ref[...] = jnp.broadcast_to(jnp.sum(sel.astype(F32), axis=0, keepdims=True), cnt_ref.shape)


def _modulate_router(x, g, mods, shift_idx, scale_idx, router_pad):
    tm = MOE_BLOCK
    nb = N_TOK // tm
    return pl.pallas_call(
        _modulate_router_kernel,
        grid=(nb,),
        in_specs=[pl.BlockSpec((tm, D_MODEL), lambda i: (i, 0)),
                  pl.BlockSpec((1, D_MODEL), lambda i: (0, 0)),
                  _mod_spec(shift_idx, tm), _mod_spec(scale_idx, tm),
                  pl.BlockSpec((D_MODEL, LANES), lambda i: (0, 0))],
        out_specs=[pl.BlockSpec((tm, D_MODEL), lambda i: (i, 0)),
                   pl.BlockSpec((tm, LANES), lambda i: (i, 0)),
                   pl.BlockSpec((tm, LANES), lambda i: (i, 0)),
                   pl.BlockSpec((None, 8, LANES), lambda i: (i, 0, 0))],
        out_shape=[jax.ShapeDtypeStruct((N_TOK, D_MODEL), BF16),
                   jax.ShapeDtypeStruct((N_TOK, LANES), F32),
                   jax.ShapeDtypeStruct((N_TOK, LANES), F32),
                   jax.ShapeDtypeStruct((nb, 8, LANES), F32)],
        compiler_params=_params("arbitrary"),
        name="modulate_router",
    )(x, g.reshape(1, D_MODEL), mods, mods, router_pad)


def _moe_expert_kernel(cnt_ref, h_ref, gates_ref, slot_ref, wgu_ref, wd_ref, *rest, expert, first):
    y_ref = rest[-1]
    if first:
        y_ref[...] = jnp.zeros(y_ref.shape, F32)
    else:
        y_ref[...] = rest[0][...]
    tb = h_ref.shape[0]
    gate_col = gates_ref[:, expert:expert + 1]
    slot_col = slot_ref[:, expert:expert + 1]
    lane = lax.broadcasted_iota(jnp.int32, (tb, MOE_TILE), 1).astype(F32)
    n_tiles = (cnt_ref[pl.program_id(0) * N_EXPERTS + expert] + (MOE_TILE - 1)) // MOE_TILE
    cw = 256

    def tile(k, carry):
        onehot = (slot_col == lane + (k * MOE_TILE + 1).astype(F32)).astype(BF16)
        hg = lax.dot_general(onehot, h_ref[...], TN_DIMS, preferred_element_type=F32).astype(BF16)
        gu = jnp.dot(hg, wgu_ref[...], preferred_element_type=F32)
        act = (_silu(gu[:, :D_EXPERT]) * gu[:, D_EXPERT:]).astype(BF16)
        out = jnp.dot(act, wd_ref[...], preferred_element_type=F32).astype(BF16)
        for c0 in range(0, D_MODEL, cw):
            y_ref[:, c0:c0 + cw] += gate_col * jnp.dot(onehot, out[:, c0:c0 + cw],
                                                       preferred_element_type=F32)
        return carry

    lax.fori_loop(0, n_tiles, tile, 0)


def _moe_expert(h, gates, slots, counts, w_gu, w_down, y, expert):
    tb = MOE_BLOCK
    first = y is None
    blk = lambda b, cnt: (b, 0)
    fixed = lambda b, cnt: (0, 0)
    in_specs = [pl.BlockSpec((tb, D_MODEL), blk), pl.BlockSpec((tb, LANES), blk),
                pl.BlockSpec((tb, LANES), blk),
                pl.BlockSpec((D_MODEL, 2 * D_EXPERT), fixed, pipeline_mode=pl.Buffered(1)),
                pl.BlockSpec((D_EXPERT, D_MODEL), fixed, pipeline_mode=pl.Buffered(1))]
    args = [h, gates, slots, w_gu, w_down]
    if not first:
        in_specs.append(pl.BlockSpec((tb, D_MODEL), blk))
        args.append(y)
    return pl.pallas_call(
        functools.partial(_moe_expert_kernel, expert=expert, first=first),
        grid_spec=pltpu.PrefetchScalarGridSpec(
            num_scalar_prefetch=1, grid=(N_TOK // tb,), in_specs=in_specs,
            out_specs=pl.BlockSpec((tb, D_MODEL), blk)),
        out_shape=jax.ShapeDtypeStruct((N_TOK, D_MODEL), F32),
        compiler_params=_params("arbitrary"),
        name="moe_expert",
    )(counts, *args)


def _mm_kernel(a_ref, w_ref, o_ref):
    o_ref[...] = jnp.dot(a_ref[...], w_ref[...], preferred_element_type=F32).astype(o_ref.dtype)


def _matmul(a, w, out_dtype, tm, tn):
    m, k = a.shape
    n = w.shape[1]
    return pl.pallas_call(
        _mm_kernel,
        grid=(m // tm, n // tn),
        in_specs=[pl.BlockSpec((tm, k), lambda i, j: (i, 0)),
                  pl.BlockSpec((k, tn), lambda i, j: (0, j))],
        out_specs=pl.BlockSpec((tm, tn), lambda i, j: (i, j)),
        out_shape=jax.ShapeDtypeStruct((m, n), out_dtype),
        compiler_params=_params("arbitrary", "arbitrary"),
        name="matmul",
    )(a, w)


def _swiglu_kernel(a_ref, wg_ref, wu_ref, o_ref):
    a = a_ref[...]
    g = jnp.dot(a, wg_ref[...], preferred_element_type=F32)
    u = jnp.dot(a, wu_ref[...], preferred_element_type=F32)
    o_ref[...] = (_silu(g) * u).astype(o_ref.dtype)


def _swiglu_up(a, w_gu, tm, tn):
    m, k = a.shape
    f = w_gu.shape[1] // 2
    nj = f // tn
    return pl.pallas_call(
        _swiglu_kernel,
        grid=(m // tm, nj),
        in_specs=[pl.BlockSpec((tm, k), lambda i, j: (i, 0)),
                  pl.BlockSpec((k, tn), lambda i, j: (0, j)),
                  pl.BlockSpec((k, tn), lambda i, j: (0, j + nj))],
        out_specs=pl.BlockSpec((tm, tn), lambda i, j: (i, j)),
        out_shape=jax.ShapeDtypeStruct((m, f), BF16),
        compiler_params=_params("arbitrary", "arbitrary"),
        name="swiglu_up",
    )(a, w_gu, w_gu)


def _mm_residual_kernel(a_ref, w_ref, x_ref, g_ref, o_ref):
    y = jnp.dot(a_ref[...], w_ref[...], preferred_element_type=F32)
    o_ref[...] = x_ref[...] + g_ref[...] * y


def _matmul_residual(a, w, x, mods, gate_idx, tm=1024, tn=512):
    k = a.shape[1]
    return pl.pallas_call(
        _mm_residual_kernel,
        grid=(N_TOK // tm, D_MODEL // tn),
        in_specs=[pl.BlockSpec((tm, k), lambda i, j: (i, 0)),
                  pl.BlockSpec((k, tn), lambda i, j: (0, j)),
                  pl.BlockSpec((tm, tn), lambda i, j: (i, j)),
                  _mod_spec(gate_idx, tm, tn, col=True)],
        out_specs=pl.BlockSpec((tm, tn), lambda i, j: (i, j)),
        out_shape=jax.ShapeDtypeStruct((N_TOK, D_MODEL), F32),
        compiler_params=_params("arbitrary", "arbitrary"),
        name="matmul_residual",
    )(a, w, x, mods)


def _gated_add_kernel(x_ref, y_ref, g_ref, o_ref):
    o_ref[...] = x_ref[...] + g_ref[...] * y_ref[...]


def _gated_add(x, y, mods, gate_idx, tm=1024):
    return pl.pallas_call(
        _gated_add_kernel,
        grid=(N_TOK // tm,),
        in_specs=[pl.BlockSpec((tm, D_MODEL), lambda i: (i, 0)),
                  pl.BlockSpec((tm, D_MODEL), lambda i: (i, 0)),
                  _mod_spec(gate_idx, tm)],
        out_specs=pl.BlockSpec((tm, D_MODEL), lambda i: (i, 0)),
        out_shape=jax.ShapeDtypeStruct((N_TOK, D_MODEL), F32),
        compiler_params=_params("arbitrary"),
        name="gated_add",
    )(x, y, mods)


def _final_norm_kernel(x_ref, g_ref, o_ref):
    o_ref[...] = _rms(x_ref[...]) * g_ref[...]


def _final_norm(x, g, tm=1024):
    return pl.pallas_call(
        _final_norm_kernel,
        grid=(N_TOK // tm,),
        in_specs=[pl.BlockSpec((tm, D_MODEL), lambda i: (i, 0)),
                  pl.BlockSpec((1, D_MODEL), lambda i: (0, 0))],
        out_specs=pl.BlockSpec((tm, D_MODEL), lambda i: (i, 0)),
        out_shape=jax.ShapeDtypeStruct((N_TOK, D_MODEL), F32),
        compiler_params=_params("arbitrary"),
        name="final_norm",
    )(x, g.reshape(1, D_MODEL))


def _log_sigmoid(x):
    return jnp.minimum(x, 0.0) - jnp.log(1.0 + jnp.exp(-jnp.abs(x)))


def _gla_gate_kernel(gd_ref, w0_ref, w1_ref, b_ref, o_ref):
    gd = gd_ref[...]
    x0 = jnp.dot(gd, w0_ref[...], precision=HIGHEST, preferred_element_type=F32) + b_ref[0:1, :]
    x1 = jnp.dot(gd, w1_ref[...], precision=HIGHEST, preferred_element_type=F32) + b_ref[1:2, :]
    o_ref[:, :GLA_DK_TOT] = _log_sigmoid(x0) * (1.0 / GATE_TEMP)
    o_ref[:, GLA_DK_TOT:] = _log_sigmoid(x1) * (1.0 / GATE_TEMP)


def _gla_gates(proj, w0, w1, b_gate, tm=1024):
    gd_col = (GLA_PROJ_W - LANES) // LANES
    return pl.pallas_call(
        _gla_gate_kernel,
        grid=(N_TOK // tm,),
        in_specs=[pl.BlockSpec((tm, LANES), lambda i: (i, gd_col)),
                  pl.BlockSpec((LANES, GLA_DK_TOT), lambda i: (0, 0)),
                  pl.BlockSpec((LANES, GLA_DK_TOT), lambda i: (0, 0)),
                  pl.BlockSpec((2, GLA_DK_TOT), lambda i: (0, 0))],
        out_specs=pl.BlockSpec((tm, 2 * GLA_DK_TOT), lambda i: (i, 0)),
        out_shape=jax.ShapeDtypeStruct((N_TOK, 2 * GLA_DK_TOT), F32),
        compiler_params=_params("arbitrary"),
        name="gla_gates",
    )(proj, w0, w1, b_gate)


def _gla_keep_mask(causal):
    c = GLA_CHUNK
    row = lax.broadcasted_iota(jnp.int32, (c, c), 0)
    col = lax.broadcasted_iota(jnp.int32, (c, c), 1)
    return (col <= row) if causal else (col >= row)


def _gla_cum_decay(la_ref, causal):
    mask = _gla_keep_mask(causal).astype(BF16)
    la = la_ref[...]
    hi = la.astype(BF16)
    lo = (la - hi.astype(F32)).astype(BF16)
    return (jnp.dot(mask, hi, preferred_element_type=F32)
            + jnp.dot(mask, lo, preferred_element_type=F32))


def _gla_direction(q_ref, k_ref, v_ref, b, o_ref, s_ref, b_ref, d, causal, factorised):
    c = GLA_CHUNK
    keep = _gla_keep_mask(causal)
    end = c - 1 if causal else 0
    b_end = b[end:end + 1, :]
    q_dec = q_ref[...] * (GLA_DK ** -0.5) * jnp.exp(b)
    k_end = k_ref[...] * jnp.exp(b_end - b)
    e_end = jnp.exp(b_end)
    eye = (lax.broadcasted_iota(jnp.int32, (GLA_DK, GLA_DK), 0)
           == lax.broadcasted_iota(jnp.int32, (GLA_DK, GLA_DK), 1))
    if factorised:
        k_inv = k_ref[...] * jnp.exp(-b)
    else:
        b_ref[...] = b
        col = lax.broadcasted_iota(jnp.int32, (c, c), 1)
    for h in range(GLA_HEADS):
        ks = slice(h * GLA_DK, (h + 1) * GLA_DK)
        vs = slice(h * GLA_DV, (h + 1) * GLA_DV)
        state = s_ref[d, h]
        qh = q_dec[:, ks].astype(BF16)
        vh = v_ref[:, vs].astype(BF16)
        if factorised:
            scores = lax.dot_general(qh, k_inv[:, ks].astype(BF16), NT_DIMS,
                                     preferred_element_type=F32)
        else:
            def key_column(s, sc, ks=ks):
                base = pl.multiple_of((s // 8) * 8, 8)
                pick = lax.broadcasted_iota(jnp.int32, (8, GLA_DK), 0) == s % 8
                b_s = jnp.sum(jnp.where(pick, b_ref[pl.ds(base, 8), ks], 0.0), axis=0, keepdims=True)
                k_s = jnp.sum(jnp.where(pick, k_ref[pl.ds(base, 8), ks], 0.0), axis=0, keepdims=True)
                decay = jnp.exp(jnp.minimum(b_ref[:, ks] - b_s, 0.0))
                column = jnp.sum(q_ref[:, ks] * (GLA_DK ** -0.5) * k_s * decay, axis=1, keepdims=True)
                return jnp.where(col == s, column, sc)

            scores = lax.fori_loop(0, c, key_column, jnp.zeros((c, c), F32))
        scores = jnp.where(keep, scores, 0.0)
        o_ref[:, vs] = (jnp.dot(scores.astype(BF16), vh, preferred_element_type=F32)
                        + jnp.dot(qh, state.astype(BF16), preferred_element_type=F32))
        e_col = jnp.sum(jnp.where(eye, jnp.broadcast_to(e_end[:, ks], (GLA_DK, GLA_DK)), 0.0),
                        axis=1, keepdims=True)
        s_ref[d, h] = state * e_col + lax.dot_general(
            k_end[:, ks].astype(BF16), vh, TN_DIMS, preferred_element_type=F32)


def _gla_scan_kernel(*refs, n_chunks, has_s0, write_state):
    qf, kf, vf, laf, qb, kb, vb, lab = refs[:8]
    rest = list(refs[8:])
    s0_ref = rest.pop(0) if has_s0 else None
    of_ref, ob_ref = rest.pop(0), rest.pop(0)
    sout_ref = rest.pop(0) if write_state else None
    s_ref, b_ref = rest
    i = pl.program_id(1)

    @pl.when(i == 0)
    def _():
        s_ref[...] = s0_ref[...] if has_s0 else jnp.zeros(s_ref.shape, F32)

    b_f = _gla_cum_decay(laf, True)
    b_b = _gla_cum_decay(lab, False)
    c = GLA_CHUNK
    total_decay = jnp.minimum(jnp.min(b_f[c - 1:c, :]), jnp.min(b_b[0:1, :]))
    factorisable = total_decay >= -GLA_MAX_FACTORISED_DECAY

    def step(factorised):
        _gla_direction(qf, kf, vf, b_f, of_ref, s_ref, b_ref.at[0], 0, True, factorised)
        _gla_direction(qb, kb, vb, b_b, ob_ref, s_ref, b_ref.at[1], 1, False, factorised)

    pl.when(factorisable)(functools.partial(step, True))
    pl.when(jnp.logical_not(factorisable))(functools.partial(step, False))

    if write_state:
        @pl.when(i == n_chunks - 1)
        def _():
            sout_ref[...] = s_ref[...]


def _gla_scan(proj, la, row0, n_batch, seq, s0):
    c = GLA_CHUNK
    n = seq // c
    base = row0 // c
    has_s0 = s0 is not None
    write_state = not has_s0
    fwd = lambda b, i: base + b * n + i
    bwd = lambda b, i: base + b * n + (n - 1 - i)

    def specs(blk):
        return [pl.BlockSpec((c, GLA_DK_TOT), lambda b, i: (blk(b, i), 0)),
                pl.BlockSpec((c, GLA_DK_TOT), lambda b, i: (blk(b, i), 1)),
                pl.BlockSpec((c, GLA_DV_TOT), lambda b, i: (blk(b, i), 1))]
    state_block = (None, 2, GLA_HEADS, GLA_DK, GLA_DV)
    state_spec = pl.BlockSpec(state_block, lambda b, i: (b, 0, 0, 0, 0))
    in_specs = (specs(fwd) + [pl.BlockSpec((c, GLA_DK_TOT), lambda b, i: (fwd(b, i), 0))]
                + specs(bwd) + [pl.BlockSpec((c, GLA_DK_TOT), lambda b, i: (bwd(b, i), 1))])
    args = [proj, proj, proj, la, proj, proj, proj, la]
    if has_s0:
        in_specs.append(state_spec)
        args.append(s0)
    out_rows = n_batch * seq
    out_specs = [pl.BlockSpec((c, GLA_DV_TOT), lambda b, i: (b * n + i, 0)),
                 pl.BlockSpec((c, GLA_DV_TOT), lambda b, i: (b * n + (n - 1 - i), 0))]
    out_shape = [jax.ShapeDtypeStruct((out_rows, GLA_DV_TOT), F32)] * 2
    if write_state:
        out_specs.append(state_spec)
        out_shape.append(jax.ShapeDtypeStruct((n_batch, 2, GLA_HEADS, GLA_DK, GLA_DV), F32))
    return pl.pallas_call(
        functools.partial(_gla_scan_kernel, n_chunks=n, has_s0=has_s0, write_state=write_state),
        grid=(n_batch, n),
        in_specs=in_specs,
        out_specs=out_specs,
        out_shape=out_shape,
        scratch_shapes=[pltpu.VMEM((2, GLA_HEADS, GLA_DK, GLA_DV), F32),
                        pltpu.VMEM((2, c, GLA_DK_TOT), F32)],
        compiler_params=_params("arbitrary", "arbitrary"),
        name="gla_scan",
    )(*args)


def _gla_post_kernel(of_ref, ob_ref, r_ref, g_ref, o_ref):
    o = of_ref[...] + ob_ref[...]
    normed = jnp.concatenate(
        [_rms(o[:, h * GLA_DV:(h + 1) * GLA_DV]) for h in range(GLA_HEADS)], axis=1)
    o_ref[...] = (normed * g_ref[...] * _silu(r_ref[...])).astype(o_ref.dtype)


def _gla_post(o_f, o_b, proj, g_tiled, tm=1024):
    return pl.pallas_call(
        _gla_post_kernel,
        grid=(N_TOK // tm,),
        in_specs=[pl.BlockSpec((tm, GLA_DV_TOT), lambda i: (i, 0)),
                  pl.BlockSpec((tm, GLA_DV_TOT), lambda i: (i, 0)),
                  pl.BlockSpec((tm, GLA_DV_TOT), lambda i: (i, 2)),
                  pl.BlockSpec((1, GLA_DV_TOT), lambda i: (0, 0))],
        out_specs=pl.BlockSpec((tm, GLA_DV_TOT), lambda i: (i, 0)),
        out_shape=jax.ShapeDtypeStruct((N_TOK, GLA_DV_TOT), BF16),
        compiler_params=_params("arbitrary"),
        name="gla_post",
    )(o_f, o_b, proj, g_tiled)


def _rope_pair(u, cs):
    t = u * cs
    return t + pltpu.roll(t, QK_ROPE, axis=1)


def _mla_down_kernel(a_ref, w_ref, qn_ref, kvn_ref, cs_ref, cq_ref, ckv_ref, kr_ref, krp_ref):
    acc = jnp.dot(a_ref[...], w_ref[...], preferred_element_type=F32)
    cq_ref[...] = (_rms(acc[:, :Q_LORA]) * qn_ref[...]).astype(cq_ref.dtype)
    ckv_ref[...] = _rms(acc[:, Q_LORA:Q_LORA + KV_LORA]) * kvn_ref[...]
    u = acc[:, Q_LORA + KV_LORA:]
    kr_ref[...] = u
    lane = lax.broadcasted_iota(jnp.int32, u.shape, 1)
    krp_ref[...] = jnp.where(lane < QK_ROPE, _rope_pair(u, cs_ref[...]), 0.0).astype(krp_ref.dtype)


def _mla_down(h, w_ext, q_norm, kv_norm, cs, tm=512):
    wn = w_ext.shape[1]
    row = lambda i: (i, 0)
    fixed = lambda i: (0, 0)
    return pl.pallas_call(
        _mla_down_kernel,
        grid=(N_TOK // tm,),
        in_specs=[pl.BlockSpec((tm, D_MODEL), row), pl.BlockSpec((D_MODEL, wn), fixed),
                  pl.BlockSpec((1, Q_LORA), fixed), pl.BlockSpec((1, KV_LORA), fixed),
                  pl.BlockSpec((tm, LANES), row)],
        out_specs=[pl.BlockSpec((tm, Q_LORA), row), pl.BlockSpec((tm, KV_LORA), row),
                   pl.BlockSpec((tm, LANES), row), pl.BlockSpec((tm, LANES), row)],
        out_shape=[jax.ShapeDtypeStruct((N_TOK, Q_LORA), BF16),
                   jax.ShapeDtypeStruct((N_TOK, KV_LORA), F32),
                   jax.ShapeDtypeStruct((N_TOK, LANES), F32),
                   jax.ShapeDtypeStruct((N_TOK, LANES), BF16)],
        compiler_params=_params("arbitrary"),
        name="mla_down",
    )(h, w_ext, q_norm.reshape(1, Q_LORA), kv_norm.reshape(1, KV_LORA), cs)


def _mla_q_kernel(a_ref, w_ref, cs_ref, o_ref):
    acc = jnp.dot(a_ref[...], w_ref[...], preferred_element_type=F32)
    cs = cs_ref[...]
    for h in range(MLA_HEADS):
        lo = h * 2 * LANES
        o_ref[:, lo:lo + LANES] = acc[:, lo:lo + LANES].astype(o_ref.dtype)
        o_ref[:, lo + LANES:lo + 2 * LANES] = _rope_pair(
            acc[:, lo + LANES:lo + 2 * LANES], cs).astype(o_ref.dtype)


def _mla_q(cq, w_ext, cs, tm=512):
    wn = w_ext.shape[1]
    return pl.pallas_call(
        _mla_q_kernel,
        grid=(N_TOK // tm,),
        in_specs=[pl.BlockSpec((tm, Q_LORA), lambda i: (i, 0)),
                  pl.BlockSpec((Q_LORA, wn), lambda i: (0, 0)),
                  pl.BlockSpec((tm, LANES), lambda i: (i, 0))],
        out_specs=pl.BlockSpec((tm, wn), lambda i: (i, 0)),
        out_shape=jax.ShapeDtypeStruct((N_TOK, wn), BF16),
        compiler_params=_params("arbitrary"),
        name="mla_q",
    )(cq, w_ext, cs)


def _attn_kernel(q_ref, kv_ref, kr_ref, o_ref, *, key_chunk, heads):
    tq = q_ref.shape[0]
    c = MLA_SCALE * LOG2E
    hw = QK_NOPE + V_HEAD
    for h in range(heads):
        q = q_ref[:, h * hw:(h + 1) * hw]
        m = jnp.full((tq, 1), -jnp.inf, F32)
        l = jnp.zeros((tq, 1), F32)
        acc = jnp.zeros((tq, V_HEAD), F32)
        for c0 in range(0, kv_ref.shape[0], key_chunk):
            rows = slice(c0, c0 + key_chunk)
            k = jnp.concatenate([kv_ref[rows, h * hw:h * hw + QK_NOPE], kr_ref[rows, :]], axis=1)
            s = lax.dot_general(q, k, NT_DIMS, preferred_element_type=F32)
            m_new = jnp.maximum(m, jnp.max(s, axis=1, keepdims=True))
            alpha = jnp.exp2((m - m_new) * c)
            p = jnp.exp2((s - m_new) * c)
            l = alpha * l + jnp.sum(p, axis=1, keepdims=True)
            acc = alpha * acc + jnp.dot(p.astype(BF16), kv_ref[rows, h * hw + QK_NOPE:(h + 1) * hw],
                                        preferred_element_type=F32)
            m = m_new
        o_ref[:, h * V_HEAD:(h + 1) * V_HEAD] = (acc / l).astype(o_ref.dtype)


def _attention(q, kv, krp, row0, n_batch, seq, n_keys, tq, heads):
    nq = seq // tq
    base = row0 // tq
    hw = heads * (QK_NOPE + V_HEAD)
    return pl.pallas_call(
        functools.partial(_attn_kernel, key_chunk=min(n_keys, ATTN_KEY_CHUNK), heads=heads),
        grid=(n_batch, MLA_HEADS // heads, nq),
        in_specs=[pl.BlockSpec((tq, hw), lambda b, h, i: (base + b * nq + i, h)),
                  pl.BlockSpec((n_keys, hw), lambda b, h, i: (b, h)),
                  pl.BlockSpec((n_keys, LANES), lambda b, h, i: (b, 0))],
        out_specs=pl.BlockSpec((tq, heads * V_HEAD), lambda b, h, i: (b * nq + i, h)),
        out_shape=jax.ShapeDtypeStruct((n_batch * seq, MLA_HEADS * V_HEAD), BF16),
        compiler_params=_params("arbitrary", "arbitrary", "arbitrary"),
        name="mla_attention",
    )(q, kv, krp)


def _rope_tables():
    half = QK_ROPE // 2
    pos = np.arange(DEC_SEQ)
    inv = ROPE_BASE ** (-np.arange(0, half, 2, dtype=np.float64) / half)
    ang = np.concatenate([(pos // GRID_W)[:, None] * inv, (pos % GRID_W)[:, None] * inv], axis=-1)
    ang = ang.astype(np.float32)
    cos = np.repeat(np.cos(ang), 2, axis=1)
    sin = np.repeat(np.sin(ang), 2, axis=1) * np.tile(np.array([-1.0, 1.0], np.float32), half)
    cs = np.concatenate([cos, sin], axis=1).astype(np.float32)
    ident = np.concatenate([np.ones((N_PROMPT, QK_ROPE), np.float32),
                            np.zeros((N_PROMPT, QK_ROPE), np.float32)], axis=1)
    return jnp.asarray(np.concatenate([ident, np.tile(cs, (DEC_BATCH, 1))], axis=0))


def kernel(x_prompt, x_sample, c, state_gla, cache_ckv, cache_krope, c_ctx, ada_w, ada_b, norm_mix, norm_ffn, norm_final, gla_w_in, gla_w_gate_up, gla_b_gate, gla_norm, gla_w_out, mla_w_down, mla_q_norm, mla_kv_norm, mla_w_uq, mla_w_ukv, mla_w_out, ffn_w_gate_up, ffn_w_down, moe_router, moe_w_gate_up, moe_w_down):
    x = jnp.concatenate([x_prompt.reshape(N_PROMPT, D_MODEL), x_sample.reshape(N_SAMPLE, D_MODEL)])
    cond = jnp.concatenate([c_ctx[None], c, jnp.zeros((N_COND - 1 - DEC_BATCH, D_MODEL), F32)])
    mods_all = _ada_mods(cond, ada_w, ada_b).reshape(DEPTH, N_COND, 6, 1, D_MODEL)
    cs = _rope_tables()
    swap = np.arange(QK_ROPE) ^ 1

    new_gla, new_ckv, new_kr = [], [], []
    for layer in range(DEPTH):
        j = layer // 2
        mods = mods_all[layer]
        h = _modulate(x, norm_mix[layer], mods, 0, 1)
        if layer % 2 == 0:
            w_in = jnp.pad(gla_w_in[j], ((0, 0), (0, GLA_PROJ_W - gla_w_in.shape[2]))).astype(BF16)
            proj = _matmul(h, w_in, F32, tm=1024, tn=640)
            pad_lo = jnp.zeros((GATE_RANK, GLA_DK_TOT), F32)
            pad_hi = jnp.zeros((LANES - 2 * GATE_RANK, GLA_DK_TOT), F32)
            w0 = jnp.concatenate([gla_w_gate_up[j, 0], pad_lo, pad_hi])
            w1 = jnp.concatenate([pad_lo, gla_w_gate_up[j, 1], pad_hi])
            la = _gla_gates(proj, w0, w1, gla_b_gate[j])
            of_p, ob_p, st = _gla_scan(proj, la, 0, BATCH, SEQ, None)
            of_s, ob_s = _gla_scan(proj, la, N_PROMPT, DEC_BATCH, DEC_SEQ, state_gla[:, j])
            new_gla.append(st)
            mixed = _gla_post(jnp.concatenate([of_p, of_s]), jnp.concatenate([ob_p, ob_s]), proj,
                              jnp.tile(gla_norm[j], GLA_HEADS).reshape(1, GLA_DV_TOT))
            x = _matmul_residual(mixed, gla_w_out[j].astype(BF16), x, mods, 2)
        else:
            wd = mla_w_down[j]
            w_down = jnp.concatenate([wd, wd[:, Q_LORA + KV_LORA:][:, swap]], axis=1).astype(BF16)
            cq, ckv, kr, krp = _mla_down(h, w_down, mla_q_norm[j], mla_kv_norm[j], cs)
            wq = mla_w_uq[j].reshape(Q_LORA, MLA_HEADS, QK_NOPE + QK_ROPE)
            w_uq = jnp.concatenate([wq, wq[:, :, QK_NOPE:][:, :, swap]], axis=2)
            q = _mla_q(cq, w_uq.reshape(Q_LORA, -1).astype(BF16), cs)
            w_ukv = mla_w_ukv[j].astype(BF16)
            ckv_b = ckv.astype(BF16)
            kv_p = _matmul(ckv_b[:N_PROMPT], w_ukv, BF16, tm=1024, tn=1024)
            o_p = _attention(q, kv_p, krp[:N_PROMPT], 0, BATCH, SEQ, SEQ, tq=SEQ, heads=MLA_HEADS)
            n_keys = PAST_LEN + DEC_SEQ
            ckv_keys = jnp.concatenate(
                [cache_ckv[:, j].astype(BF16), ckv_b[N_PROMPT:].reshape(DEC_BATCH, DEC_SEQ, KV_LORA)],
                axis=1).reshape(DEC_BATCH * n_keys, KV_LORA)
            kr_ctx = jnp.pad(cache_krope[:, j], ((0, 0), (0, 0), (0, LANES - QK_ROPE))).astype(BF16)
            krp_keys = jnp.concatenate(
                [kr_ctx, krp[N_PROMPT:].reshape(DEC_BATCH, DEC_SEQ, LANES)],
                axis=1).reshape(DEC_BATCH * n_keys, LANES)
            kv_s = _matmul(ckv_keys, w_ukv, BF16, tm=1024, tn=1024)
            o_s = _attention(q, kv_s, krp_keys, N_PROMPT, DEC_BATCH, DEC_SEQ, n_keys, tq=1024, heads=1)
            new_ckv.append(ckv[:N_PROMPT].reshape(BATCH, SEQ, KV_LORA))
            new_kr.append(kr[:N_PROMPT, :QK_ROPE].reshape(BATCH, SEQ, QK_ROPE))
            x = _matmul_residual(jnp.concatenate([o_p, o_s]), mla_w_out[j].astype(BF16), x, mods, 2)

        if layer % 2 == 0:
            h = _modulate(x, norm_ffn[layer], mods, 3, 4)
            act = _swiglu_up(h, ffn_w_gate_up[j].astype(BF16), tm=1024, tn=1408)
            x = _matmul_residual(act, ffn_w_down[j].astype(BF16), x, mods, 5)
        else:
            router_pad = jnp.pad(moe_router[j], ((0, 0), (0, LANES - N_EXPERTS)))
            h, gates, slots, cnt = _modulate_router(x, norm_ffn[layer], mods, 3, 4, router_pad)
            counts = cnt[:, 0, :N_EXPERTS].astype(jnp.int32).reshape(-1)
            y = None
            for e in range(N_EXPERTS):
                y = _moe_expert(h, gates, slots, counts, moe_w_gate_up[j, e].astype(BF16),
                                moe_w_down[j, e].astype(BF16), y, e)
            x = _gated_add(x, y, mods, 5)

    y = _final_norm(x, norm_final)
    return (y[:N_PROMPT].reshape(BATCH, SEQ, D_MODEL),
            y[N_PROMPT:].reshape(DEC_BATCH, DEC_SEQ, D_MODEL),
            jnp.stack(new_gla, axis=1),
            jnp.stack(new_ckv, axis=1),
            jnp.stack(new_kr, axis=1))
```

```python
import functools

import jax
import jax.numpy as jnp
import numpy as np
from jax import lax
from jax.experimental import pallas as pl
from jax.experimental.pallas import tpu as pltpu

D_MODEL = 1024
BATCH = 16
SEQ = 256
DEPTH = 4
DEC_BATCH = 4
DEC_SEQ = 2048
PAST_LEN = 512
GRID_W = 64
GLA_HEADS = 4
GLA_DK = 128
GLA_DV = 256
GLA_DK_TOT = GLA_HEADS * GLA_DK
GLA_DV_TOT = GLA_HEADS * GLA_DV
GATE_RANK = 16
GATE_TEMP = 16.0
MLA_HEADS = 8
Q_LORA = 384
KV_LORA = 256
QK_NOPE = 128
QK_ROPE = 64
V_HEAD = 128
MLA_SCALE = (QK_NOPE + QK_ROPE) ** -0.5
ROPE_BASE = 10000.0
D_FF = 2816
N_EXPERTS = 8
D_EXPERT = 3584
EPS = 1e-6

N_PROMPT = BATCH * SEQ
N_SAMPLE = DEC_BATCH * DEC_SEQ
N_TOK = N_PROMPT + N_SAMPLE
N_COND = 8
LANES = 128
GLA_QKVR_W = 2 * GLA_DK_TOT + 2 * GLA_DV_TOT
GLA_CHUNK = 256
GLA_MAX_FACTORISED_DECAY = 80.0
ATTN_KEY_CHUNK = 512
MOE_BLOCK = 1024
MOE_TILE = 256
MOE_W_CHUNKS = 28
LOG2E = 1.4426950408889634
VMEM_LIMIT = 56 * 1024 * 1024

F32 = jnp.float32
BF16 = jnp.bfloat16
HIGHEST = lax.Precision.HIGHEST
NT_DIMS = (((1,), (1,)), ((), ()))
TN_DIMS = (((0,), (0,)), ((), ()))


def _params(*sem):
    return pltpu.CompilerParams(dimension_semantics=sem, vmem_limit_bytes=VMEM_LIMIT)


def _cond_row(i, tm):
    return jnp.where(i * tm < N_PROMPT, 0, 1 + jnp.maximum(i * tm - N_PROMPT, 0) // DEC_SEQ)


def _mod_spec(which, tm, tn=D_MODEL, col=None):
    if col is None:
        return pl.BlockSpec((None, None, 1, tn), lambda i, *_: (_cond_row(i, tm), which, 0, 0))
    return pl.BlockSpec((None, None, 1, tn), lambda i, j: (_cond_row(i, tm), which, 0, j))


def _silu(x):
    return x / (1.0 + jnp.exp(-x))


def _rms(x):
    return x * lax.rsqrt(jnp.mean(x * x, axis=-1, keepdims=True) + EPS)


def _ada_kernel(c_ref, w_ref, b_ref, o_ref):
    o_ref[...] = jnp.dot(_silu(c_ref[...]), w_ref[...], precision=HIGHEST,
                         preferred_element_type=F32) + b_ref[...]


def _ada_mods(cond, ada_w, ada_b):
    tn = 1536
    return pl.pallas_call(
        _ada_kernel,
        grid=(DEPTH, 6 * D_MODEL // tn),
        in_specs=[pl.BlockSpec((N_COND, D_MODEL), lambda l, j: (0, 0)),
                  pl.BlockSpec((None, D_MODEL, tn), lambda l, j: (l, 0, j)),
                  pl.BlockSpec((None, 1, tn), lambda l, j: (l, 0, j))],
        out_specs=pl.BlockSpec((None, N_COND, tn), lambda l, j: (l, 0, j)),
        out_shape=jax.ShapeDtypeStruct((DEPTH, N_COND, 6 * D_MODEL), F32),
        compiler_params=_params("arbitrary", "arbitrary"),
        name="ada_mods",
    )(cond, ada_w, ada_b.reshape(DEPTH, 1, 6 * D_MODEL))


def _modulate_kernel(x_ref, g_ref, sh_ref, sc_ref, o_ref):
    y = _rms(x_ref[...]) * g_ref[...]
    o_ref[...] = (y * (1.0 + sc_ref[...]) + sh_ref[...]).astype(o_ref.dtype)


def _modulate(x, g, mods, shift_idx, scale_idx, tm=1024):
    return pl.pallas_call(
        _modulate_kernel,
        grid=(N_TOK // tm,),
        in_specs=[pl.BlockSpec((tm, D_MODEL), lambda i: (i, 0)),
                  pl.BlockSpec((1, D_MODEL), lambda i: (0, 0)),
                  _mod_spec(shift_idx, tm), _mod_spec(scale_idx, tm)],
        out_specs=pl.BlockSpec((tm, D_MODEL), lambda i: (i, 0)),
        out_shape=jax.ShapeDtypeStruct((N_TOK, D_MODEL), BF16),
        compiler_params=_params("arbitrary"),
        name="modulate",
    )(x, g.reshape(1, D_MODEL), mods, mods)


def _modulate_router_kernel(x_ref, g_ref, sh_ref, sc_ref, r_ref, o_ref, gates_ref, slot_ref, cnt_ref):
    y = _rms(x_ref[...]) * g_ref[...]
    h = y * (1.0 + sc_ref[...]) + sh_ref[...]
    o_ref[...] = h.astype(o_ref.dtype)
    logits = jnp.dot(h, r_ref[...], precision=HIGHEST, preferred_element_type=F32)
    lane = lax.broadcasted_iota(jnp.int32, logits.shape, 1)
    l1 = jnp.where(lane < N_EXPERTS, logits, -jnp.inf)
    m1 = jnp.max(l1, axis=1, keepdims=True)
    i1 = jnp.min(jnp.where(l1 == m1, lane, LANES), axis=1, keepdims=True)
    l2 = jnp.where(lane == i1, -jnp.inf, l1)
    m2 = jnp.max(l2, axis=1, keepdims=True)
    i2 = jnp.min(jnp.where(l2 == m2, lane, LANES), axis=1, keepdims=True)
    e = jnp.exp(m2 - m1)
    w1 = 1.0 / (1.0 + e)
    w2 = e / (1.0 + e)
    gates_ref[...] = jnp.where(lane == i1, w1, 0.0) + jnp.where(lane == i2, w2, 0.0)
    sel = (lane == i1) | (lane == i2)
    tb = logits.shape[0]
    earlier = (lax.broadcasted_iota(jnp.int32, (tb, tb), 1)
               < lax.broadcasted_iota(jnp.int32, (tb, tb), 0)).astype(BF16)
    before = jnp.dot(earlier, sel.astype(BF16), preferred_element_type=F32)
    slot_ref[...] = jnp.where(sel, before + 1.0, 0.0)
    cnt_ref[...] = jnp.broadcast_to(jnp.sum(sel.astype(F32), axis=0, keepdims=True), cnt_ref.shape)


def _modulate_router(x, g, mods, shift_idx, scale_idx, router_pad):
    tm = MOE_BLOCK
    nb = N_TOK // tm
    return pl.pallas_call(
        _modulate_router_kernel,
        grid=(nb,),
        in_specs=[pl.BlockSpec((tm, D_MODEL), lambda i: (i, 0)),
                  pl.BlockSpec((1, D_MODEL), lambda i: (0, 0)),
                  _mod_spec(shift_idx, tm), _mod_spec(scale_idx, tm),
                  pl.BlockSpec((D_MODEL, LANES), lambda i: (0, 0))],
        out_specs=[pl.BlockSpec((tm, D_MODEL), lambda i: (i, 0)),
                   pl.BlockSpec((tm, LANES), lambda i: (i, 0)),
                   pl.BlockSpec((tm, LANES), lambda i: (i, 0)),
                   pl.BlockSpec((None, 8, LANES), lambda i: (i, 0, 0))],
        out_shape=[jax.ShapeDtypeStruct((N_TOK, D_MODEL), BF16),
                   jax.ShapeDtypeStruct((N_TOK, LANES), F32),
                   jax.ShapeDtypeStruct((N_TOK, LANES), F32),
                   jax.ShapeDtypeStruct((nb, 8, LANES), F32)],
        compiler_params=_params("arbitrary"),
        name="modulate_router",
    )(x, g.reshape(1, D_MODEL), mods, mods, router_pad)


def _moe_expert_kernel(cnt_ref, h_ref, gates_ref, slot_ref, wgu_ref, wd_ref, *rest, expert, first):
    wgu_s, wd_s = rest[-2:]
    y_ref = rest[-3]
    step = pl.program_id(0)
    half = MOE_W_CHUNKS // 2

    wc = wgu_ref.shape[1]
    for c in range(MOE_W_CHUNKS):
        @pl.when(step == c)
        def _(c=c):
            wgu_s[:, c * wc:(c + 1) * wc] = wgu_ref[...].astype(BF16)

    @pl.when(step < half)
    def _():
        wd_s[pl.ds(pl.multiple_of(step * wc, wc), wc), :] = wd_ref[...].astype(BF16)

    @pl.when(step >= MOE_W_CHUNKS)
    def _():
        if first:
            y_ref[...] = jnp.zeros(y_ref.shape, F32)
        else:
            y_ref[...] = rest[0][...]
        tb = h_ref.shape[0]
        gate_col = gates_ref[:, expert:expert + 1]
        slot_col = slot_ref[:, expert:expert + 1]
        count = cnt_ref[(step - MOE_W_CHUNKS) * N_EXPERTS + expert]
        cw = 256

        def tile(first_slot, rows):
            lane = lax.broadcasted_iota(jnp.int32, (tb, rows), 1).astype(F32)
            onehot = (slot_col == lane + (first_slot + 1).astype(F32)).astype(BF16)
            hg = lax.dot_general(onehot, h_ref[...], TN_DIMS,
                                 preferred_element_type=F32).astype(BF16)
            gu = jnp.dot(hg, wgu_s[...], preferred_element_type=F32)
            act = (_silu(gu[:, :D_EXPERT]) * gu[:, D_EXPERT:]).astype(BF16)
            out = jnp.dot(act, wd_s[...], preferred_element_type=F32).astype(BF16)
            for c0 in range(0, D_MODEL, cw):
                y_ref[:, c0:c0 + cw] += gate_col * jnp.dot(onehot, out[:, c0:c0 + cw],
                                                           preferred_element_type=F32)

        full = count // MOE_TILE
        left = count - full * MOE_TILE
        n_big = full + (left > MOE_TILE // 2).astype(jnp.int32)

        def big_tile(k, carry):
            tile(k * MOE_TILE, MOE_TILE)
            return carry

        lax.fori_loop(0, n_big, big_tile, 0)

        @pl.when(jnp.logical_and(left > 0, left <= MOE_TILE // 2))
        def _():
            tile(full * MOE_TILE, MOE_TILE // 2)


def _moe_expert(h, gates, slots, counts, w_gu, w_down, y, layer, expert):
    tb = MOE_BLOCK
    nc = MOE_W_CHUNKS
    wc = 2 * D_EXPERT // nc
    first = y is None
    blk = lambda s, cnt: (jnp.maximum(s - nc, 0), 0)
    in_specs = [pl.BlockSpec((tb, D_MODEL), blk), pl.BlockSpec((tb, LANES), blk),
                pl.BlockSpec((tb, LANES), blk),
                pl.BlockSpec((None, None, D_MODEL, wc),
                             lambda s, cnt: (layer, expert, 0, jnp.minimum(s, nc - 1))),
                pl.BlockSpec((None, None, wc, D_MODEL),
                             lambda s, cnt: (layer, expert, jnp.minimum(s, nc // 2 - 1), 0))]
    args = [h, gates, slots, w_gu, w_down]
    if not first:
        in_specs.append(pl.BlockSpec((tb, D_MODEL), blk))
        args.append(y)
    return pl.pallas_call(
        functools.partial(_moe_expert_kernel, expert=expert, first=first),
        grid_spec=pltpu.PrefetchScalarGridSpec(
            num_scalar_prefetch=1, grid=(nc + N_TOK // tb,), in_specs=in_specs,
            out_specs=pl.BlockSpec((tb, D_MODEL), blk),
            scratch_shapes=[pltpu.VMEM((D_MODEL, 2 * D_EXPERT), BF16),
                            pltpu.VMEM((D_EXPERT, D_MODEL), BF16)]),
        out_shape=jax.ShapeDtypeStruct((N_TOK, D_MODEL), F32),
        compiler_params=_params("arbitrary"),
        name="moe_expert",
    )(counts, *args)


def _mm_kernel(a_ref, w_ref, o_ref):
    o_ref[...] = jnp.dot(a_ref[...], w_ref[...], preferred_element_type=F32).astype(o_ref.dtype)


def _matmul(a, w, out_dtype, tm, tn):
    m, k = a.shape
    n = w.shape[1]
    return pl.pallas_call(
        _mm_kernel,
        grid=(m // tm, n // tn),
        in_specs=[pl.BlockSpec((tm, k), lambda i, j: (i, 0)),
                  pl.BlockSpec((k, tn), lambda i, j: (0, j))],
        out_specs=pl.BlockSpec((tm, tn), lambda i, j: (i, j)),
        out_shape=jax.ShapeDtypeStruct((m, n), out_dtype),
        compiler_params=_params("arbitrary", "arbitrary"),
        name="matmul",
    )(a, w)


def _swiglu_kernel(a_ref, wg_ref, wu_ref, o_ref):
    a = a_ref[...]
    g = jnp.dot(a, wg_ref[...], preferred_element_type=F32)
    u = jnp.dot(a, wu_ref[...], preferred_element_type=F32)
    o_ref[...] = (_silu(g) * u).astype(o_ref.dtype)


def _swiglu_up(a, w_gu, tm, tn):
    m, k = a.shape
    f = w_gu.shape[1] // 2
    nj = f // tn
    return pl.pallas_call(
        _swiglu_kernel,
        grid=(m // tm, nj),
        in_specs=[pl.BlockSpec((tm, k), lambda i, j: (i, 0)),
                  pl.BlockSpec((k, tn), lambda i, j: (0, j)),
                  pl.BlockSpec((k, tn), lambda i, j: (0, j + nj))],
        out_specs=pl.BlockSpec((tm, tn), lambda i, j: (i, j)),
        out_shape=jax.ShapeDtypeStruct((m, f), BF16),
        compiler_params=_params("arbitrary", "arbitrary"),
        name="swiglu_up",
    )(a, w_gu, w_gu)


def _mm_residual_kernel(a_ref, w_ref, x_ref, g_ref, o_ref):
    y = jnp.dot(a_ref[...], w_ref[...], preferred_element_type=F32)
    o_ref[...] = x_ref[...] + g_ref[...] * y


def _matmul_residual(a, w, x, mods, gate_idx, tm=1024, tn=512):
    k = a.shape[1]
    return pl.pallas_call(
        _mm_residual_kernel,
        grid=(N_TOK // tm, D_MODEL // tn),
        in_specs=[pl.BlockSpec((tm, k), lambda i, j: (i, 0)),
                  pl.BlockSpec((k, tn), lambda i, j: (0, j)),
                  pl.BlockSpec((tm, tn), lambda i, j: (i, j)),
                  _mod_spec(gate_idx, tm, tn, col=True)],
        out_specs=pl.BlockSpec((tm, tn), lambda i, j: (i, j)),
        out_shape=jax.ShapeDtypeStruct((N_TOK, D_MODEL), F32),
        compiler_params=_params("arbitrary", "arbitrary"),
        name="matmul_residual",
    )(a, w, x, mods)


def _mm_residual_split_kernel(ap_ref, as_ref, w_ref, x_ref, g_ref, o_ref):
    is_prompt = pl.program_id(0) < N_PROMPT // o_ref.shape[0]
    a = jnp.where(is_prompt, ap_ref[...], as_ref[...])
    o_ref[...] = x_ref[...] + g_ref[...] * jnp.dot(a, w_ref[...], preferred_element_type=F32)


def _matmul_residual_split(a_prompt, a_sample, w, x, mods, gate_idx, tm=1024, tn=512):
    k = w.shape[0]
    return pl.pallas_call(
        _mm_residual_split_kernel,
        grid=(N_TOK // tm, D_MODEL // tn),
        in_specs=(_prompt_sample_specs(tm, k)
                  + [pl.BlockSpec((k, tn), lambda i, j: (0, j)),
                     pl.BlockSpec((tm, tn), lambda i, j: (i, j)),
                     _mod_spec(gate_idx, tm, tn, col=True)]),
        out_specs=pl.BlockSpec((tm, tn), lambda i, j: (i, j)),
        out_shape=jax.ShapeDtypeStruct((N_TOK, D_MODEL), F32),
        compiler_params=_params("arbitrary", "arbitrary"),
        name="matmul_residual_split",
    )(a_prompt, a_sample, w, x, mods)


def _gated_add_kernel(x_ref, y_ref, g_ref, o_ref):
    o_ref[...] = x_ref[...] + g_ref[...] * y_ref[...]


def _gated_add(x, y, mods, gate_idx, tm=1024):
    return pl.pallas_call(
        _gated_add_kernel,
        grid=(N_TOK // tm,),
        in_specs=[pl.BlockSpec((tm, D_MODEL), lambda i: (i, 0)),
                  pl.BlockSpec((tm, D_MODEL), lambda i: (i, 0)),
                  _mod_spec(gate_idx, tm)],
        out_specs=pl.BlockSpec((tm, D_MODEL), lambda i: (i, 0)),
        out_shape=jax.ShapeDtypeStruct((N_TOK, D_MODEL), F32),
        compiler_params=_params("arbitrary"),
        name="gated_add",
    )(x, y, mods)


def _final_norm_kernel(x_ref, g_ref, o_ref):
    o_ref[...] = _rms(x_ref[...]) * g_ref[...]


def _final_norm(x, g, tm=1024):
    return pl.pallas_call(
        _final_norm_kernel,
        grid=(N_TOK // tm,),
        in_specs=[pl.BlockSpec((tm, D_MODEL), lambda i: (i, 0)),
                  pl.BlockSpec((1, D_MODEL), lambda i: (0, 0))],
        out_specs=pl.BlockSpec((tm, D_MODEL), lambda i: (i, 0)),
        out_shape=jax.ShapeDtypeStruct((N_TOK, D_MODEL), F32),
        compiler_params=_params("arbitrary"),
        name="final_norm",
    )(x, g.reshape(1, D_MODEL))


def _log_sigmoid(x):
    return jnp.minimum(x, 0.0) - jnp.log(1.0 + jnp.exp(-jnp.abs(x)))


def _gla_gate_kernel(gd_ref, w0_ref, w1_ref, b_ref, o_ref):
    gd = gd_ref[...]
    x0 = jnp.dot(gd, w0_ref[...], precision=HIGHEST, preferred_element_type=F32) + b_ref[0:1, :]
    x1 = jnp.dot(gd, w1_ref[...], precision=HIGHEST, preferred_element_type=F32) + b_ref[1:2, :]
    o_ref[:, :GLA_DK_TOT] = _log_sigmoid(x0) * (1.0 / GATE_TEMP)
    o_ref[:, GLA_DK_TOT:] = _log_sigmoid(x1) * (1.0 / GATE_TEMP)


def _gla_gates(gd, w0, w1, b_gate, tm=1024):
    return pl.pallas_call(
        _gla_gate_kernel,
        grid=(N_TOK // tm,),
        in_specs=[pl.BlockSpec((tm, LANES), lambda i: (i, 0)),
                  pl.BlockSpec((LANES, GLA_DK_TOT), lambda i: (0, 0)),
                  pl.BlockSpec((LANES, GLA_DK_TOT), lambda i: (0, 0)),
                  pl.BlockSpec((2, GLA_DK_TOT), lambda i: (0, 0))],
        out_specs=pl.BlockSpec((tm, 2 * GLA_DK_TOT), lambda i: (i, 0)),
        out_shape=jax.ShapeDtypeStruct((N_TOK, 2 * GLA_DK_TOT), F32),
        compiler_params=_params("arbitrary"),
        name="gla_gates",
    )(gd, w0, w1, b_gate)


def _gla_keep_mask(causal):
    c = GLA_CHUNK
    row = lax.broadcasted_iota(jnp.int32, (c, c), 0)
    col = lax.broadcasted_iota(jnp.int32, (c, c), 1)
    return (col <= row) if causal else (col >= row)


def _gla_cum_decay(la_ref, causal):
    mask = _gla_keep_mask(causal).astype(BF16)
    la = la_ref[...]
    hi = la.astype(BF16)
    lo = (la - hi.astype(F32)).astype(BF16)
    return (jnp.dot(mask, hi, preferred_element_type=F32)
            + jnp.dot(mask, lo, preferred_element_type=F32))


def _gla_direction(q_ref, k_ref, v_ref, b, o_ref, s_ref, b_ref, d, causal, factorised):
    c = GLA_CHUNK
    keep = _gla_keep_mask(causal)
    end = c - 1 if causal else 0
    b_end = b[end:end + 1, :]
    q_dec = q_ref[...].astype(F32) * (GLA_DK ** -0.5) * jnp.exp(b)
    k_end = k_ref[...] * jnp.exp(b_end - b)
    e_end = jnp.exp(b_end)
    eye = (lax.broadcasted_iota(jnp.int32, (GLA_DK, GLA_DK), 0)
           == lax.broadcasted_iota(jnp.int32, (GLA_DK, GLA_DK), 1))
    if factorised:
        k_inv = k_ref[...] * jnp.exp(-b)
    else:
        b_ref[...] = b
        col = lax.broadcasted_iota(jnp.int32, (c, c), 1)
    for h in range(GLA_HEADS):
        ks = slice(h * GLA_DK, (h + 1) * GLA_DK)
        vs = slice(h * GLA_DV, (h + 1) * GLA_DV)
        state = s_ref[d, h]
        qh = q_dec[:, ks].astype(BF16)
        vh = v_ref[:, vs].astype(BF16)
        if factorised:
            scores = lax.dot_general(qh, k_inv[:, ks].astype(BF16), NT_DIMS,
                                     preferred_element_type=F32)
        else:
            def key_column(s, sc, ks=ks):
                base = pl.multiple_of((s // 16) * 16, 16)
                pick = lax.broadcasted_iota(jnp.int32, (16, GLA_DK), 0) == s % 16
                b_s = jnp.sum(jnp.where(pick, b_ref[pl.ds(base, 16), ks], 0.0),
                              axis=0, keepdims=True)
                k_s = jnp.sum(jnp.where(pick, k_ref[pl.ds(base, 16), ks].astype(F32), 0.0),
                              axis=0, keepdims=True)
                decay = jnp.exp(jnp.minimum(b_ref[:, ks] - b_s, 0.0))
                column = jnp.sum(q_ref[:, ks].astype(F32) * (GLA_DK ** -0.5) * k_s * decay,
                                 axis=1, keepdims=True)
                return jnp.where(col == s, column, sc)

            scores = lax.fori_loop(0, c, key_column, jnp.zeros((c, c), F32))
        scores = jnp.where(keep, scores, 0.0)
        o_ref[:, vs] = (jnp.dot(scores.astype(BF16), vh, preferred_element_type=F32)
                        + jnp.dot(qh, state.astype(BF16), preferred_element_type=F32))
        e_col = jnp.sum(jnp.where(eye, jnp.broadcast_to(e_end[:, ks], (GLA_DK, GLA_DK)), 0.0),
                        axis=1, keepdims=True)
        s_ref[d, h] = state * e_col + lax.dot_general(
            k_end[:, ks].astype(BF16), vh, TN_DIMS, preferred_element_type=F32)


def _gla_scan_kernel(*refs, n_chunks, has_s0, write_state):
    qf, kf, vf, laf, qb, kb, vb, lab = refs[:8]
    rest = list(refs[8:])
    s0_ref = rest.pop(0) if has_s0 else None
    of_ref, ob_ref = rest.pop(0), rest.pop(0)
    sout_ref = rest.pop(0) if write_state else None
    s_ref, b_ref = rest
    i = pl.program_id(1)

    @pl.when(i == 0)
    def _():
        s_ref[...] = s0_ref[...] if has_s0 else jnp.zeros(s_ref.shape, F32)

    b_f = _gla_cum_decay(laf, True)
    b_b = _gla_cum_decay(lab, False)
    c = GLA_CHUNK
    total_decay = jnp.minimum(jnp.min(b_f[c - 1:c, :]), jnp.min(b_b[0:1, :]))
    factorisable = total_decay >= -GLA_MAX_FACTORISED_DECAY

    def step(factorised):
        _gla_direction(qf, kf, vf, b_f, of_ref, s_ref, b_ref.at[0], 0, True, factorised)
        _gla_direction(qb, kb, vb, b_b, ob_ref, s_ref, b_ref.at[1], 1, False, factorised)

    pl.when(factorisable)(functools.partial(step, True))
    pl.when(jnp.logical_not(factorisable))(functools.partial(step, False))

    if write_state:
        @pl.when(i == n_chunks - 1)
        def _():
            sout_ref[...] = s_ref[...]


def _gla_scan(proj, la, row0, n_batch, seq, s0):
    c = GLA_CHUNK
    n = seq // c
    base = row0 // c
    has_s0 = s0 is not None
    write_state = not has_s0
    fwd = lambda b, i: base + b * n + i
    bwd = lambda b, i: base + b * n + (n - 1 - i)

    def specs(blk):
        return [pl.BlockSpec((c, GLA_DK_TOT), lambda b, i: (blk(b, i), 0)),
                pl.BlockSpec((c, GLA_DK_TOT), lambda b, i: (blk(b, i), 1)),
                pl.BlockSpec((c, GLA_DV_TOT), lambda b, i: (blk(b, i), 1))]
    state_block = (None, 2, GLA_HEADS, GLA_DK, GLA_DV)
    state_spec = pl.BlockSpec(state_block, lambda b, i: (b, 0, 0, 0, 0))
    in_specs = (specs(fwd) + [pl.BlockSpec((c, GLA_DK_TOT), lambda b, i: (fwd(b, i), 0))]
                + specs(bwd) + [pl.BlockSpec((c, GLA_DK_TOT), lambda b, i: (bwd(b, i), 1))])
    args = [proj, proj, proj, la, proj, proj, proj, la]
    if has_s0:
        in_specs.append(state_spec)
        args.append(s0)
    out_rows = n_batch * seq
    out_specs = [pl.BlockSpec((c, GLA_DV_TOT), lambda b, i: (b * n + i, 0)),
                 pl.BlockSpec((c, GLA_DV_TOT), lambda b, i: (b * n + (n - 1 - i), 0))]
    out_shape = [jax.ShapeDtypeStruct((out_rows, GLA_DV_TOT), F32)] * 2
    if write_state:
        out_specs.append(state_spec)
        out_shape.append(jax.ShapeDtypeStruct((n_batch, 2, GLA_HEADS, GLA_DK, GLA_DV), F32))
    return pl.pallas_call(
        functools.partial(_gla_scan_kernel, n_chunks=n, has_s0=has_s0, write_state=write_state),
        grid=(n_batch, n),
        in_specs=in_specs,
        out_specs=out_specs,
        out_shape=out_shape,
        scratch_shapes=[pltpu.VMEM((2, GLA_HEADS, GLA_DK, GLA_DV), F32),
                        pltpu.VMEM((2, c, GLA_DK_TOT), F32)],
        compiler_params=_params("arbitrary", "arbitrary"),
        name="gla_scan",
    )(*args)


def _prompt_sample_specs(tm, width):
    n_p = N_PROMPT // tm
    return [pl.BlockSpec((tm, width), lambda i, *_: (jnp.minimum(i, n_p - 1), 0)),
            pl.BlockSpec((tm, width), lambda i, *_: (jnp.maximum(i - n_p, 0), 0))]


def _gla_post_kernel(ofp_ref, ofs_ref, obp_ref, obs_ref, r_ref, g_ref, o_ref):
    is_prompt = pl.program_id(0) < N_PROMPT // o_ref.shape[0]
    o = jnp.where(is_prompt, ofp_ref[...] + obp_ref[...], ofs_ref[...] + obs_ref[...])
    normed = jnp.concatenate(
        [_rms(o[:, h * GLA_DV:(h + 1) * GLA_DV]) for h in range(GLA_HEADS)], axis=1)
    o_ref[...] = (normed * g_ref[...] * _silu(r_ref[...].astype(F32))).astype(o_ref.dtype)


def _gla_post(of_p, of_s, ob_p, ob_s, proj, g_tiled, tm=1024):
    return pl.pallas_call(
        _gla_post_kernel,
        grid=(N_TOK // tm,),
        in_specs=(_prompt_sample_specs(tm, GLA_DV_TOT) + _prompt_sample_specs(tm, GLA_DV_TOT)
                  + [pl.BlockSpec((tm, GLA_DV_TOT), lambda i: (i, 2)),
                     pl.BlockSpec((1, GLA_DV_TOT), lambda i: (0, 0))]),
        out_specs=pl.BlockSpec((tm, GLA_DV_TOT), lambda i: (i, 0)),
        out_shape=jax.ShapeDtypeStruct((N_TOK, GLA_DV_TOT), BF16),
        compiler_params=_params("arbitrary"),
        name="gla_post",
    )(of_p, of_s, ob_p, ob_s, proj, g_tiled)


def _rope_pair(u, cs):
    t = u * cs
    return t + pltpu.roll(t, QK_ROPE, axis=1)


def _mla_down_kernel(a_ref, w_ref, qn_ref, kvn_ref, cs_ref, cq_ref, ckv_ref, kr_ref, krp_ref):
    acc = jnp.dot(a_ref[...], w_ref[...], preferred_element_type=F32)
    cq_ref[...] = (_rms(acc[:, :Q_LORA]) * qn_ref[...]).astype(cq_ref.dtype)
    ckv_ref[...] = _rms(acc[:, Q_LORA:Q_LORA + KV_LORA]) * kvn_ref[...]
    u = acc[:, Q_LORA + KV_LORA:]
    kr_ref[...] = u
    lane = lax.broadcasted_iota(jnp.int32, u.shape, 1)
    krp_ref[...] = jnp.where(lane < QK_ROPE, _rope_pair(u, cs_ref[...]), 0.0).astype(krp_ref.dtype)


def _mla_down(h, w_ext, q_norm, kv_norm, cs, tm=512):
    wn = w_ext.shape[1]
    row = lambda i: (i, 0)
    fixed = lambda i: (0, 0)
    return pl.pallas_call(
        _mla_down_kernel,
        grid=(N_TOK // tm,),
        in_specs=[pl.BlockSpec((tm, D_MODEL), row), pl.BlockSpec((D_MODEL, wn), fixed),
                  pl.BlockSpec((1, Q_LORA), fixed), pl.BlockSpec((1, KV_LORA), fixed),
                  pl.BlockSpec((tm, LANES), row)],
        out_specs=[pl.BlockSpec((tm, Q_LORA), row), pl.BlockSpec((tm, KV_LORA), row),
                   pl.BlockSpec((tm, LANES), row), pl.BlockSpec((tm, LANES), row)],
        out_shape=[jax.ShapeDtypeStruct((N_TOK, Q_LORA), BF16),
                   jax.ShapeDtypeStruct((N_TOK, KV_LORA), F32),
                   jax.ShapeDtypeStruct((N_TOK, LANES), F32),
                   jax.ShapeDtypeStruct((N_TOK, LANES), BF16)],
        compiler_params=_params("arbitrary"),
        name="mla_down",
    )(h, w_ext, q_norm.reshape(1, Q_LORA), kv_norm.reshape(1, KV_LORA), cs)


def _mla_q_kernel(a_ref, w_ref, cs_ref, o_ref):
    acc = jnp.dot(a_ref[...], w_ref[...], preferred_element_type=F32)
    cs = cs_ref[...]
    for h in range(MLA_HEADS):
        lo = h * 2 * LANES
        o_ref[:, lo:lo + LANES] = acc[:, lo:lo + LANES].astype(o_ref.dtype)
        o_ref[:, lo + LANES:lo + 2 * LANES] = _rope_pair(
            acc[:, lo + LANES:lo + 2 * LANES], cs).astype(o_ref.dtype)


def _mla_q(cq, w_ext, cs, tm=512):
    wn = w_ext.shape[1]
    return pl.pallas_call(
        _mla_q_kernel,
        grid=(N_TOK // tm,),
        in_specs=[pl.BlockSpec((tm, Q_LORA), lambda i: (i, 0)),
                  pl.BlockSpec((Q_LORA, wn), lambda i: (0, 0)),
                  pl.BlockSpec((tm, LANES), lambda i: (i, 0))],
        out_specs=pl.BlockSpec((tm, wn), lambda i: (i, 0)),
        out_shape=jax.ShapeDtypeStruct((N_TOK, wn), BF16),
        compiler_params=_params("arbitrary"),
        name="mla_q",
    )(cq, w_ext, cs)


def _attn_kernel(q_ref, kv_ref, kr_ref, o_ref, *, key_chunk, heads):
    tq = q_ref.shape[0]
    c = MLA_SCALE * LOG2E
    hw = QK_NOPE + V_HEAD
    for h in range(heads):
        q = q_ref[:, h * hw:(h + 1) * hw]
        m = jnp.full((tq, 1), -jnp.inf, F32)
        l = jnp.zeros((tq, 1), F32)
        acc = jnp.zeros((tq, V_HEAD), F32)
        for c0 in range(0, kv_ref.shape[0], key_chunk):
            rows = slice(c0, c0 + key_chunk)
            k = jnp.concatenate([kv_ref[rows, h * hw:h * hw + QK_NOPE], kr_ref[rows, :]], axis=1)
            s = lax.dot_general(q, k, NT_DIMS, preferred_element_type=F32)
            m_new = jnp.maximum(m, jnp.max(s, axis=1, keepdims=True))
            alpha = jnp.exp2((m - m_new) * c)
            p = jnp.exp2((s - m_new) * c)
            l = alpha * l + jnp.sum(p, axis=1, keepdims=True)
            acc = alpha * acc + jnp.dot(p.astype(BF16), kv_ref[rows, h * hw + QK_NOPE:(h + 1) * hw],
                                        preferred_element_type=F32)
            m = m_new
        o_ref[:, h * V_HEAD:(h + 1) * V_HEAD] = (acc / l).astype(o_ref.dtype)


def _attention(q, kv, krp, row0, n_batch, seq, n_keys, tq, heads):
    nq = seq // tq
    base = row0 // tq
    hw = heads * (QK_NOPE + V_HEAD)
    return pl.pallas_call(
        functools.partial(_attn_kernel, key_chunk=min(n_keys, ATTN_KEY_CHUNK), heads=heads),
        grid=(n_batch, MLA_HEADS // heads, nq),
        in_specs=[pl.BlockSpec((tq, hw), lambda b, h, i: (base + b * nq + i, h)),
                  pl.BlockSpec((n_keys, hw), lambda b, h, i: (b, h)),
                  pl.BlockSpec((n_keys, LANES), lambda b, h, i: (b, 0))],
        out_specs=pl.BlockSpec((tq, heads * V_HEAD), lambda b, h, i: (b * nq + i, h)),
        out_shape=jax.ShapeDtypeStruct((n_batch * seq, MLA_HEADS * V_HEAD), BF16),
        compiler_params=_params("arbitrary", "arbitrary", "arbitrary"),
        name="mla_attention",
    )(q, kv, krp)


def _rope_tables():
    half = QK_ROPE // 2
    pos = np.arange(DEC_SEQ)
    inv = ROPE_BASE ** (-np.arange(0, half, 2, dtype=np.float64) / half)
    ang = np.concatenate([(pos // GRID_W)[:, None] * inv, (pos % GRID_W)[:, None] * inv], axis=-1)
    ang = ang.astype(np.float32)
    cos = np.repeat(np.cos(ang), 2, axis=1)
    sin = np.repeat(np.sin(ang), 2, axis=1) * np.tile(np.array([-1.0, 1.0], np.float32), half)
    cs = np.concatenate([cos, sin], axis=1).astype(np.float32)
    ident = np.concatenate([np.ones((N_PROMPT, QK_ROPE), np.float32),
                            np.zeros((N_PROMPT, QK_ROPE), np.float32)], axis=1)
    return jnp.asarray(np.concatenate([ident, np.tile(cs, (DEC_BATCH, 1))], axis=0))


def kernel(x_prompt, x_sample, c, state_gla, cache_ckv, cache_krope, c_ctx, ada_w, ada_b, norm_mix, norm_ffn, norm_final, gla_w_in, gla_w_gate_up, gla_b_gate, gla_norm, gla_w_out, mla_w_down, mla_q_norm, mla_kv_norm, mla_w_uq, mla_w_ukv, mla_w_out, ffn_w_gate_up, ffn_w_down, moe_router, moe_w_gate_up, moe_w_down):
    x = jnp.concatenate([x_prompt.reshape(N_PROMPT, D_MODEL), x_sample.reshape(N_SAMPLE, D_MODEL)])
    cond = jnp.concatenate([c_ctx[None], c, jnp.zeros((N_COND - 1 - DEC_BATCH, D_MODEL), F32)])
    mods_all = _ada_mods(cond, ada_w, ada_b).reshape(DEPTH, N_COND, 6, 1, D_MODEL)
    cs = _rope_tables()
    swap = np.arange(QK_ROPE) ^ 1

    new_gla, new_ckv, new_kr = [], [], []
    for layer in range(DEPTH):
        j = layer // 2
        mods = mods_all[layer]
        h = _modulate(x, norm_mix[layer], mods, 0, 1)
        if layer % 2 == 0:
            w_in = gla_w_in[j].astype(BF16)
            proj = _matmul(h, w_in[:, :GLA_QKVR_W], BF16, tm=1024, tn=768)
            w_gd = jnp.pad(w_in[:, GLA_QKVR_W:], ((0, 0), (0, LANES - 2 * GATE_RANK)))
            gd = _matmul(h, w_gd, F32, tm=1024, tn=LANES)
            pad_lo = jnp.zeros((GATE_RANK, GLA_DK_TOT), F32)
            pad_hi = jnp.zeros((LANES - 2 * GATE_RANK, GLA_DK_TOT), F32)
            w0 = jnp.concatenate([gla_w_gate_up[j, 0], pad_lo, pad_hi])
            w1 = jnp.concatenate([pad_lo, gla_w_gate_up[j, 1], pad_hi])
            la = _gla_gates(gd, w0, w1, gla_b_gate[j])
            of_p, ob_p, st = _gla_scan(proj, la, 0, BATCH, SEQ, None)
            of_s, ob_s = _gla_scan(proj, la, N_PROMPT, DEC_BATCH, DEC_SEQ, state_gla[:, j])
            new_gla.append(st)
            mixed = _gla_post(of_p, of_s, ob_p, ob_s, proj,
                              jnp.tile(gla_norm[j], GLA_HEADS).reshape(1, GLA_DV_TOT))
            x = _matmul_residual(mixed, gla_w_out[j].astype(BF16), x, mods, 2)
        else:
            wd = mla_w_down[j]
            w_down = jnp.concatenate([wd, wd[:, Q_LORA + KV_LORA:][:, swap]], axis=1).astype(BF16)
            cq, ckv, kr, krp = _mla_down(h, w_down, mla_q_norm[j], mla_kv_norm[j], cs)
            wq = mla_w_uq[j].reshape(Q_LORA, MLA_HEADS, QK_NOPE + QK_ROPE)
            w_uq = jnp.concatenate([wq, wq[:, :, QK_NOPE:][:, :, swap]], axis=2)
            q = _mla_q(cq, w_uq.reshape(Q_LORA, -1).astype(BF16), cs)
            w_ukv = mla_w_ukv[j].astype(BF16)
            ckv_b = ckv.astype(BF16)
            kv_p = _matmul(ckv_b[:N_PROMPT], w_ukv, BF16, tm=1024, tn=1024)
            o_p = _attention(q, kv_p, krp[:N_PROMPT], 0, BATCH, SEQ, SEQ, tq=SEQ, heads=MLA_HEADS)
            n_keys = PAST_LEN + DEC_SEQ
            ckv_keys = jnp.concatenate(
                [cache_ckv[:, j].astype(BF16), ckv_b[N_PROMPT:].reshape(DEC_BATCH, DEC_SEQ, KV_LORA)],
                axis=1).reshape(DEC_BATCH * n_keys, KV_LORA)
            kr_ctx = jnp.pad(cache_krope[:, j], ((0, 0), (0, 0), (0, LANES - QK_ROPE))).astype(BF16)
            krp_keys = jnp.concatenate(
                [kr_ctx, krp[N_PROMPT:].reshape(DEC_BATCH, DEC_SEQ, LANES)],
                axis=1).reshape(DEC_BATCH * n_keys, LANES)
            kv_s = _matmul(ckv_keys, w_ukv, BF16, tm=1024, tn=1024)
            o_s = _attention(q, kv_s, krp_keys, N_PROMPT, DEC_BATCH, DEC_SEQ, n_keys, tq=1024, heads=1)
            new_ckv.append(ckv[:N_PROMPT].reshape(BATCH, SEQ, KV_LORA))
            new_kr.append(kr[:N_PROMPT, :QK_ROPE].reshape(BATCH, SEQ, QK_ROPE))
            x = _matmul_residual_split(o_p, o_s, mla_w_out[j].astype(BF16), x, mods, 2)

        if layer % 2 == 0:
            h = _modulate(x, norm_ffn[layer], mods, 3, 4)
            act = _swiglu_up(h, ffn_w_gate_up[j].astype(BF16), tm=1024, tn=1408)
            x = _matmul_residual(act, ffn_w_down[j].astype(BF16), x, mods, 5)
        else:
            router_pad = jnp.pad(moe_router[j], ((0, 0), (0, LANES - N_EXPERTS)))
            h, gates, slots, cnt = _modulate_router(x, norm_ffn[layer], mods, 3, 4, router_pad)
            counts = cnt[:, 0, :N_EXPERTS].astype(jnp.int32).reshape(-1)
            y = None
            for e in range(N_EXPERTS):
                y = _moe_expert(h, gates, slots, counts, moe_w_gate_up, moe_w_down, y, j, e)
            x = _gated_add(x, y, mods, 5)

    y = _final_norm(x, norm_final)
    return (y[:N_PROMPT].reshape(BATCH, SEQ, D_MODEL),
            y[N_PROMPT:].reshape(DEC_BATCH, DEC_SEQ, D_MODEL),
            jnp.stack(new_gla, axis=1),
            jnp.stack(new_ckv, axis=1),
            jnp.stack(new_kr, axis=1))
```

```python
import functools

import jax
import jax.numpy as jnp
import numpy as np
from jax import lax
from jax.experimental import pallas as pl
from jax.experimental.pallas import tpu as pltpu

D_MODEL = 1024
BATCH = 16
SEQ = 256
DEPTH = 4
DEC_BATCH = 4
DEC_SEQ = 2048
PAST_LEN = 512
GRID_W = 64
GLA_HEADS = 4
GLA_DK = 128
GLA_DV = 256
GLA_DK_TOT = GLA_HEADS * GLA_DK
GLA_DV_TOT = GLA_HEADS * GLA_DV
GATE_RANK = 16
GATE_TEMP = 16.0
MLA_HEADS = 8
Q_LORA = 384
KV_LORA = 256
QK_NOPE = 128
QK_ROPE = 64
V_HEAD = 128
MLA_SCALE = (QK_NOPE + QK_ROPE) ** -0.5
ROPE_BASE = 10000.0
D_FF = 2816
N_EXPERTS = 8
D_EXPERT = 3584
EPS = 1e-6

N_PROMPT = BATCH * SEQ
N_SAMPLE = DEC_BATCH * DEC_SEQ
N_TOK = N_PROMPT + N_SAMPLE
N_COND = 8
LANES = 128
GLA_QKVR_W = 2 * GLA_DK_TOT + 2 * GLA_DV_TOT
GLA_CHUNK = 256
GLA_MAX_FACTORISED_DECAY = 80.0
ATTN_KEY_CHUNK = 512
MOE_BLOCK = 1024
MOE_TILE = 256
MOE_W_CHUNKS = 28
LOG2E = 1.4426950408889634
VMEM_LIMIT = 56 * 1024 * 1024

F32 = jnp.float32
BF16 = jnp.bfloat16
HIGHEST = lax.Precision.HIGHEST
NT_DIMS = (((1,), (1,)), ((), ()))
TN_DIMS = (((0,), (0,)), ((), ()))


def _params(*sem):
    return pltpu.CompilerParams(dimension_semantics=sem, vmem_limit_bytes=VMEM_LIMIT)


def _cond_row(i, tm):
    return jnp.where(i * tm < N_PROMPT, 0, 1 + jnp.maximum(i * tm - N_PROMPT, 0) // DEC_SEQ)


def _mod_spec(which, tm, tn=D_MODEL, col=None):
    if col is None:
        return pl.BlockSpec((None, None, 1, tn), lambda i, *_: (_cond_row(i, tm), which, 0, 0))
    return pl.BlockSpec((None, None, 1, tn), lambda i, j: (_cond_row(i, tm), which, 0, j))


def _silu(x):
    return x / (1.0 + jnp.exp(-x))


def _rms(x):
    return x * lax.rsqrt(jnp.mean(x * x, axis=-1, keepdims=True) + EPS)


def _ada_kernel(c_ref, w_ref, b_ref, o_ref):
    o_ref[...] = jnp.dot(_silu(c_ref[...]), w_ref[...], precision=HIGHEST,
                         preferred_element_type=F32) + b_ref[...]


def _ada_mods(cond, ada_w, ada_b):
    tn = 1536
    return pl.pallas_call(
        _ada_kernel,
        grid=(DEPTH, 6 * D_MODEL // tn),
        in_specs=[pl.BlockSpec((N_COND, D_MODEL), lambda l, j: (0, 0)),
                  pl.BlockSpec((None, D_MODEL, tn), lambda l, j: (l, 0, j)),
                  pl.BlockSpec((None, 1, tn), lambda l, j: (l, 0, j))],
        out_specs=pl.BlockSpec((None, N_COND, tn), lambda l, j: (l, 0, j)),
        out_shape=jax.ShapeDtypeStruct((DEPTH, N_COND, 6 * D_MODEL), F32),
        compiler_params=_params("arbitrary", "arbitrary"),
        name="ada_mods",
    )(cond, ada_w, ada_b.reshape(DEPTH, 1, 6 * D_MODEL))


def _modulate_kernel(x_ref, g_ref, sh_ref, sc_ref, o_ref):
    y = _rms(x_ref[...]) * g_ref[...]
    o_ref[...] = (y * (1.0 + sc_ref[...]) + sh_ref[...]).astype(o_ref.dtype)


def _modulate(x, g, mods, shift_idx, scale_idx, tm=1024):
    return pl.pallas_call(
        _modulate_kernel,
        grid=(N_TOK // tm,),
        in_specs=[pl.BlockSpec((tm, D_MODEL), lambda i: (i, 0)),
                  pl.BlockSpec((1, D_MODEL), lambda i: (0, 0)),
                  _mod_spec(shift_idx, tm), _mod_spec(scale_idx, tm)],
        out_specs=pl.BlockSpec((tm, D_MODEL), lambda i: (i, 0)),
        out_shape=jax.ShapeDtypeStruct((N_TOK, D_MODEL), BF16),
        compiler_params=_params("arbitrary"),
        name="modulate",
    )(x, g.reshape(1, D_MODEL), mods, mods)


def _modulate_router_kernel(x_ref, g_ref, sh_ref, sc_ref, r_ref, o_ref, gates_ref, slot_ref, cnt_ref):
    y = _rms(x_ref[...]) * g_ref[...]
    h = y * (1.0 + sc_ref[...]) + sh_ref[...]
    o_ref[...] = h.astype(o_ref.dtype)
    logits = _dot_split(h, r_ref[...])
    lane = lax.broadcasted_iota(jnp.int32, logits.shape, 1)
    l1 = jnp.where(lane < N_EXPERTS, logits, -jnp.inf)
    m1 = jnp.max(l1, axis=1, keepdims=True)
    i1 = jnp.min(jnp.where(l1 == m1, lane, LANES), axis=1, keepdims=True)
    l2 = jnp.where(lane == i1, -jnp.inf, l1)
    m2 = jnp.max(l2, axis=1, keepdims=True)
    i2 = jnp.min(jnp.where(l2 == m2, lane, LANES), axis=1, keepdims=True)
    e = jnp.exp(m2 - m1)
    w1 = 1.0 / (1.0 + e)
    w2 = e / (1.0 + e)
    gates_ref[...] = jnp.where(lane == i1, w1, 0.0) + jnp.where(lane == i2, w2, 0.0)
    sel = (lane == i1) | (lane == i2)
    tb = logits.shape[0]
    earlier = (lax.broadcasted_iota(jnp.int32, (tb, tb), 1)
               < lax.broadcasted_iota(jnp.int32, (tb, tb), 0)).astype(BF16)
    before = jnp.dot(earlier, sel.astype(BF16), preferred_element_type=F32)
    slot_ref[...] = jnp.where(sel, before + 1.0, 0.0)
    cnt_ref[...] = jnp.broadcast_to(jnp.sum(sel.astype(F32), axis=0, keepdims=True), cnt_ref.shape)


def _modulate_router(x, g, mods, shift_idx, scale_idx, router_pad):
    tm = MOE_BLOCK
    nb = N_TOK // tm
    return pl.pallas_call(
        _modulate_router_kernel,
        grid=(nb,),
        in_specs=[pl.BlockSpec((tm, D_MODEL), lambda i: (i, 0)),
                  pl.BlockSpec((1, D_MODEL), lambda i: (0, 0)),
                  _mod_spec(shift_idx, tm), _mod_spec(scale_idx, tm),
                  pl.BlockSpec((D_MODEL, LANES), lambda i: (0, 0))],
        out_specs=[pl.BlockSpec((tm, D_MODEL), lambda i: (i, 0)),
                   pl.BlockSpec((tm, LANES), lambda i: (i, 0)),
                   pl.BlockSpec((tm, LANES), lambda i: (i, 0)),
                   pl.BlockSpec((None, 8, LANES), lambda i: (i, 0, 0))],
        out_shape=[jax.ShapeDtypeStruct((N_TOK, D_MODEL), BF16),
                   jax.ShapeDtypeStruct((N_TOK, LANES), F32),
                   jax.ShapeDtypeStruct((N_TOK, LANES), F32),
                   jax.ShapeDtypeStruct((nb, 8, LANES), F32)],
        compiler_params=_params("arbitrary"),
        name="modulate_router",
    )(x, g.reshape(1, D_MODEL), mods, mods, router_pad)


def _moe_expert_kernel(cnt_ref, h_ref, gates_ref, slot_ref, wgu_ref, wd_ref, *rest, expert, first):
    wgu_s, wd_s = rest[-2:]
    y_ref = rest[-3]
    step = pl.program_id(0)
    half = MOE_W_CHUNKS // 2

    wc = wgu_ref.shape[1]
    for c in range(MOE_W_CHUNKS):
        @pl.when(step == c)
        def _(c=c):
            wgu_s[:, c * wc:(c + 1) * wc] = wgu_ref[...].astype(BF16)

    @pl.when(step < half)
    def _():
        wd_s[pl.ds(pl.multiple_of(step * wc, wc), wc), :] = wd_ref[...].astype(BF16)

    @pl.when(step >= MOE_W_CHUNKS)
    def _():
        if first:
            y_ref[...] = jnp.zeros(y_ref.shape, F32)
        else:
            y_ref[...] = rest[0][...]
        tb = h_ref.shape[0]
        gate_col = gates_ref[:, expert:expert + 1]
        slot_col = slot_ref[:, expert:expert + 1]
        count = cnt_ref[(step - MOE_W_CHUNKS) * N_EXPERTS + expert]
        cw = 256

        def tile(first_slot, rows):
            lane = lax.broadcasted_iota(jnp.int32, (tb, rows), 1).astype(F32)
            onehot = (slot_col == lane + (first_slot + 1).astype(F32)).astype(BF16)
            hg = lax.dot_general(onehot, h_ref[...], TN_DIMS,
                                 preferred_element_type=F32).astype(BF16)
            gu = jnp.dot(hg, wgu_s[...], preferred_element_type=F32)
            act = (_silu(gu[:, :D_EXPERT]) * gu[:, D_EXPERT:]).astype(BF16)
            out = jnp.dot(act, wd_s[...], preferred_element_type=F32).astype(BF16)
            for c0 in range(0, D_MODEL, cw):
                y_ref[:, c0:c0 + cw] += gate_col * jnp.dot(onehot, out[:, c0:c0 + cw],
                                                           preferred_element_type=F32)

        full = count // MOE_TILE
        left = count - full * MOE_TILE
        n_big = full + (left > MOE_TILE // 2).astype(jnp.int32)

        def big_tile(k, carry):
            tile(k * MOE_TILE, MOE_TILE)
            return carry

        lax.fori_loop(0, n_big, big_tile, 0)

        @pl.when(jnp.logical_and(left > 0, left <= MOE_TILE // 2))
        def _():
            tile(full * MOE_TILE, MOE_TILE // 2)


def _moe_expert(h, gates, slots, counts, w_gu, w_down, y, layer, expert):
    tb = MOE_BLOCK
    nc = MOE_W_CHUNKS
    wc = 2 * D_EXPERT // nc
    first = y is None
    blk = lambda s, cnt: (jnp.maximum(s - nc, 0), 0)
    in_specs = [pl.BlockSpec((tb, D_MODEL), blk), pl.BlockSpec((tb, LANES), blk),
                pl.BlockSpec((tb, LANES), blk),
                pl.BlockSpec((None, None, D_MODEL, wc),
                             lambda s, cnt: (layer, expert, 0, jnp.minimum(s, nc - 1))),
                pl.BlockSpec((None, None, wc, D_MODEL),
                             lambda s, cnt: (layer, expert, jnp.minimum(s, nc // 2 - 1), 0))]
    args = [h, gates, slots, w_gu, w_down]
    if not first:
        in_specs.append(pl.BlockSpec((tb, D_MODEL), blk))
        args.append(y)
    return pl.pallas_call(
        functools.partial(_moe_expert_kernel, expert=expert, first=first),
        grid_spec=pltpu.PrefetchScalarGridSpec(
            num_scalar_prefetch=1, grid=(nc + N_TOK // tb,), in_specs=in_specs,
            out_specs=pl.BlockSpec((tb, D_MODEL), blk),
            scratch_shapes=[pltpu.VMEM((D_MODEL, 2 * D_EXPERT), BF16),
                            pltpu.VMEM((D_EXPERT, D_MODEL), BF16)]),
        out_shape=jax.ShapeDtypeStruct((N_TOK, D_MODEL), F32),
        compiler_params=_params("arbitrary"),
        name="moe_expert",
    )(counts, *args)


def _mm_kernel(a_ref, w_ref, o_ref):
    o_ref[...] = jnp.dot(a_ref[...], w_ref[...], preferred_element_type=F32).astype(o_ref.dtype)


def _matmul(a, w, out_dtype, tm, tn):
    m, k = a.shape
    n = w.shape[1]
    return pl.pallas_call(
        _mm_kernel,
        grid=(m // tm, n // tn),
        in_specs=[pl.BlockSpec((tm, k), lambda i, j: (i, 0)),
                  pl.BlockSpec((k, tn), lambda i, j: (0, j))],
        out_specs=pl.BlockSpec((tm, tn), lambda i, j: (i, j)),
        out_shape=jax.ShapeDtypeStruct((m, n), out_dtype),
        compiler_params=_params("arbitrary", "arbitrary"),
        name="matmul",
    )(a, w)


def _swiglu_kernel(a_ref, wg_ref, wu_ref, o_ref):
    a = a_ref[...]
    g = jnp.dot(a, wg_ref[...], preferred_element_type=F32)
    u = jnp.dot(a, wu_ref[...], preferred_element_type=F32)
    o_ref[...] = (_silu(g) * u).astype(o_ref.dtype)


def _swiglu_up(a, w_gu, tm, tn):
    m, k = a.shape
    f = w_gu.shape[1] // 2
    nj = f // tn
    return pl.pallas_call(
        _swiglu_kernel,
        grid=(m // tm, nj),
        in_specs=[pl.BlockSpec((tm, k), lambda i, j: (i, 0)),
                  pl.BlockSpec((k, tn), lambda i, j: (0, j)),
                  pl.BlockSpec((k, tn), lambda i, j: (0, j + nj))],
        out_specs=pl.BlockSpec((tm, tn), lambda i, j: (i, j)),
        out_shape=jax.ShapeDtypeStruct((m, f), BF16),
        compiler_params=_params("arbitrary", "arbitrary"),
        name="swiglu_up",
    )(a, w_gu, w_gu)


def _next_norm_specs(next_norm, tm):
    gain, mods, shift_idx, scale_idx = next_norm
    specs = [pl.BlockSpec((1, D_MODEL), lambda i, *_: (0, 0)),
             _mod_spec(shift_idx, tm), _mod_spec(scale_idx, tm)]
    return specs, [gain.reshape(1, D_MODEL), mods, mods]


def _mm_residual_norm_kernel(a_ref, w_ref, x_ref, g_ref, ng_ref, nsh_ref, nsc_ref, o_ref, h_ref):
    x = x_ref[...] + g_ref[...] * jnp.dot(a_ref[...], w_ref[...], preferred_element_type=F32)
    o_ref[...] = x
    h_ref[...] = (_rms(x) * ng_ref[...] * (1.0 + nsc_ref[...]) + nsh_ref[...]).astype(h_ref.dtype)


def _matmul_residual(a, w, x, mods, gate_idx, next_norm, tm=512):
    k = a.shape[1]
    norm_specs, norm_args = _next_norm_specs(next_norm, tm)
    row = lambda i: (i, 0)
    return pl.pallas_call(
        _mm_residual_norm_kernel,
        grid=(N_TOK // tm,),
        in_specs=[pl.BlockSpec((tm, k), row),
                  pl.BlockSpec((k, D_MODEL), lambda i: (0, 0)),
                  pl.BlockSpec((tm, D_MODEL), row),
                  _mod_spec(gate_idx, tm)] + norm_specs,
        out_specs=[pl.BlockSpec((tm, D_MODEL), row), pl.BlockSpec((tm, D_MODEL), row)],
        out_shape=[jax.ShapeDtypeStruct((N_TOK, D_MODEL), F32),
                   jax.ShapeDtypeStruct((N_TOK, D_MODEL), BF16)],
        compiler_params=_params("arbitrary"),
        name="matmul_residual",
    )(a, w, x, mods, *norm_args)


def _mm_residual_split_kernel(ap_ref, as_ref, w_ref, x_ref, g_ref, o_ref):
    is_prompt = pl.program_id(0) < N_PROMPT // o_ref.shape[0]
    a = jnp.where(is_prompt, ap_ref[...], as_ref[...])
    o_ref[...] = x_ref[...] + g_ref[...] * jnp.dot(a, w_ref[...], preferred_element_type=F32)


def _matmul_residual_split(a_prompt, a_sample, w, x, mods, gate_idx, tm=1024, tn=512):
    k = w.shape[0]
    return pl.pallas_call(
        _mm_residual_split_kernel,
        grid=(N_TOK // tm, D_MODEL // tn),
        in_specs=(_prompt_sample_specs(tm, k)
                  + [pl.BlockSpec((k, tn), lambda i, j: (0, j)),
                     pl.BlockSpec((tm, tn), lambda i, j: (i, j)),
                     _mod_spec(gate_idx, tm, tn, col=True)]),
        out_specs=pl.BlockSpec((tm, tn), lambda i, j: (i, j)),
        out_shape=jax.ShapeDtypeStruct((N_TOK, D_MODEL), F32),
        compiler_params=_params("arbitrary", "arbitrary"),
        name="matmul_residual_split",
    )(a_prompt, a_sample, w, x, mods)


def _gated_add_norm_kernel(x_ref, y_ref, g_ref, ng_ref, nsh_ref, nsc_ref, o_ref, h_ref):
    x = x_ref[...] + g_ref[...] * y_ref[...]
    o_ref[...] = x
    h_ref[...] = (_rms(x) * ng_ref[...] * (1.0 + nsc_ref[...]) + nsh_ref[...]).astype(h_ref.dtype)


def _gated_add(x, y, mods, gate_idx, next_norm, tm=1024):
    norm_specs, norm_args = _next_norm_specs(next_norm, tm)
    row = lambda i: (i, 0)
    return pl.pallas_call(
        _gated_add_norm_kernel,
        grid=(N_TOK // tm,),
        in_specs=[pl.BlockSpec((tm, D_MODEL), row), pl.BlockSpec((tm, D_MODEL), row),
                  _mod_spec(gate_idx, tm)] + norm_specs,
        out_specs=[pl.BlockSpec((tm, D_MODEL), row), pl.BlockSpec((tm, D_MODEL), row)],
        out_shape=[jax.ShapeDtypeStruct((N_TOK, D_MODEL), F32),
                   jax.ShapeDtypeStruct((N_TOK, D_MODEL), BF16)],
        compiler_params=_params("arbitrary"),
        name="gated_add",
    )(x, y, mods, *norm_args)


def _gated_add_final_kernel(x_ref, y_ref, g_ref, ng_ref, o_ref):
    o_ref[...] = _rms(x_ref[...] + g_ref[...] * y_ref[...]) * ng_ref[...]


def _gated_add_final(x, y, mods, gate_idx, gain, tm=1024):
    row = lambda i: (i, 0)
    return pl.pallas_call(
        _gated_add_final_kernel,
        grid=(N_TOK // tm,),
        in_specs=[pl.BlockSpec((tm, D_MODEL), row), pl.BlockSpec((tm, D_MODEL), row),
                  _mod_spec(gate_idx, tm), pl.BlockSpec((1, D_MODEL), lambda i: (0, 0))],
        out_specs=pl.BlockSpec((tm, D_MODEL), row),
        out_shape=jax.ShapeDtypeStruct((N_TOK, D_MODEL), F32),
        compiler_params=_params("arbitrary"),
        name="gated_add_final",
    )(x, y, mods, gain.reshape(1, D_MODEL))


def _log_sigmoid(x):
    return jnp.minimum(x, 0.0) - jnp.log(1.0 + jnp.exp(-jnp.abs(x)))


def _split_bf16(x):
    hi = x.astype(BF16)
    return hi, (x - hi.astype(F32)).astype(BF16)


def _dot_split(a, b):
    a_hi, a_lo = _split_bf16(a)
    b_hi, b_lo = _split_bf16(b)
    return (jnp.dot(a_hi, b_hi, preferred_element_type=F32)
            + jnp.dot(a_lo, b_hi, preferred_element_type=F32)
            + jnp.dot(a_hi, b_lo, preferred_element_type=F32))


def _gla_gate_kernel(h_ref, wgd_ref, w0_ref, w1_ref, b_ref, o_ref):
    gd = jnp.dot(h_ref[...], wgd_ref[...], preferred_element_type=F32)
    x0 = _dot_split(gd, w0_ref[...]) + b_ref[0:1, :]
    x1 = _dot_split(gd, w1_ref[...]) + b_ref[1:2, :]
    o_ref[:, :GLA_DK_TOT] = _log_sigmoid(x0) * (1.0 / GATE_TEMP)
    o_ref[:, GLA_DK_TOT:] = _log_sigmoid(x1) * (1.0 / GATE_TEMP)


def _gla_gates(h, w_gd, w0, w1, b_gate, tm=1024):
    return pl.pallas_call(
        _gla_gate_kernel,
        grid=(N_TOK // tm,),
        in_specs=[pl.BlockSpec((tm, D_MODEL), lambda i: (i, 0)),
                  pl.BlockSpec((D_MODEL, LANES), lambda i: (0, 0)),
                  pl.BlockSpec((LANES, GLA_DK_TOT), lambda i: (0, 0)),
                  pl.BlockSpec((LANES, GLA_DK_TOT), lambda i: (0, 0)),
                  pl.BlockSpec((2, GLA_DK_TOT), lambda i: (0, 0))],
        out_specs=pl.BlockSpec((tm, 2 * GLA_DK_TOT), lambda i: (i, 0)),
        out_shape=jax.ShapeDtypeStruct((N_TOK, 2 * GLA_DK_TOT), F32),
        compiler_params=_params("arbitrary"),
        name="gla_gates",
    )(h, w_gd, w0, w1, b_gate)


def _gla_keep_mask(causal):
    c = GLA_CHUNK
    row = lax.broadcasted_iota(jnp.int32, (c, c), 0)
    col = lax.broadcasted_iota(jnp.int32, (c, c), 1)
    return (col <= row) if causal else (col >= row)


def _gla_cum_decay(la_ref, causal):
    mask = _gla_keep_mask(causal).astype(BF16)
    la = la_ref[...]
    hi = la.astype(BF16)
    lo = (la - hi.astype(F32)).astype(BF16)
    return (jnp.dot(mask, hi, preferred_element_type=F32)
            + jnp.dot(mask, lo, preferred_element_type=F32))


def _gla_direction(q_ref, k_ref, v_ref, b, o_ref, s_ref, b_ref, d, causal, factorised):
    c = GLA_CHUNK
    keep = _gla_keep_mask(causal)
    end = c - 1 if causal else 0
    b_end = b[end:end + 1, :]
    q_dec = q_ref[...].astype(F32) * (GLA_DK ** -0.5) * jnp.exp(b)
    k_end = k_ref[...] * jnp.exp(b_end - b)
    e_end = jnp.exp(b_end)
    eye = (lax.broadcasted_iota(jnp.int32, (GLA_DK, GLA_DK), 0)
           == lax.broadcasted_iota(jnp.int32, (GLA_DK, GLA_DK), 1))
    if factorised:
        k_inv = k_ref[...] * jnp.exp(-b)
    else:
        b_ref[...] = b
        col = lax.broadcasted_iota(jnp.int32, (c, c), 1)
    for h in range(GLA_HEADS):
        ks = slice(h * GLA_DK, (h + 1) * GLA_DK)
        vs = slice(h * GLA_DV, (h + 1) * GLA_DV)
        state = s_ref[d, h]
        qh = q_dec[:, ks].astype(BF16)
        vh = v_ref[:, vs].astype(BF16)
        if factorised:
            scores = lax.dot_general(qh, k_inv[:, ks].astype(BF16), NT_DIMS,
                                     preferred_element_type=F32)
        else:
            def key_column(s, sc, ks=ks):
                base = pl.multiple_of((s // 16) * 16, 16)
                pick = lax.broadcasted_iota(jnp.int32, (16, GLA_DK), 0) == s % 16
                b_s = jnp.sum(jnp.where(pick, b_ref[pl.ds(base, 16), ks], 0.0),
                              axis=0, keepdims=True)
                k_s = jnp.sum(jnp.where(pick, k_ref[pl.ds(base, 16), ks].astype(F32), 0.0),
                              axis=0, keepdims=True)
                decay = jnp.exp(jnp.minimum(b_ref[:, ks] - b_s, 0.0))
                column = jnp.sum(q_ref[:, ks].astype(F32) * (GLA_DK ** -0.5) * k_s * decay,
                                 axis=1, keepdims=True)
                return jnp.where(col == s, column, sc)

            scores = lax.fori_loop(0, c, key_column, jnp.zeros((c, c), F32))
        scores = jnp.where(keep, scores, 0.0)
        o_ref[:, vs] = (jnp.dot(scores.astype(BF16), vh, preferred_element_type=F32)
                        + jnp.dot(qh, state.astype(BF16), preferred_element_type=F32))
        e_col = jnp.sum(jnp.where(eye, jnp.broadcast_to(e_end[:, ks], (GLA_DK, GLA_DK)), 0.0),
                        axis=1, keepdims=True)
        s_ref[d, h] = state * e_col + lax.dot_general(
            k_end[:, ks].astype(BF16), vh, TN_DIMS, preferred_element_type=F32)


def _gla_scan_kernel(*refs, n_chunks, has_s0, write_state):
    qf, kf, vf, laf, qb, kb, vb, lab = refs[:8]
    rest = list(refs[8:])
    s0_ref = rest.pop(0) if has_s0 else None
    of_ref, ob_ref = rest.pop(0), rest.pop(0)
    sout_ref = rest.pop(0) if write_state else None
    s_ref, b_ref = rest
    i = pl.program_id(1)

    @pl.when(i == 0)
    def _():
        s_ref[...] = s0_ref[...] if has_s0 else jnp.zeros(s_ref.shape, F32)

    b_f = _gla_cum_decay(laf, True)
    b_b = _gla_cum_decay(lab, False)
    c = GLA_CHUNK
    total_decay = jnp.minimum(jnp.min(b_f[c - 1:c, :]), jnp.min(b_b[0:1, :]))
    factorisable = total_decay >= -GLA_MAX_FACTORISED_DECAY

    def step(factorised):
        _gla_direction(qf, kf, vf, b_f, of_ref, s_ref, b_ref.at[0], 0, True, factorised)
        _gla_direction(qb, kb, vb, b_b, ob_ref, s_ref, b_ref.at[1], 1, False, factorised)

    pl.when(factorisable)(functools.partial(step, True))
    pl.when(jnp.logical_not(factorisable))(functools.partial(step, False))

    if write_state:
        @pl.when(i == n_chunks - 1)
        def _():
            sout_ref[...] = s_ref[...]


def _gla_scan(proj, la, row0, n_batch, seq, s0):
    c = GLA_CHUNK
    n = seq // c
    base = row0 // c
    has_s0 = s0 is not None
    write_state = not has_s0
    fwd = lambda b, i: base + b * n + i
    bwd = lambda b, i: base + b * n + (n - 1 - i)

    def specs(blk):
        return [pl.BlockSpec((c, GLA_DK_TOT), lambda b, i: (blk(b, i), 0)),
                pl.BlockSpec((c, GLA_DK_TOT), lambda b, i: (blk(b, i), 1)),
                pl.BlockSpec((c, GLA_DV_TOT), lambda b, i: (blk(b, i), 1))]
    state_block = (None, 2, GLA_HEADS, GLA_DK, GLA_DV)
    state_spec = pl.BlockSpec(state_block, lambda b, i: (b, 0, 0, 0, 0))
    in_specs = (specs(fwd) + [pl.BlockSpec((c, GLA_DK_TOT), lambda b, i: (fwd(b, i), 0))]
                + specs(bwd) + [pl.BlockSpec((c, GLA_DK_TOT), lambda b, i: (bwd(b, i), 1))])
    args = [proj, proj, proj, la, proj, proj, proj, la]
    if has_s0:
        in_specs.append(state_spec)
        args.append(s0)
    out_rows = n_batch * seq
    out_specs = [pl.BlockSpec((c, GLA_DV_TOT), lambda b, i: (b * n + i, 0)),
                 pl.BlockSpec((c, GLA_DV_TOT), lambda b, i: (b * n + (n - 1 - i), 0))]
    out_shape = [jax.ShapeDtypeStruct((out_rows, GLA_DV_TOT), F32)] * 2
    if write_state:
        out_specs.append(state_spec)
        out_shape.append(jax.ShapeDtypeStruct((n_batch, 2, GLA_HEADS, GLA_DK, GLA_DV), F32))
    return pl.pallas_call(
        functools.partial(_gla_scan_kernel, n_chunks=n, has_s0=has_s0, write_state=write_state),
        grid=(n_batch, n),
        in_specs=in_specs,
        out_specs=out_specs,
        out_shape=out_shape,
        scratch_shapes=[pltpu.VMEM((2, GLA_HEADS, GLA_DK, GLA_DV), F32),
                        pltpu.VMEM((2, c, GLA_DK_TOT), F32)],
        compiler_params=_params("arbitrary", "arbitrary"),
        name="gla_scan",
    )(*args)


def _prompt_sample_specs(tm, width):
    n_p = N_PROMPT // tm
    return [pl.BlockSpec((tm, width), lambda i, *_: (jnp.minimum(i, n_p - 1), 0)),
            pl.BlockSpec((tm, width), lambda i, *_: (jnp.maximum(i - n_p, 0), 0))]


def _gla_post_kernel(ofp_ref, ofs_ref, obp_ref, obs_ref, r_ref, g_ref, o_ref):
    is_prompt = pl.program_id(0) < N_PROMPT // o_ref.shape[0]
    o = jnp.where(is_prompt, ofp_ref[...] + obp_ref[...], ofs_ref[...] + obs_ref[...])
    normed = jnp.concatenate(
        [_rms(o[:, h * GLA_DV:(h + 1) * GLA_DV]) for h in range(GLA_HEADS)], axis=1)
    o_ref[...] = (normed * g_ref[...] * _silu(r_ref[...].astype(F32))).astype(o_ref.dtype)


def _gla_post(of_p, of_s, ob_p, ob_s, proj, g_tiled, tm=1024):
    return pl.pallas_call(
        _gla_post_kernel,
        grid=(N_TOK // tm,),
        in_specs=(_prompt_sample_specs(tm, GLA_DV_TOT) + _prompt_sample_specs(tm, GLA_DV_TOT)
                  + [pl.BlockSpec((tm, GLA_DV_TOT), lambda i: (i, 2)),
                     pl.BlockSpec((1, GLA_DV_TOT), lambda i: (0, 0))]),
        out_specs=pl.BlockSpec((tm, GLA_DV_TOT), lambda i: (i, 0)),
        out_shape=jax.ShapeDtypeStruct((N_TOK, GLA_DV_TOT), BF16),
        compiler_params=_params("arbitrary"),
        name="gla_post",
    )(of_p, of_s, ob_p, ob_s, proj, g_tiled)


def _rope_pair(u, cs):
    t = u * cs
    return t + pltpu.roll(t, QK_ROPE, axis=1)


def _mla_down_kernel(a_ref, w_ref, qn_ref, kvn_ref, cs_ref, cq_ref, ckv_ref, kr_ref, krp_ref):
    acc = jnp.dot(a_ref[...], w_ref[...], preferred_element_type=F32)
    cq_ref[...] = (_rms(acc[:, :Q_LORA]) * qn_ref[...]).astype(cq_ref.dtype)
    ckv_ref[...] = _rms(acc[:, Q_LORA:Q_LORA + KV_LORA]) * kvn_ref[...]
    u = acc[:, Q_LORA + KV_LORA:]
    kr_ref[...] = u
    lane = lax.broadcasted_iota(jnp.int32, u.shape, 1)
    krp_ref[...] = jnp.where(lane < QK_ROPE, _rope_pair(u, cs_ref[...]), 0.0).astype(krp_ref.dtype)


def _mla_down(h, w_ext, q_norm, kv_norm, cs, tm=512):
    wn = w_ext.shape[1]
    row = lambda i: (i, 0)
    fixed = lambda i: (0, 0)
    return pl.pallas_call(
        _mla_down_kernel,
        grid=(N_TOK // tm,),
        in_specs=[pl.BlockSpec((tm, D_MODEL), row), pl.BlockSpec((D_MODEL, wn), fixed),
                  pl.BlockSpec((1, Q_LORA), fixed), pl.BlockSpec((1, KV_LORA), fixed),
                  pl.BlockSpec((tm, LANES), row)],
        out_specs=[pl.BlockSpec((tm, Q_LORA), row), pl.BlockSpec((tm, KV_LORA), row),
                   pl.BlockSpec((tm, LANES), row), pl.BlockSpec((tm, LANES), row)],
        out_shape=[jax.ShapeDtypeStruct((N_TOK, Q_LORA), BF16),
                   jax.ShapeDtypeStruct((N_TOK, KV_LORA), F32),
                   jax.ShapeDtypeStruct((N_TOK, LANES), F32),
                   jax.ShapeDtypeStruct((N_TOK, LANES), BF16)],
        compiler_params=_params("arbitrary"),
        name="mla_down",
    )(h, w_ext, q_norm.reshape(1, Q_LORA), kv_norm.reshape(1, KV_LORA), cs)


def _mla_q_kernel(a_ref, w_ref, cs_ref, o_ref):
    acc = jnp.dot(a_ref[...], w_ref[...], preferred_element_type=F32)
    cs = cs_ref[...]
    for h in range(MLA_HEADS):
        lo = h * 2 * LANES
        o_ref[:, lo:lo + LANES] = acc[:, lo:lo + LANES].astype(o_ref.dtype)
        o_ref[:, lo + LANES:lo + 2 * LANES] = _rope_pair(
            acc[:, lo + LANES:lo + 2 * LANES], cs).astype(o_ref.dtype)


def _mla_q(cq, w_ext, cs, tm=512):
    wn = w_ext.shape[1]
    return pl.pallas_call(
        _mla_q_kernel,
        grid=(N_TOK // tm,),
        in_specs=[pl.BlockSpec((tm, Q_LORA), lambda i: (i, 0)),
                  pl.BlockSpec((Q_LORA, wn), lambda i: (0, 0)),
                  pl.BlockSpec((tm, LANES), lambda i: (i, 0))],
        out_specs=pl.BlockSpec((tm, wn), lambda i: (i, 0)),
        out_shape=jax.ShapeDtypeStruct((N_TOK, wn), BF16),
        compiler_params=_params("arbitrary"),
        name="mla_q",
    )(cq, w_ext, cs)


def _attn_kernel(q_ref, kv_ref, kr_ref, o_ref, *, key_chunk, heads):
    tq = q_ref.shape[0]
    c = MLA_SCALE * LOG2E
    hw = QK_NOPE + V_HEAD
    for h in range(heads):
        q = q_ref[:, h * hw:(h + 1) * hw]
        m = jnp.full((tq, 1), -jnp.inf, F32)
        l = jnp.zeros((tq, 1), F32)
        acc = jnp.zeros((tq, V_HEAD), F32)
        for c0 in range(0, kv_ref.shape[0], key_chunk):
            rows = slice(c0, c0 + key_chunk)
            k = jnp.concatenate([kv_ref[rows, h * hw:h * hw + QK_NOPE], kr_ref[rows, :]], axis=1)
            s = lax.dot_general(q, k, NT_DIMS, preferred_element_type=F32)
            m_new = jnp.maximum(m, jnp.max(s, axis=1, keepdims=True))
            alpha = jnp.exp2((m - m_new) * c)
            p = jnp.exp2((s - m_new) * c)
            l = alpha * l + jnp.sum(p, axis=1, keepdims=True)
            acc = alpha * acc + jnp.dot(p.astype(BF16), kv_ref[rows, h * hw + QK_NOPE:(h + 1) * hw],
                                        preferred_element_type=F32)
            m = m_new
        o_ref[:, h * V_HEAD:(h + 1) * V_HEAD] = (acc / l).astype(o_ref.dtype)


def _attention(q, kv, krp, row0, n_batch, seq, n_keys, tq, heads):
    nq = seq // tq
    base = row0 // tq
    hw = heads * (QK_NOPE + V_HEAD)
    return pl.pallas_call(
        functools.partial(_attn_kernel, key_chunk=min(n_keys, ATTN_KEY_CHUNK), heads=heads),
        grid=(n_batch, MLA_HEADS // heads, nq),
        in_specs=[pl.BlockSpec((tq, hw), lambda b, h, i: (base + b * nq + i, h)),
                  pl.BlockSpec((n_keys, hw), lambda b, h, i: (b, h)),
                  pl.BlockSpec((n_keys, LANES), lambda b, h, i: (b, 0))],
        out_specs=pl.BlockSpec((tq, heads * V_HEAD), lambda b, h, i: (b * nq + i, h)),
        out_shape=jax.ShapeDtypeStruct((n_batch * seq, MLA_HEADS * V_HEAD), BF16),
        compiler_params=_params("arbitrary", "arbitrary", "arbitrary"),
        name="mla_attention",
    )(q, kv, krp)


def _rope_tables():
    half = QK_ROPE // 2
    pos = np.arange(DEC_SEQ)
    inv = ROPE_BASE ** (-np.arange(0, half, 2, dtype=np.float64) / half)
    ang = np.concatenate([(pos // GRID_W)[:, None] * inv, (pos % GRID_W)[:, None] * inv], axis=-1)
    ang = ang.astype(np.float32)
    cos = np.repeat(np.cos(ang), 2, axis=1)
    sin = np.repeat(np.sin(ang), 2, axis=1) * np.tile(np.array([-1.0, 1.0], np.float32), half)
    cs = np.concatenate([cos, sin], axis=1).astype(np.float32)
    ident = np.concatenate([np.ones((N_PROMPT, QK_ROPE), np.float32),
                            np.zeros((N_PROMPT, QK_ROPE), np.float32)], axis=1)
    return jnp.asarray(np.concatenate([ident, np.tile(cs, (DEC_BATCH, 1))], axis=0))


def kernel(x_prompt, x_sample, c, state_gla, cache_ckv, cache_krope, c_ctx, ada_w, ada_b, norm_mix, norm_ffn, norm_final, gla_w_in, gla_w_gate_up, gla_b_gate, gla_norm, gla_w_out, mla_w_down, mla_q_norm, mla_kv_norm, mla_w_uq, mla_w_ukv, mla_w_out, ffn_w_gate_up, ffn_w_down, moe_router, moe_w_gate_up, moe_w_down):
    x = jnp.concatenate([x_prompt.reshape(N_PROMPT, D_MODEL), x_sample.reshape(N_SAMPLE, D_MODEL)])
    cond = jnp.concatenate([c_ctx[None], c, jnp.zeros((N_COND - 1 - DEC_BATCH, D_MODEL), F32)])
    mods_all = _ada_mods(cond, ada_w, ada_b).reshape(DEPTH, N_COND, 6, 1, D_MODEL)
    cs = _rope_tables()
    swap = np.arange(QK_ROPE) ^ 1

    new_gla, new_ckv, new_kr = [], [], []
    h = _modulate(x, norm_mix[0], mods_all[0], 0, 1)
    for layer in range(DEPTH):
        j = layer // 2
        mods = mods_all[layer]
        if layer % 2 == 0:
            w_in = gla_w_in[j].astype(BF16)
            proj = _matmul(h, w_in[:, :GLA_QKVR_W], BF16, tm=1024, tn=768)
            w_gd = jnp.pad(w_in[:, GLA_QKVR_W:], ((0, 0), (0, LANES - 2 * GATE_RANK)))
            pad_lo = jnp.zeros((GATE_RANK, GLA_DK_TOT), F32)
            pad_hi = jnp.zeros((LANES - 2 * GATE_RANK, GLA_DK_TOT), F32)
            w0 = jnp.concatenate([gla_w_gate_up[j, 0], pad_lo, pad_hi])
            w1 = jnp.concatenate([pad_lo, gla_w_gate_up[j, 1], pad_hi])
            la = _gla_gates(h, w_gd, w0, w1, gla_b_gate[j])
            of_p, ob_p, st = _gla_scan(proj, la, 0, BATCH, SEQ, None)
            of_s, ob_s = _gla_scan(proj, la, N_PROMPT, DEC_BATCH, DEC_SEQ, state_gla[:, j])
            new_gla.append(st)
            mixed = _gla_post(of_p, of_s, ob_p, ob_s, proj,
                              jnp.tile(gla_norm[j], GLA_HEADS).reshape(1, GLA_DV_TOT))
            x, h = _matmul_residual(mixed, gla_w_out[j].astype(BF16), x, mods, 2,
                                    (norm_ffn[layer], mods, 3, 4))
            act = _swiglu_up(h, ffn_w_gate_up[j].astype(BF16), tm=1024, tn=1408)
            x, h = _matmul_residual(act, ffn_w_down[j].astype(BF16), x, mods, 5,
                                    (norm_mix[layer + 1], mods_all[layer + 1], 0, 1))
        else:
            wd = mla_w_down[j]
            w_down = jnp.concatenate([wd, wd[:, Q_LORA + KV_LORA:][:, swap]], axis=1).astype(BF16)
            cq, ckv, kr, krp = _mla_down(h, w_down, mla_q_norm[j], mla_kv_norm[j], cs)
            wq = mla_w_uq[j].reshape(Q_LORA, MLA_HEADS, QK_NOPE + QK_ROPE)
            w_uq = jnp.concatenate([wq, wq[:, :, QK_NOPE:][:, :, swap]], axis=2)
            q = _mla_q(cq, w_uq.reshape(Q_LORA, -1).astype(BF16), cs)
            w_ukv = mla_w_ukv[j].astype(BF16)
            ckv_b = ckv.astype(BF16)
            kv_p = _matmul(ckv_b[:N_PROMPT], w_ukv, BF16, tm=1024, tn=1024)
            o_p = _attention(q, kv_p, krp[:N_PROMPT], 0, BATCH, SEQ, SEQ, tq=SEQ, heads=MLA_HEADS)
            n_keys = PAST_LEN + DEC_SEQ
            ckv_keys = jnp.concatenate(
                [cache_ckv[:, j].astype(BF16), ckv_b[N_PROMPT:].reshape(DEC_BATCH, DEC_SEQ, KV_LORA)],
                axis=1).reshape(DEC_BATCH * n_keys, KV_LORA)
            kr_ctx = jnp.pad(cache_krope[:, j], ((0, 0), (0, 0), (0, LANES - QK_ROPE))).astype(BF16)
            krp_keys = jnp.concatenate(
                [kr_ctx, krp[N_PROMPT:].reshape(DEC_BATCH, DEC_SEQ, LANES)],
                axis=1).reshape(DEC_BATCH * n_keys, LANES)
            kv_s = _matmul(ckv_keys, w_ukv, BF16, tm=1024, tn=1024)
            o_s = _attention(q, kv_s, krp_keys, N_PROMPT, DEC_BATCH, DEC_SEQ, n_keys, tq=1024, heads=1)
            new_ckv.append(ckv[:N_PROMPT].reshape(BATCH, SEQ, KV_LORA))
            new_kr.append(kr[:N_PROMPT, :QK_ROPE].reshape(BATCH, SEQ, QK_ROPE))
            x = _matmul_residual_split(o_p, o_s, mla_w_out[j].astype(BF16), x, mods, 2)

            router_pad = jnp.pad(moe_router[j], ((0, 0), (0, LANES - N_EXPERTS)))
            h, gates, slots, cnt = _modulate_router(x, norm_ffn[layer], mods, 3, 4, router_pad)
            counts = cnt[:, 0, :N_EXPERTS].astype(jnp.int32).reshape(-1)
            y = None
            for e in range(N_EXPERTS):
                y = _moe_expert(h, gates, slots, counts, moe_w_gate_up, moe_w_down, y, j, e)
            if layer + 1 < DEPTH:
                x, h = _gated_add(x, y, mods, 5, (norm_mix[layer + 1], mods_all[layer + 1], 0, 1))
            else:
                out = _gated_add_final(x, y, mods, 5, norm_final)

    return (out[:N_PROMPT].reshape(BATCH, SEQ, D_MODEL),
            out[N_PROMPT:].reshape(DEC_BATCH, DEC_SEQ, D_MODEL),
            jnp.stack(new_gla, axis=1),
            jnp.stack(new_ckv, axis=1),
            jnp.stack(new_kr, axis=1))
```

```python
import functools

import jax
import jax.numpy as jnp
import numpy as np
from jax import lax
from jax.experimental import pallas as pl
from jax.experimental.pallas import tpu as pltpu

D_MODEL = 1024
BATCH = 16
SEQ = 256
DEPTH = 4
DEC_BATCH = 4
DEC_SEQ = 2048
PAST_LEN = 512
GRID_W = 64
GLA_HEADS = 4
GLA_DK = 128
GLA_DV = 256
GLA_DK_TOT = GLA_HEADS * GLA_DK
GLA_DV_TOT = GLA_HEADS * GLA_DV
GATE_RANK = 16
GATE_TEMP = 16.0
MLA_HEADS = 8
Q_LORA = 384
KV_LORA = 256
QK_NOPE = 128
QK_ROPE = 64
V_HEAD = 128
MLA_SCALE = (QK_NOPE + QK_ROPE) ** -0.5
ROPE_BASE = 10000.0
D_FF = 2816
N_EXPERTS = 8
D_EXPERT = 3584
EPS = 1e-6

N_PROMPT = BATCH * SEQ
N_SAMPLE = DEC_BATCH * DEC_SEQ
N_TOK = N_PROMPT + N_SAMPLE
N_COND = 8
LANES = 128
GLA_QKVR_W = 2 * GLA_DK_TOT + 2 * GLA_DV_TOT
GLA_CHUNK = 256
GLA_MAX_FACTORISED_DECAY = 80.0
ATTN_KEY_CHUNK = 512
MOE_BLOCK = 1024
MOE_TILE_UNIT = 128
MOE_W_CHUNKS = 14
LOG2E = 1.4426950408889634
VMEM_LIMIT = 56 * 1024 * 1024

F32 = jnp.float32
BF16 = jnp.bfloat16
HIGHEST = lax.Precision.HIGHEST
NT_DIMS = (((1,), (1,)), ((), ()))
TN_DIMS = (((0,), (0,)), ((), ()))


def _params(*sem):
    return pltpu.CompilerParams(dimension_semantics=sem, vmem_limit_bytes=VMEM_LIMIT)


def _cond_row(i, tm):
    return jnp.where(i * tm < N_PROMPT, 0, 1 + jnp.maximum(i * tm - N_PROMPT, 0) // DEC_SEQ)


def _mod_spec(which, tm, tn=D_MODEL, col=None):
    if col is None:
        return pl.BlockSpec((None, None, 1, tn), lambda i, *_: (_cond_row(i, tm), which, 0, 0))
    return pl.BlockSpec((None, None, 1, tn), lambda i, j: (_cond_row(i, tm), which, 0, j))


def _silu(x):
    return x / (1.0 + jnp.exp(-x))


def _rms(x):
    return x * lax.rsqrt(jnp.mean(x * x, axis=-1, keepdims=True) + EPS)


def _ada_kernel(c_ref, w_ref, b_ref, o_ref):
    o_ref[...] = jnp.dot(_silu(c_ref[...]), w_ref[...], precision=HIGHEST,
                         preferred_element_type=F32) + b_ref[...]


def _ada_mods(cond, ada_w, ada_b):
    tn = 1536
    return pl.pallas_call(
        _ada_kernel,
        grid=(DEPTH, 6 * D_MODEL // tn),
        in_specs=[pl.BlockSpec((N_COND, D_MODEL), lambda l, j: (0, 0)),
                  pl.BlockSpec((None, D_MODEL, tn), lambda l, j: (l, 0, j)),
                  pl.BlockSpec((None, 1, tn), lambda l, j: (l, 0, j))],
        out_specs=pl.BlockSpec((None, N_COND, tn), lambda l, j: (l, 0, j)),
        out_shape=jax.ShapeDtypeStruct((DEPTH, N_COND, 6 * D_MODEL), F32),
        compiler_params=_params("arbitrary", "arbitrary"),
        name="ada_mods",
    )(cond, ada_w, ada_b.reshape(DEPTH, 1, 6 * D_MODEL))


def _modulate_kernel(x_ref, g_ref, sh_ref, sc_ref, o_ref):
    y = _rms(x_ref[...]) * g_ref[...]
    o_ref[...] = (y * (1.0 + sc_ref[...]) + sh_ref[...]).astype(o_ref.dtype)


def _modulate(x, g, mods, shift_idx, scale_idx, tm=1024):
    return pl.pallas_call(
        _modulate_kernel,
        grid=(N_TOK // tm,),
        in_specs=[pl.BlockSpec((tm, D_MODEL), lambda i: (i, 0)),
                  pl.BlockSpec((1, D_MODEL), lambda i: (0, 0)),
                  _mod_spec(shift_idx, tm), _mod_spec(scale_idx, tm)],
        out_specs=pl.BlockSpec((tm, D_MODEL), lambda i: (i, 0)),
        out_shape=jax.ShapeDtypeStruct((N_TOK, D_MODEL), BF16),
        compiler_params=_params("arbitrary"),
        name="modulate",
    )(x, g.reshape(1, D_MODEL), mods, mods)


def _modulate_router_kernel(x_ref, g_ref, sh_ref, sc_ref, r_ref, o_ref, gates_ref, slot_ref, cnt_ref):
    y = _rms(x_ref[...]) * g_ref[...]
    h = y * (1.0 + sc_ref[...]) + sh_ref[...]
    o_ref[...] = h.astype(o_ref.dtype)
    logits = _dot_split(h, r_ref[...])
    lane = lax.broadcasted_iota(jnp.int32, logits.shape, 1)
    l1 = jnp.where(lane < N_EXPERTS, logits, -jnp.inf)
    m1 = jnp.max(l1, axis=1, keepdims=True)
    i1 = jnp.min(jnp.where(l1 == m1, lane, LANES), axis=1, keepdims=True)
    l2 = jnp.where(lane == i1, -jnp.inf, l1)
    m2 = jnp.max(l2, axis=1, keepdims=True)
    i2 = jnp.min(jnp.where(l2 == m2, lane, LANES), axis=1, keepdims=True)
    e = jnp.exp(m2 - m1)
    w1 = 1.0 / (1.0 + e)
    w2 = e / (1.0 + e)
    gates_ref[...] = jnp.where(lane == i1, w1, 0.0) + jnp.where(lane == i2, w2, 0.0)
    sel = (lane == i1) | (lane == i2)
    tb = logits.shape[0]
    earlier = (lax.broadcasted_iota(jnp.int32, (tb, tb), 1)
               < lax.broadcasted_iota(jnp.int32, (tb, tb), 0)).astype(BF16)
    before = jnp.dot(earlier, sel.astype(BF16), preferred_element_type=F32)
    slot_ref[...] = jnp.where(sel, before + 1.0, 0.0)
    cnt_ref[...] = jnp.broadcast_to(jnp.sum(sel.astype(F32), axis=0, keepdims=True), cnt_ref.shape)


def _modulate_router(x, g, mods, shift_idx, scale_idx, router_pad):
    tm = MOE_BLOCK
    nb = N_TOK // tm
    return pl.pallas_call(
        _modulate_router_kernel,
        grid=(nb,),
        in_specs=[pl.BlockSpec((tm, D_MODEL), lambda i: (i, 0)),
                  pl.BlockSpec((1, D_MODEL), lambda i: (0, 0)),
                  _mod_spec(shift_idx, tm), _mod_spec(scale_idx, tm),
                  pl.BlockSpec((D_MODEL, LANES), lambda i: (0, 0))],
        out_specs=[pl.BlockSpec((tm, D_MODEL), lambda i: (i, 0)),
                   pl.BlockSpec((tm, LANES), lambda i: (i, 0)),
                   pl.BlockSpec((tm, LANES), lambda i: (i, 0)),
                   pl.BlockSpec((None, 8, LANES), lambda i: (i, 0, 0))],
        out_shape=[jax.ShapeDtypeStruct((N_TOK, D_MODEL), BF16),
                   jax.ShapeDtypeStruct((N_TOK, LANES), F32),
                   jax.ShapeDtypeStruct((N_TOK, LANES), F32),
                   jax.ShapeDtypeStruct((nb, 8, LANES), F32)],
        compiler_params=_params("arbitrary"),
        name="modulate_router",
    )(x, g.reshape(1, D_MODEL), mods, mods, router_pad)


def _moe_expert_kernel(cnt_ref, h_ref, gates_ref, slot_ref, wgu_ref, wd_ref, *rest, expert, first):
    wgu_s, wd_s = rest[-2:]
    y_ref = rest[-3]
    step = pl.program_id(0)
    half = MOE_W_CHUNKS // 2

    wc = wgu_ref.shape[1]
    for c in range(MOE_W_CHUNKS):
        @pl.when(step == c)
        def _(c=c):
            wgu_s[:, c * wc:(c + 1) * wc] = wgu_ref[...].astype(BF16)

    @pl.when(step < half)
    def _():
        wd_s[pl.ds(pl.multiple_of(step * wc, wc), wc), :] = wd_ref[...].astype(BF16)

    @pl.when(step >= MOE_W_CHUNKS)
    def _():
        tb = h_ref.shape[0]
        gate_col = gates_ref[:, expert:expert + 1]
        slot_col = slot_ref[:, expert:expert + 1]
        count = cnt_ref[(step - MOE_W_CHUNKS) * N_EXPERTS + expert]
        cw = 256
        fh = D_EXPERT // 2

        def y_before(cols):
            return jnp.zeros((tb, cw), F32) if first else rest[0][:, cols]

        def tile(first_slot, rows, fresh):
            lane = lax.broadcasted_iota(jnp.int32, (tb, rows), 1).astype(F32)
            onehot = (slot_col == lane + (first_slot + 1).astype(F32)).astype(BF16)
            hg = lax.dot_general(onehot, h_ref[...], TN_DIMS,
                                 preferred_element_type=F32).astype(BF16)
            out = None
            for f0 in (0, fh):
                g = jnp.dot(hg, wgu_s[:, f0:f0 + fh], preferred_element_type=F32)
                u = jnp.dot(hg, wgu_s[:, D_EXPERT + f0:D_EXPERT + f0 + fh],
                            preferred_element_type=F32)
                part = jnp.dot((_silu(g) * u).astype(BF16), wd_s[f0:f0 + fh, :],
                               preferred_element_type=F32)
                out = part if out is None else out + part
            out = out.astype(BF16)
            for c0 in range(0, D_MODEL, cw):
                cols = slice(c0, c0 + cw)
                prev = jnp.where(fresh, y_before(cols), y_ref[:, cols])
                y_ref[:, cols] = prev + gate_col * jnp.dot(onehot, out[:, cols],
                                                           preferred_element_type=F32)

        unit = MOE_TILE_UNIT
        units = (count + (unit - 1)) // unit
        n_big = units // 2

        def big_tile(k, carry):
            tile(k * (2 * unit), 2 * unit, k == 0)
            return carry

        lax.fori_loop(0, n_big, big_tile, 0)

        @pl.when(units - 2 * n_big == 1)
        def _():
            tile(n_big * (2 * unit), unit, n_big == 0)

        @pl.when(count == 0)
        def _():
            for c0 in range(0, D_MODEL, cw):
                y_ref[:, c0:c0 + cw] = y_before(slice(c0, c0 + cw))


def _moe_expert(h, gates, slots, counts, w_gu, w_down, y, layer, expert):
    tb = MOE_BLOCK
    nc = MOE_W_CHUNKS
    wc = 2 * D_EXPERT // nc
    first = y is None
    blk = lambda s, cnt: (jnp.maximum(s - nc, 0), 0)
    in_specs = [pl.BlockSpec((tb, D_MODEL), blk), pl.BlockSpec((tb, LANES), blk),
                pl.BlockSpec((tb, LANES), blk),
                pl.BlockSpec((None, None, D_MODEL, wc),
                             lambda s, cnt: (layer, expert, 0, jnp.minimum(s, nc - 1))),
                pl.BlockSpec((None, None, wc, D_MODEL),
                             lambda s, cnt: (layer, expert, jnp.minimum(s, nc // 2 - 1), 0))]
    args = [h, gates, slots, w_gu, w_down]
    if not first:
        in_specs.append(pl.BlockSpec((tb, D_MODEL), blk))
        args.append(y)
    return pl.pallas_call(
        functools.partial(_moe_expert_kernel, expert=expert, first=first),
        grid_spec=pltpu.PrefetchScalarGridSpec(
            num_scalar_prefetch=1, grid=(nc + N_TOK // tb,), in_specs=in_specs,
            out_specs=pl.BlockSpec((tb, D_MODEL), blk),
            scratch_shapes=[pltpu.VMEM((D_MODEL, 2 * D_EXPERT), BF16),
                            pltpu.VMEM((D_EXPERT, D_MODEL), BF16)]),
        out_shape=jax.ShapeDtypeStruct((N_TOK, D_MODEL), F32),
        compiler_params=_params("arbitrary"),
        name="moe_expert",
    )(counts, *args)


def _mm_kernel(a_ref, w_ref, o_ref):
    o_ref[...] = jnp.dot(a_ref[...], w_ref[...], preferred_element_type=F32).astype(o_ref.dtype)


def _matmul(a, w, out_dtype, tm, tn):
    m, k = a.shape
    n = w.shape[1]
    return pl.pallas_call(
        _mm_kernel,
        grid=(m // tm, n // tn),
        in_specs=[pl.BlockSpec((tm, k), lambda i, j: (i, 0)),
                  pl.BlockSpec((k, tn), lambda i, j: (0, j))],
        out_specs=pl.BlockSpec((tm, tn), lambda i, j: (i, j)),
        out_shape=jax.ShapeDtypeStruct((m, n), out_dtype),
        compiler_params=_params("arbitrary", "arbitrary"),
        name="matmul",
    )(a, w)


def _swiglu_kernel(a_ref, wg_ref, wu_ref, o_ref):
    a = a_ref[...]
    g = jnp.dot(a, wg_ref[...], preferred_element_type=F32)
    u = jnp.dot(a, wu_ref[...], preferred_element_type=F32)
    o_ref[...] = (_silu(g) * u).astype(o_ref.dtype)


def _swiglu_up(a, w_gu, tm, tn):
    m, k = a.shape
    f = w_gu.shape[1] // 2
    nj = f // tn
    return pl.pallas_call(
        _swiglu_kernel,
        grid=(m // tm, nj),
        in_specs=[pl.BlockSpec((tm, k), lambda i, j: (i, 0)),
                  pl.BlockSpec((k, tn), lambda i, j: (0, j)),
                  pl.BlockSpec((k, tn), lambda i, j: (0, j + nj))],
        out_specs=pl.BlockSpec((tm, tn), lambda i, j: (i, j)),
        out_shape=jax.ShapeDtypeStruct((m, f), BF16),
        compiler_params=_params("arbitrary", "arbitrary"),
        name="swiglu_up",
    )(a, w_gu, w_gu)


def _next_norm_specs(next_norm, tm):
    gain, mods, shift_idx, scale_idx = next_norm
    specs = [pl.BlockSpec((1, D_MODEL), lambda i, *_: (0, 0)),
             _mod_spec(shift_idx, tm), _mod_spec(scale_idx, tm)]
    return specs, [gain.reshape(1, D_MODEL), mods, mods]


def _mm_residual_norm_kernel(a_ref, w_ref, x_ref, g_ref, ng_ref, nsh_ref, nsc_ref, o_ref, h_ref):
    x = x_ref[...] + g_ref[...] * jnp.dot(a_ref[...], w_ref[...], preferred_element_type=F32)
    o_ref[...] = x
    h_ref[...] = (_rms(x) * ng_ref[...] * (1.0 + nsc_ref[...]) + nsh_ref[...]).astype(h_ref.dtype)


def _matmul_residual(a, w, x, mods, gate_idx, next_norm, tm=512):
    k = a.shape[1]
    norm_specs, norm_args = _next_norm_specs(next_norm, tm)
    row = lambda i: (i, 0)
    return pl.pallas_call(
        _mm_residual_norm_kernel,
        grid=(N_TOK // tm,),
        in_specs=[pl.BlockSpec((tm, k), row),
                  pl.BlockSpec((k, D_MODEL), lambda i: (0, 0)),
                  pl.BlockSpec((tm, D_MODEL), row),
                  _mod_spec(gate_idx, tm)] + norm_specs,
        out_specs=[pl.BlockSpec((tm, D_MODEL), row), pl.BlockSpec((tm, D_MODEL), row)],
        out_shape=[jax.ShapeDtypeStruct((N_TOK, D_MODEL), F32),
                   jax.ShapeDtypeStruct((N_TOK, D_MODEL), BF16)],
        compiler_params=_params("arbitrary"),
        name="matmul_residual",
    )(a, w, x, mods, *norm_args)


def _mm_residual_split_kernel(ap_ref, as_ref, w_ref, x_ref, g_ref, o_ref):
    is_prompt = pl.program_id(0) < N_PROMPT // o_ref.shape[0]
    a = jnp.where(is_prompt, ap_ref[...], as_ref[...])
    o_ref[...] = x_ref[...] + g_ref[...] * jnp.dot(a, w_ref[...], preferred_element_type=F32)


def _matmul_residual_split(a_prompt, a_sample, w, x, mods, gate_idx, tm=1024, tn=512):
    k = w.shape[0]
    return pl.pallas_call(
        _mm_residual_split_kernel,
        grid=(N_TOK // tm, D_MODEL // tn),
        in_specs=(_prompt_sample_specs(tm, k)
                  + [pl.BlockSpec((k, tn), lambda i, j: (0, j)),
                     pl.BlockSpec((tm, tn), lambda i, j: (i, j)),
                     _mod_spec(gate_idx, tm, tn, col=True)]),
        out_specs=pl.BlockSpec((tm, tn), lambda i, j: (i, j)),
        out_shape=jax.ShapeDtypeStruct((N_TOK, D_MODEL), F32),
        compiler_params=_params("arbitrary", "arbitrary"),
        name="matmul_residual_split",
    )(a_prompt, a_sample, w, x, mods)


def _gated_add_norm_kernel(x_ref, y_ref, g_ref, ng_ref, nsh_ref, nsc_ref, o_ref, h_ref):
    x = x_ref[...] + g_ref[...] * y_ref[...]
    o_ref[...] = x
    h_ref[...] = (_rms(x) * ng_ref[...] * (1.0 + nsc_ref[...]) + nsh_ref[...]).astype(h_ref.dtype)


def _gated_add(x, y, mods, gate_idx, next_norm, tm=1024):
    norm_specs, norm_args = _next_norm_specs(next_norm, tm)
    row = lambda i: (i, 0)
    return pl.pallas_call(
        _gated_add_norm_kernel,
        grid=(N_TOK // tm,),
        in_specs=[pl.BlockSpec((tm, D_MODEL), row), pl.BlockSpec((tm, D_MODEL), row),
                  _mod_spec(gate_idx, tm)] + norm_specs,
        out_specs=[pl.BlockSpec((tm, D_MODEL), row), pl.BlockSpec((tm, D_MODEL), row)],
        out_shape=[jax.ShapeDtypeStruct((N_TOK, D_MODEL), F32),
                   jax.ShapeDtypeStruct((N_TOK, D_MODEL), BF16)],
        compiler_params=_params("arbitrary"),
        name="gated_add",
    )(x, y, mods, *norm_args)


def _gated_add_final_kernel(x_ref, y_ref, g_ref, ng_ref, o_ref):
    o_ref[...] = _rms(x_ref[...] + g_ref[...] * y_ref[...]) * ng_ref[...]


def _gated_add_final(x, y, mods, gate_idx, gain, tm=1024):
    row = lambda i: (i, 0)
    return pl.pallas_call(
        _gated_add_final_kernel,
        grid=(N_TOK // tm,),
        in_specs=[pl.BlockSpec((tm, D_MODEL), row), pl.BlockSpec((tm, D_MODEL), row),
                  _mod_spec(gate_idx, tm), pl.BlockSpec((1, D_MODEL), lambda i: (0, 0))],
        out_specs=pl.BlockSpec((tm, D_MODEL), row),
        out_shape=jax.ShapeDtypeStruct((N_TOK, D_MODEL), F32),
        compiler_params=_params("arbitrary"),
        name="gated_add_final",
    )(x, y, mods, gain.reshape(1, D_MODEL))


def _log_sigmoid(x):
    return jnp.minimum(x, 0.0) - jnp.log(1.0 + jnp.exp(-jnp.abs(x)))


def _split_bf16(x):
    hi = x.astype(BF16)
    return hi, (x - hi.astype(F32)).astype(BF16)


def _dot_split(a, b):
    a_hi, a_lo = _split_bf16(a)
    b_hi, b_lo = _split_bf16(b)
    return (jnp.dot(a_hi, b_hi, preferred_element_type=F32)
            + jnp.dot(a_lo, b_hi, preferred_element_type=F32)
            + jnp.dot(a_hi, b_lo, preferred_element_type=F32))


def _gla_gate_kernel(h_ref, wgd_ref, w0_ref, w1_ref, b_ref, o_ref):
    gd = jnp.dot(h_ref[...], wgd_ref[...], preferred_element_type=F32)
    x0 = _dot_split(gd, w0_ref[...]) + b_ref[0:1, :]
    x1 = _dot_split(gd, w1_ref[...]) + b_ref[1:2, :]
    o_ref[:, :GLA_DK_TOT] = _log_sigmoid(x0) * (1.0 / GATE_TEMP)
    o_ref[:, GLA_DK_TOT:] = _log_sigmoid(x1) * (1.0 / GATE_TEMP)


def _gla_gates(h, w_gd, w0, w1, b_gate, tm=1024):
    return pl.pallas_call(
        _gla_gate_kernel,
        grid=(N_TOK // tm,),
        in_specs=[pl.BlockSpec((tm, D_MODEL), lambda i: (i, 0)),
                  pl.BlockSpec((D_MODEL, LANES), lambda i: (0, 0)),
                  pl.BlockSpec((LANES, GLA_DK_TOT), lambda i: (0, 0)),
                  pl.BlockSpec((LANES, GLA_DK_TOT), lambda i: (0, 0)),
                  pl.BlockSpec((2, GLA_DK_TOT), lambda i: (0, 0))],
        out_specs=pl.BlockSpec((tm, 2 * GLA_DK_TOT), lambda i: (i, 0)),
        out_shape=jax.ShapeDtypeStruct((N_TOK, 2 * GLA_DK_TOT), F32),
        compiler_params=_params("arbitrary"),
        name="gla_gates",
    )(h, w_gd, w0, w1, b_gate)


def _gla_keep_mask(causal):
    c = GLA_CHUNK
    row = lax.broadcasted_iota(jnp.int32, (c, c), 0)
    col = lax.broadcasted_iota(jnp.int32, (c, c), 1)
    return (col <= row) if causal else (col >= row)


def _gla_cum_decay(la_ref, causal):
    mask = _gla_keep_mask(causal).astype(BF16)
    la = la_ref[...]
    hi = la.astype(BF16)
    lo = (la - hi.astype(F32)).astype(BF16)
    return (jnp.dot(mask, hi, preferred_element_type=F32)
            + jnp.dot(mask, lo, preferred_element_type=F32))


def _gla_direction(q_ref, k_ref, v_ref, b, o_ref, s_ref, b_ref, d, causal, factorised):
    c = GLA_CHUNK
    keep = _gla_keep_mask(causal)
    end = c - 1 if causal else 0
    b_end = b[end:end + 1, :]
    q_dec = q_ref[...].astype(F32) * (GLA_DK ** -0.5) * jnp.exp(b)
    k_end = k_ref[...] * jnp.exp(b_end - b)
    e_end = jnp.exp(b_end)
    eye = (lax.broadcasted_iota(jnp.int32, (GLA_DK, GLA_DK), 0)
           == lax.broadcasted_iota(jnp.int32, (GLA_DK, GLA_DK), 1))
    if factorised:
        k_inv = k_ref[...] * jnp.exp(-b)
    else:
        b_ref[...] = b
        col = lax.broadcasted_iota(jnp.int32, (c, c), 1)
    for h in range(GLA_HEADS):
        ks = slice(h * GLA_DK, (h + 1) * GLA_DK)
        vs = slice(h * GLA_DV, (h + 1) * GLA_DV)
        state = s_ref[d, h]
        qh = q_dec[:, ks].astype(BF16)
        vh = v_ref[:, vs].astype(BF16)
        if factorised:
            scores = lax.dot_general(qh, k_inv[:, ks].astype(BF16), NT_DIMS,
                                     preferred_element_type=F32)
        else:
            def key_column(s, sc, ks=ks):
                base = pl.multiple_of((s // 16) * 16, 16)
                pick = lax.broadcasted_iota(jnp.int32, (16, GLA_DK), 0) == s % 16
                b_s = jnp.sum(jnp.where(pick, b_ref[pl.ds(base, 16), ks], 0.0),
                              axis=0, keepdims=True)
                k_s = jnp.sum(jnp.where(pick, k_ref[pl.ds(base, 16), ks].astype(F32), 0.0),
                              axis=0, keepdims=True)
                decay = jnp.exp(jnp.minimum(b_ref[:, ks] - b_s, 0.0))
                column = jnp.sum(q_ref[:, ks].astype(F32) * (GLA_DK ** -0.5) * k_s * decay,
                                 axis=1, keepdims=True)
                return jnp.where(col == s, column, sc)

            scores = lax.fori_loop(0, c, key_column, jnp.zeros((c, c), F32))
        scores = jnp.where(keep, scores, 0.0)
        o_ref[:, vs] = (jnp.dot(scores.astype(BF16), vh, preferred_element_type=F32)
                        + jnp.dot(qh, state.astype(BF16), preferred_element_type=F32))
        e_col = jnp.sum(jnp.where(eye, jnp.broadcast_to(e_end[:, ks], (GLA_DK, GLA_DK)), 0.0),
                        axis=1, keepdims=True)
        s_ref[d, h] = state * e_col + lax.dot_general(
            k_end[:, ks].astype(BF16), vh, TN_DIMS, preferred_element_type=F32)


def _gla_scan_kernel(*refs, n_chunks, has_s0, write_state):
    qf, kf, vf, laf, qb, kb, vb, lab = refs[:8]
    rest = list(refs[8:])
    s0_ref = rest.pop(0) if has_s0 else None
    of_ref, ob_ref = rest.pop(0), rest.pop(0)
    sout_ref = rest.pop(0) if write_state else None
    s_ref, b_ref = rest
    i = pl.program_id(1)

    @pl.when(i == 0)
    def _():
        s_ref[...] = s0_ref[...] if has_s0 else jnp.zeros(s_ref.shape, F32)

    b_f = _gla_cum_decay(laf, True)
    b_b = _gla_cum_decay(lab, False)
    c = GLA_CHUNK
    total_decay = jnp.minimum(jnp.min(b_f[c - 1:c, :]), jnp.min(b_b[0:1, :]))
    factorisable = total_decay >= -GLA_MAX_FACTORISED_DECAY

    def step(factorised):
        _gla_direction(qf, kf, vf, b_f, of_ref, s_ref, b_ref.at[0], 0, True, factorised)
        _gla_direction(qb, kb, vb, b_b, ob_ref, s_ref, b_ref.at[1], 1, False, factorised)

    pl.when(factorisable)(functools.partial(step, True))
    pl.when(jnp.logical_not(factorisable))(functools.partial(step, False))

    if write_state:
        @pl.when(i == n_chunks - 1)
        def _():
            sout_ref[...] = s_ref[...]


def _gla_scan(proj, la, row0, n_batch, seq, s0):
    c = GLA_CHUNK
    n = seq // c
    base = row0 // c
    has_s0 = s0 is not None
    write_state = not has_s0
    fwd = lambda b, i: base + b * n + i
    bwd = lambda b, i: base + b * n + (n - 1 - i)

    def specs(blk):
        return [pl.BlockSpec((c, GLA_DK_TOT), lambda b, i: (blk(b, i), 0)),
                pl.BlockSpec((c, GLA_DK_TOT), lambda b, i: (blk(b, i), 1)),
                pl.BlockSpec((c, GLA_DV_TOT), lambda b, i: (blk(b, i), 1))]
    state_block = (None, 2, GLA_HEADS, GLA_DK, GLA_DV)
    state_spec = pl.BlockSpec(state_block, lambda b, i: (b, 0, 0, 0, 0))
    in_specs = (specs(fwd) + [pl.BlockSpec((c, GLA_DK_TOT), lambda b, i: (fwd(b, i), 0))]
                + specs(bwd) + [pl.BlockSpec((c, GLA_DK_TOT), lambda b, i: (bwd(b, i), 1))])
    args = [proj, proj, proj, la, proj, proj, proj, la]
    if has_s0:
        in_specs.append(state_spec)
        args.append(s0)
    out_rows = n_batch * seq
    out_specs = [pl.BlockSpec((c, GLA_DV_TOT), lambda b, i: (b * n + i, 0)),
                 pl.BlockSpec((c, GLA_DV_TOT), lambda b, i: (b * n + (n - 1 - i), 0))]
    out_shape = [jax.ShapeDtypeStruct((out_rows, GLA_DV_TOT), F32)] * 2
    if write_state:
        out_specs.append(state_spec)
        out_shape.append(jax.ShapeDtypeStruct((n_batch, 2, GLA_HEADS, GLA_DK, GLA_DV), F32))
    return pl.pallas_call(
        functools.partial(_gla_scan_kernel, n_chunks=n, has_s0=has_s0, write_state=write_state),
        grid=(n_batch, n),
        in_specs=in_specs,
        out_specs=out_specs,
        out_shape=out_shape,
        scratch_shapes=[pltpu.VMEM((2, GLA_HEADS, GLA_DK, GLA_DV), F32),
                        pltpu.VMEM((2, c, GLA_DK_TOT), F32)],
        compiler_params=_params("arbitrary", "arbitrary"),
        name="gla_scan",
    )(*args)


def _prompt_sample_specs(tm, width):
    n_p = N_PROMPT // tm
    return [pl.BlockSpec((tm, width), lambda i, *_: (jnp.minimum(i, n_p - 1), 0)),
            pl.BlockSpec((tm, width), lambda i, *_: (jnp.maximum(i - n_p, 0), 0))]


def _gla_post_kernel(ofp_ref, ofs_ref, obp_ref, obs_ref, r_ref, g_ref, o_ref):
    is_prompt = pl.program_id(0) < N_PROMPT // o_ref.shape[0]
    o = jnp.where(is_prompt, ofp_ref[...] + obp_ref[...], ofs_ref[...] + obs_ref[...])
    normed = jnp.concatenate(
        [_rms(o[:, h * GLA_DV:(h + 1) * GLA_DV]) for h in range(GLA_HEADS)], axis=1)
    o_ref[...] = (normed * g_ref[...] * _silu(r_ref[...].astype(F32))).astype(o_ref.dtype)


def _gla_post(of_p, of_s, ob_p, ob_s, proj, g_tiled, tm=1024):
    return pl.pallas_call(
        _gla_post_kernel,
        grid=(N_TOK // tm,),
        in_specs=(_prompt_sample_specs(tm, GLA_DV_TOT) + _prompt_sample_specs(tm, GLA_DV_TOT)
                  + [pl.BlockSpec((tm, GLA_DV_TOT), lambda i: (i, 2)),
                     pl.BlockSpec((1, GLA_DV_TOT), lambda i: (0, 0))]),
        out_specs=pl.BlockSpec((tm, GLA_DV_TOT), lambda i: (i, 0)),
        out_shape=jax.ShapeDtypeStruct((N_TOK, GLA_DV_TOT), BF16),
        compiler_params=_params("arbitrary"),
        name="gla_post",
    )(of_p, of_s, ob_p, ob_s, proj, g_tiled)


def _rope_pair(u, cs):
    t = u * cs
    return t + pltpu.roll(t, QK_ROPE, axis=1)


def _mla_down_kernel(a_ref, w_ref, qn_ref, kvn_ref, cs_ref, cq_ref, ckv_ref, kr_ref, krp_ref):
    acc = jnp.dot(a_ref[...], w_ref[...], preferred_element_type=F32)
    cq_ref[...] = (_rms(acc[:, :Q_LORA]) * qn_ref[...]).astype(cq_ref.dtype)
    ckv_ref[...] = _rms(acc[:, Q_LORA:Q_LORA + KV_LORA]) * kvn_ref[...]
    u = acc[:, Q_LORA + KV_LORA:]
    kr_ref[...] = u
    lane = lax.broadcasted_iota(jnp.int32, u.shape, 1)
    krp_ref[...] = jnp.where(lane < QK_ROPE, _rope_pair(u, cs_ref[...]), 0.0).astype(krp_ref.dtype)


def _mla_down(h, w_ext, q_norm, kv_norm, cs, tm=512):
    wn = w_ext.shape[1]
    row = lambda i: (i, 0)
    fixed = lambda i: (0, 0)
    return pl.pallas_call(
        _mla_down_kernel,
        grid=(N_TOK // tm,),
        in_specs=[pl.BlockSpec((tm, D_MODEL), row), pl.BlockSpec((D_MODEL, wn), fixed),
                  pl.BlockSpec((1, Q_LORA), fixed), pl.BlockSpec((1, KV_LORA), fixed),
                  pl.BlockSpec((tm, LANES), row)],
        out_specs=[pl.BlockSpec((tm, Q_LORA), row), pl.BlockSpec((tm, KV_LORA), row),
                   pl.BlockSpec((tm, LANES), row), pl.BlockSpec((tm, LANES), row)],
        out_shape=[jax.ShapeDtypeStruct((N_TOK, Q_LORA), BF16),
                   jax.ShapeDtypeStruct((N_TOK, KV_LORA), F32),
                   jax.ShapeDtypeStruct((N_TOK, LANES), F32),
                   jax.ShapeDtypeStruct((N_TOK, LANES), BF16)],
        compiler_params=_params("arbitrary"),
        name="mla_down",
    )(h, w_ext, q_norm.reshape(1, Q_LORA), kv_norm.reshape(1, KV_LORA), cs)


def _mla_q_kernel(a_ref, w_ref, cs_ref, o_ref):
    acc = jnp.dot(a_ref[...], w_ref[...], preferred_element_type=F32)
    cs = cs_ref[...]
    for h in range(MLA_HEADS):
        lo = h * 2 * LANES
        o_ref[:, lo:lo + LANES] = acc[:, lo:lo + LANES].astype(o_ref.dtype)
        o_ref[:, lo + LANES:lo + 2 * LANES] = _rope_pair(
            acc[:, lo + LANES:lo + 2 * LANES], cs).astype(o_ref.dtype)


def _mla_q(cq, w_ext, cs, tm=512):
    wn = w_ext.shape[1]
    return pl.pallas_call(
        _mla_q_kernel,
        grid=(N_TOK // tm,),
        in_specs=[pl.BlockSpec((tm, Q_LORA), lambda i: (i, 0)),
                  pl.BlockSpec((Q_LORA, wn), lambda i: (0, 0)),
                  pl.BlockSpec((tm, LANES), lambda i: (i, 0))],
        out_specs=pl.BlockSpec((tm, wn), lambda i: (i, 0)),
        out_shape=jax.ShapeDtypeStruct((N_TOK, wn), BF16),
        compiler_params=_params("arbitrary"),
        name="mla_q",
    )(cq, w_ext, cs)


def _attn_kernel(q_ref, kv_ref, kr_ref, o_ref, *, key_chunk, heads):
    tq = q_ref.shape[0]
    c = MLA_SCALE * LOG2E
    hw = QK_NOPE + V_HEAD
    for h in range(heads):
        q = q_ref[:, h * hw:(h + 1) * hw]
        m = jnp.full((tq, 1), -jnp.inf, F32)
        l = jnp.zeros((tq, 1), F32)
        acc = jnp.zeros((tq, V_HEAD), F32)
        for c0 in range(0, kv_ref.shape[0], key_chunk):
            rows = slice(c0, c0 + key_chunk)
            k = jnp.concatenate([kv_ref[rows, h * hw:h * hw + QK_NOPE], kr_ref[rows, :]], axis=1)
            s = lax.dot_general(q, k, NT_DIMS, preferred_element_type=F32)
            m_new = jnp.maximum(m, jnp.max(s, axis=1, keepdims=True))
            alpha = jnp.exp2((m - m_new) * c)
            p = jnp.exp2((s - m_new) * c)
            l = alpha * l + jnp.sum(p, axis=1, keepdims=True)
            acc = alpha * acc + jnp.dot(p.astype(BF16), kv_ref[rows, h * hw + QK_NOPE:(h + 1) * hw],
                                        preferred_element_type=F32)
            m = m_new
        o_ref[:, h * V_HEAD:(h + 1) * V_HEAD] = (acc / l).astype(o_ref.dtype)


def _attention(q, kv, krp, row0, n_batch, seq, n_keys, tq, heads):
    nq = seq // tq
    base = row0 // tq
    hw = heads * (QK_NOPE + V_HEAD)
    return pl.pallas_call(
        functools.partial(_attn_kernel, key_chunk=min(n_keys, ATTN_KEY_CHUNK), heads=heads),
        grid=(n_batch, MLA_HEADS // heads, nq),
        in_specs=[pl.BlockSpec((tq, hw), lambda b, h, i: (base + b * nq + i, h)),
                  pl.BlockSpec((n_keys, hw), lambda b, h, i: (b, h)),
                  pl.BlockSpec((n_keys, LANES), lambda b, h, i: (b, 0))],
        out_specs=pl.BlockSpec((tq, heads * V_HEAD), lambda b, h, i: (b * nq + i, h)),
        out_shape=jax.ShapeDtypeStruct((n_batch * seq, MLA_HEADS * V_HEAD), BF16),
        compiler_params=_params("arbitrary", "arbitrary", "arbitrary"),
        name="mla_attention",
    )(q, kv, krp)


def _rope_tables():
    half = QK_ROPE // 2
    pos = np.arange(DEC_SEQ)
    inv = ROPE_BASE ** (-np.arange(0, half, 2, dtype=np.float64) / half)
    ang = np.concatenate([(pos // GRID_W)[:, None] * inv, (pos % GRID_W)[:, None] * inv], axis=-1)
    ang = ang.astype(np.float32)
    cos = np.repeat(np.cos(ang), 2, axis=1)
    sin = np.repeat(np.sin(ang), 2, axis=1) * np.tile(np.array([-1.0, 1.0], np.float32), half)
    cs = np.concatenate([cos, sin], axis=1).astype(np.float32)
    ident = np.concatenate([np.ones((N_PROMPT, QK_ROPE), np.float32),
                            np.zeros((N_PROMPT, QK_ROPE), np.float32)], axis=1)
    return jnp.asarray(np.concatenate([ident, np.tile(cs, (DEC_BATCH, 1))], axis=0))


def kernel(x_prompt, x_sample, c, state_gla, cache_ckv, cache_krope, c_ctx, ada_w, ada_b, norm_mix, norm_ffn, norm_final, gla_w_in, gla_w_gate_up, gla_b_gate, gla_norm, gla_w_out, mla_w_down, mla_q_norm, mla_kv_norm, mla_w_uq, mla_w_ukv, mla_w_out, ffn_w_gate_up, ffn_w_down, moe_router, moe_w_gate_up, moe_w_down):
    x = jnp.concatenate([x_prompt.reshape(N_PROMPT, D_MODEL), x_sample.reshape(N_SAMPLE, D_MODEL)])
    cond = jnp.concatenate([c_ctx[None], c, jnp.zeros((N_COND - 1 - DEC_BATCH, D_MODEL), F32)])
    mods_all = _ada_mods(cond, ada_w, ada_b).reshape(DEPTH, N_COND, 6, 1, D_MODEL)
    cs = _rope_tables()
    swap = np.arange(QK_ROPE) ^ 1

    new_gla, new_ckv, new_kr = [], [], []
    h = _modulate(x, norm_mix[0], mods_all[0], 0, 1)
    for layer in range(DEPTH):
        j = layer // 2
        mods = mods_all[layer]
        if layer % 2 == 0:
            w_in = gla_w_in[j].astype(BF16)
            proj = _matmul(h, w_in[:, :GLA_QKVR_W], BF16, tm=1024, tn=768)
            w_gd = jnp.pad(w_in[:, GLA_QKVR_W:], ((0, 0), (0, LANES - 2 * GATE_RANK)))
            pad_lo = jnp.zeros((GATE_RANK, GLA_DK_TOT), F32)
            pad_hi = jnp.zeros((LANES - 2 * GATE_RANK, GLA_DK_TOT), F32)
            w0 = jnp.concatenate([gla_w_gate_up[j, 0], pad_lo, pad_hi])
            w1 = jnp.concatenate([pad_lo, gla_w_gate_up[j, 1], pad_hi])
            la = _gla_gates(h, w_gd, w0, w1, gla_b_gate[j])
            of_p, ob_p, st = _gla_scan(proj, la, 0, BATCH, SEQ, None)
            of_s, ob_s = _gla_scan(proj, la, N_PROMPT, DEC_BATCH, DEC_SEQ, state_gla[:, j])
            new_gla.append(st)
            mixed = _gla_post(of_p, of_s, ob_p, ob_s, proj,
                              jnp.tile(gla_norm[j], GLA_HEADS).reshape(1, GLA_DV_TOT))
            x, h = _matmul_residual(mixed, gla_w_out[j].astype(BF16), x, mods, 2,
                                    (norm_ffn[layer], mods, 3, 4))
            act = _swiglu_up(h, ffn_w_gate_up[j].astype(BF16), tm=1024, tn=1408)
            x, h = _matmul_residual(act, ffn_w_down[j].astype(BF16), x, mods, 5,
                                    (norm_mix[layer + 1], mods_all[layer + 1], 0, 1))
        else:
            wd = mla_w_down[j]
            w_down = jnp.concatenate([wd, wd[:, Q_LORA + KV_LORA:][:, swap]], axis=1).astype(BF16)
            cq, ckv, kr, krp = _mla_down(h, w_down, mla_q_norm[j], mla_kv_norm[j], cs)
            wq = mla_w_uq[j].reshape(Q_LORA, MLA_HEADS, QK_NOPE + QK_ROPE)
            w_uq = jnp.concatenate([wq, wq[:, :, QK_NOPE:][:, :, swap]], axis=2)
            q = _mla_q(cq, w_uq.reshape(Q_LORA, -1).astype(BF16), cs)
            w_ukv = mla_w_ukv[j].astype(BF16)
            ckv_b = ckv.astype(BF16)
            kv_p = _matmul(ckv_b[:N_PROMPT], w_ukv, BF16, tm=1024, tn=1024)
            o_p = _attention(q, kv_p, krp[:N_PROMPT], 0, BATCH, SEQ, SEQ, tq=SEQ, heads=MLA_HEADS)
            n_keys = PAST_LEN + DEC_SEQ
            ckv_keys = jnp.concatenate(
                [cache_ckv[:, j].astype(BF16), ckv_b[N_PROMPT:].reshape(DEC_BATCH, DEC_SEQ, KV_LORA)],
                axis=1).reshape(DEC_BATCH * n_keys, KV_LORA)
            kr_ctx = jnp.pad(cache_krope[:, j], ((0, 0), (0, 0), (0, LANES - QK_ROPE))).astype(BF16)
            krp_keys = jnp.concatenate(
                [kr_ctx, krp[N_PROMPT:].reshape(DEC_BATCH, DEC_SEQ, LANES)],
                axis=1).reshape(DEC_BATCH * n_keys, LANES)
            kv_s = _matmul(ckv_keys, w_ukv, BF16, tm=1024, tn=1024)
            o_s = _attention(q, kv_s, krp_keys, N_PROMPT, DEC_BATCH, DEC_SEQ, n_keys, tq=1024, heads=1)
            new_ckv.append(ckv[:N_PROMPT].reshape(BATCH, SEQ, KV_LORA))
            new_kr.append(kr[:N_PROMPT, :QK_ROPE].reshape(BATCH, SEQ, QK_ROPE))
            x = _matmul_residual_split(o_p, o_s, mla_w_out[j].astype(BF16), x, mods, 2)

            router_pad = jnp.pad(moe_router[j], ((0, 0), (0, LANES - N_EXPERTS)))
            h, gates, slots, cnt = _modulate_router(x, norm_ffn[layer], mods, 3, 4, router_pad)
            counts = cnt[:, 0, :N_EXPERTS].astype(jnp.int32).reshape(-1)
            y = None
            for e in range(N_EXPERTS):
                y = _moe_expert(h, gates, slots, counts, moe_w_gate_up, moe_w_down, y, j, e)
            if layer + 1 < DEPTH:
                x, h = _gated_add(x, y, mods, 5, (norm_mix[layer + 1], mods_all[layer + 1], 0, 1))
            else:
                out = _gated_add_final(x, y, mods, 5, norm_final)

    return (out[:N_PROMPT].reshape(BATCH, SEQ, D_MODEL),
            out[N_PROMPT:].reshape(DEC_BATCH, DEC_SEQ, D_MODEL),
            jnp.stack(new_gla, axis=1),
            jnp.stack(new_ckv, axis=1),
            jnp.stack(new_kr, axis=1))
```

```python
import functools

import jax
import jax.numpy as jnp
import numpy as np
from jax import lax
from jax.experimental import pallas as pl
from jax.experimental.pallas import tpu as pltpu

D_MODEL = 1024
BATCH = 16
SEQ = 256
DEPTH = 4
DEC_BATCH = 4
DEC_SEQ = 2048
PAST_LEN = 512
GRID_W = 64
GLA_HEADS = 4
GLA_DK = 128
GLA_DV = 256
GLA_DK_TOT = GLA_HEADS * GLA_DK
GLA_DV_TOT = GLA_HEADS * GLA_DV
GATE_RANK = 16
GATE_TEMP = 16.0
MLA_HEADS = 8
Q_LORA = 384
KV_LORA = 256
QK_NOPE = 128
QK_ROPE = 64
V_HEAD = 128
MLA_SCALE = (QK_NOPE + QK_ROPE) ** -0.5
ROPE_BASE = 10000.0
D_FF = 2816
N_EXPERTS = 8
D_EXPERT = 3584
EPS = 1e-6

N_PROMPT = BATCH * SEQ
N_SAMPLE = DEC_BATCH * DEC_SEQ
N_TOK = N_PROMPT + N_SAMPLE
N_COND = 8
LANES = 128
GLA_QKVR_W = 2 * GLA_DK_TOT + 2 * GLA_DV_TOT
GLA_CHUNK = 256
GLA_MAX_FACTORISED_DECAY = 80.0
ATTN_KEY_CHUNK = 512
MOE_BLOCK = 1024
MOE_TILE_UNIT = 128
MOE_W_CHUNKS = 14
LOG2E = 1.4426950408889634
VMEM_LIMIT = 56 * 1024 * 1024

F32 = jnp.float32
BF16 = jnp.bfloat16
HIGHEST = lax.Precision.HIGHEST
NT_DIMS = (((1,), (1,)), ((), ()))
TN_DIMS = (((0,), (0,)), ((), ()))


def _params(*sem):
    return pltpu.CompilerParams(dimension_semantics=sem, vmem_limit_bytes=VMEM_LIMIT)


def _cond_row(i, tm):
    return jnp.where(i * tm < N_PROMPT, 0, 1 + jnp.maximum(i * tm - N_PROMPT, 0) // DEC_SEQ)


def _mod_spec(which, tm, tn=D_MODEL, col=None):
    if col is None:
        return pl.BlockSpec((None, None, 1, tn), lambda i, *_: (_cond_row(i, tm), which, 0, 0))
    return pl.BlockSpec((None, None, 1, tn), lambda i, j: (_cond_row(i, tm), which, 0, j))


def _silu(x):
    return x / (1.0 + jnp.exp(-x))


def _rms(x):
    return x * lax.rsqrt(jnp.mean(x * x, axis=-1, keepdims=True) + EPS)


def _ada_kernel(c_ref, w_ref, b_ref, o_ref):
    o_ref[...] = jnp.dot(_silu(c_ref[...]), w_ref[...], precision=HIGHEST,
                         preferred_element_type=F32) + b_ref[...]


def _ada_mods(cond, ada_w, ada_b):
    tn = 1536
    return pl.pallas_call(
        _ada_kernel,
        grid=(DEPTH, 6 * D_MODEL // tn),
        in_specs=[pl.BlockSpec((N_COND, D_MODEL), lambda l, j: (0, 0)),
                  pl.BlockSpec((None, D_MODEL, tn), lambda l, j: (l, 0, j)),
                  pl.BlockSpec((None, 1, tn), lambda l, j: (l, 0, j))],
        out_specs=pl.BlockSpec((None, N_COND, tn), lambda l, j: (l, 0, j)),
        out_shape=jax.ShapeDtypeStruct((DEPTH, N_COND, 6 * D_MODEL), F32),
        compiler_params=_params("arbitrary", "arbitrary"),
        name="ada_mods",
    )(cond, ada_w, ada_b.reshape(DEPTH, 1, 6 * D_MODEL))


def _prompt_sample_specs(tm, width):
    n_p = N_PROMPT // tm
    return [pl.BlockSpec((tm, width), lambda i, *_: (jnp.minimum(i, n_p - 1), 0)),
            pl.BlockSpec((tm, width), lambda i, *_: (jnp.maximum(i - n_p, 0), 0))]


def _pick_prompt_sample(p_ref, s_ref, tm):
    return jnp.where(pl.program_id(0) < N_PROMPT // tm, p_ref[...], s_ref[...])


def _modulate_kernel(xp_ref, xs_ref, g_ref, sh_ref, sc_ref, o_ref):
    y = _rms(_pick_prompt_sample(xp_ref, xs_ref, o_ref.shape[0])) * g_ref[...]
    o_ref[...] = (y * (1.0 + sc_ref[...]) + sh_ref[...]).astype(o_ref.dtype)


def _modulate(x_prompt, x_sample, g, mods, shift_idx, scale_idx, tm=1024):
    return pl.pallas_call(
        _modulate_kernel,
        grid=(N_TOK // tm,),
        in_specs=(_prompt_sample_specs(tm, D_MODEL)
                  + [pl.BlockSpec((1, D_MODEL), lambda i: (0, 0)),
                     _mod_spec(shift_idx, tm), _mod_spec(scale_idx, tm)]),
        out_specs=pl.BlockSpec((tm, D_MODEL), lambda i: (i, 0)),
        out_shape=jax.ShapeDtypeStruct((N_TOK, D_MODEL), BF16),
        compiler_params=_params("arbitrary"),
        name="modulate",
    )(x_prompt, x_sample, g.reshape(1, D_MODEL), mods, mods)


def _modulate_router_kernel(x_ref, g_ref, sh_ref, sc_ref, r_ref, o_ref, gates_ref, slot_ref, cnt_ref):
    y = _rms(x_ref[...]) * g_ref[...]
    h = y * (1.0 + sc_ref[...]) + sh_ref[...]
    o_ref[...] = h.astype(o_ref.dtype)
    logits = _dot_split(h, r_ref[...])
    lane = lax.broadcasted_iota(jnp.int32, logits.shape, 1)
    l1 = jnp.where(lane < N_EXPERTS, logits, -jnp.inf)
    m1 = jnp.max(l1, axis=1, keepdims=True)
    i1 = jnp.min(jnp.where(l1 == m1, lane, LANES), axis=1, keepdims=True)
    l2 = jnp.where(lane == i1, -jnp.inf, l1)
    m2 = jnp.max(l2, axis=1, keepdims=True)
    i2 = jnp.min(jnp.where(l2 == m2, lane, LANES), axis=1, keepdims=True)
    e = jnp.exp(m2 - m1)
    w1 = 1.0 / (1.0 + e)
    w2 = e / (1.0 + e)
    gates_ref[...] = jnp.where(lane == i1, w1, 0.0) + jnp.where(lane == i2, w2, 0.0)
    sel = (lane == i1) | (lane == i2)
    tb = logits.shape[0]
    earlier = (lax.broadcasted_iota(jnp.int32, (tb, tb), 1)
               < lax.broadcasted_iota(jnp.int32, (tb, tb), 0)).astype(BF16)
    before = jnp.dot(earlier, sel.astype(BF16), preferred_element_type=F32)
    slot_ref[...] = jnp.where(sel, before + 1.0, 0.0)
    cnt_ref[...] = jnp.broadcast_to(jnp.sum(sel.astype(F32), axis=0, keepdims=True), cnt_ref.shape)


def _modulate_router(x, g, mods, shift_idx, scale_idx, router_pad):
    tm = MOE_BLOCK
    nb = N_TOK // tm
    return pl.pallas_call(
        _modulate_router_kernel,
        grid=(nb,),
        in_specs=[pl.BlockSpec((tm, D_MODEL), lambda i: (i, 0)),
                  pl.BlockSpec((1, D_MODEL), lambda i: (0, 0)),
                  _mod_spec(shift_idx, tm), _mod_spec(scale_idx, tm),
                  pl.BlockSpec((D_MODEL, LANES), lambda i: (0, 0))],
        out_specs=[pl.BlockSpec((tm, D_MODEL), lambda i: (i, 0)),
                   pl.BlockSpec((tm, LANES), lambda i: (i, 0)),
                   pl.BlockSpec((tm, LANES), lambda i: (i, 0)),
                   pl.BlockSpec((None, 8, LANES), lambda i: (i, 0, 0))],
        out_shape=[jax.ShapeDtypeStruct((N_TOK, D_MODEL), BF16),
                   jax.ShapeDtypeStruct((N_TOK, LANES), F32),
                   jax.ShapeDtypeStruct((N_TOK, LANES), F32),
                   jax.ShapeDtypeStruct((nb, 8, LANES), F32)],
        compiler_params=_params("arbitrary"),
        name="modulate_router",
    )(x, g.reshape(1, D_MODEL), mods, mods, router_pad)


def _moe_expert_kernel(cnt_ref, h_ref, gates_ref, slot_ref, wgu_ref, wd_ref, *rest, expert, first):
    wgu_s, wd_s = rest[-2:]
    y_ref = rest[-3]
    step = pl.program_id(0)
    half = MOE_W_CHUNKS // 2

    wc = wgu_ref.shape[1]
    for c in range(MOE_W_CHUNKS):
        @pl.when(step == c)
        def _(c=c):
            wgu_s[:, c * wc:(c + 1) * wc] = wgu_ref[...].astype(BF16)

    @pl.when(step < half)
    def _():
        wd_s[pl.ds(pl.multiple_of(step * wc, wc), wc), :] = wd_ref[...].astype(BF16)

    @pl.when(step >= MOE_W_CHUNKS)
    def _():
        tb = h_ref.shape[0]
        gate_col = gates_ref[:, expert:expert + 1]
        slot_col = slot_ref[:, expert:expert + 1]
        count = cnt_ref[(step - MOE_W_CHUNKS) * N_EXPERTS + expert]
        cw = 256
        fh = D_EXPERT // 2

        def y_before(cols):
            return jnp.zeros((tb, cw), F32) if first else rest[0][:, cols]

        def tile(first_slot, rows, fresh):
            lane = lax.broadcasted_iota(jnp.int32, (tb, rows), 1).astype(F32)
            onehot = (slot_col == lane + (first_slot + 1).astype(F32)).astype(BF16)
            hg = lax.dot_general(onehot, h_ref[...], TN_DIMS,
                                 preferred_element_type=F32).astype(BF16)
            out = None
            for f0 in (0, fh):
                g = jnp.dot(hg, wgu_s[:, f0:f0 + fh], preferred_element_type=F32)
                u = jnp.dot(hg, wgu_s[:, D_EXPERT + f0:D_EXPERT + f0 + fh],
                            preferred_element_type=F32)
                part = jnp.dot((_silu(g) * u).astype(BF16), wd_s[f0:f0 + fh, :],
                               preferred_element_type=F32)
                out = part if out is None else out + part
            out = out.astype(BF16)
            for c0 in range(0, D_MODEL, cw):
                cols = slice(c0, c0 + cw)
                prev = jnp.where(fresh, y_before(cols), y_ref[:, cols])
                y_ref[:, cols] = prev + gate_col * jnp.dot(onehot, out[:, cols],
                                                           preferred_element_type=F32)

        unit = MOE_TILE_UNIT
        units = (count + (unit - 1)) // unit
        n_big = units // 2

        def big_tile(k, carry):
            tile(k * (2 * unit), 2 * unit, k == 0)
            return carry

        lax.fori_loop(0, n_big, big_tile, 0)

        @pl.when(units - 2 * n_big == 1)
        def _():
            tile(n_big * (2 * unit), unit, n_big == 0)

        @pl.when(count == 0)
        def _():
            for c0 in range(0, D_MODEL, cw):
                y_ref[:, c0:c0 + cw] = y_before(slice(c0, c0 + cw))


def _moe_expert(h, gates, slots, counts, w_gu, w_down, y, layer, expert):
    tb = MOE_BLOCK
    nc = MOE_W_CHUNKS
    wc = 2 * D_EXPERT // nc
    first = y is None
    blk = lambda s, cnt: (jnp.maximum(s - nc, 0), 0)
    in_specs = [pl.BlockSpec((tb, D_MODEL), blk), pl.BlockSpec((tb, LANES), blk),
                pl.BlockSpec((tb, LANES), blk),
                pl.BlockSpec((None, None, D_MODEL, wc),
                             lambda s, cnt: (layer, expert, 0, jnp.minimum(s, nc - 1))),
                pl.BlockSpec((None, None, wc, D_MODEL),
                             lambda s, cnt: (layer, expert, jnp.minimum(s, nc // 2 - 1), 0))]
    args = [h, gates, slots, w_gu, w_down]
    if not first:
        in_specs.append(pl.BlockSpec((tb, D_MODEL), blk))
        args.append(y)
    return pl.pallas_call(
        functools.partial(_moe_expert_kernel, expert=expert, first=first),
        grid_spec=pltpu.PrefetchScalarGridSpec(
            num_scalar_prefetch=1, grid=(nc + N_TOK // tb,), in_specs=in_specs,
            out_specs=pl.BlockSpec((tb, D_MODEL), blk),
            scratch_shapes=[pltpu.VMEM((D_MODEL, 2 * D_EXPERT), BF16),
                            pltpu.VMEM((D_EXPERT, D_MODEL), BF16)]),
        out_shape=jax.ShapeDtypeStruct((N_TOK, D_MODEL), F32),
        compiler_params=_params("arbitrary"),
        name="moe_expert",
    )(counts, *args)


def _mm_kernel(a_ref, w_ref, o_ref):
    o_ref[...] = jnp.dot(a_ref[...], w_ref[...], preferred_element_type=F32).astype(o_ref.dtype)


def _matmul(a, w, out_dtype, tm, tn):
    m, k = a.shape
    n = w.shape[1]
    return pl.pallas_call(
        _mm_kernel,
        grid=(m // tm, n // tn),
        in_specs=[pl.BlockSpec((tm, k), lambda i, j: (i, 0)),
                  pl.BlockSpec((k, tn), lambda i, j: (0, j))],
        out_specs=pl.BlockSpec((tm, tn), lambda i, j: (i, j)),
        out_shape=jax.ShapeDtypeStruct((m, n), out_dtype),
        compiler_params=_params("arbitrary", "arbitrary"),
        name="matmul",
    )(a, w)


def _swiglu_kernel(a_ref, wg_ref, wu_ref, o_ref):
    a = a_ref[...]
    g = jnp.dot(a, wg_ref[...], preferred_element_type=F32)
    u = jnp.dot(a, wu_ref[...], preferred_element_type=F32)
    o_ref[...] = (_silu(g) * u).astype(o_ref.dtype)


def _swiglu_up(a, w_gu, tm, tn):
    m, k = a.shape
    f = w_gu.shape[1] // 2
    nj = f // tn
    return pl.pallas_call(
        _swiglu_kernel,
        grid=(m // tm, nj),
        in_specs=[pl.BlockSpec((tm, k), lambda i, j: (i, 0)),
                  pl.BlockSpec((k, tn), lambda i, j: (0, j)),
                  pl.BlockSpec((k, tn), lambda i, j: (0, j + nj))],
        out_specs=pl.BlockSpec((tm, tn), lambda i, j: (i, j)),
        out_shape=jax.ShapeDtypeStruct((m, f), BF16),
        compiler_params=_params("arbitrary", "arbitrary"),
        name="swiglu_up",
    )(a, w_gu, w_gu)


def _next_norm_specs(next_norm, tm):
    gain, mods, shift_idx, scale_idx = next_norm
    specs = [pl.BlockSpec((1, D_MODEL), lambda i, *_: (0, 0)),
             _mod_spec(shift_idx, tm), _mod_spec(scale_idx, tm)]
    return specs, [gain.reshape(1, D_MODEL), mods, mods]


def _mm_residual_norm_kernel(a_ref, w_ref, *refs, x_pair):
    if x_pair:
        x_in = _pick_prompt_sample(refs[0], refs[1], a_ref.shape[0])
        refs = refs[2:]
    else:
        x_in = refs[0][...]
        refs = refs[1:]
    g_ref, ng_ref, nsh_ref, nsc_ref, o_ref, h_ref = refs
    x = x_in + g_ref[...] * jnp.dot(a_ref[...], w_ref[...], preferred_element_type=F32)
    o_ref[...] = x
    h_ref[...] = (_rms(x) * ng_ref[...] * (1.0 + nsc_ref[...]) + nsh_ref[...]).astype(h_ref.dtype)


def _matmul_residual(a, w, x, mods, gate_idx, next_norm, tm=512):
    k = a.shape[1]
    norm_specs, norm_args = _next_norm_specs(next_norm, tm)
    row = lambda i: (i, 0)
    x_pair = isinstance(x, tuple)
    x_specs = _prompt_sample_specs(tm, D_MODEL) if x_pair else [pl.BlockSpec((tm, D_MODEL), row)]
    x_args = list(x) if x_pair else [x]
    return pl.pallas_call(
        functools.partial(_mm_residual_norm_kernel, x_pair=x_pair),
        grid=(N_TOK // tm,),
        in_specs=([pl.BlockSpec((tm, k), row), pl.BlockSpec((k, D_MODEL), lambda i: (0, 0))]
                  + x_specs + [_mod_spec(gate_idx, tm)] + norm_specs),
        out_specs=[pl.BlockSpec((tm, D_MODEL), row), pl.BlockSpec((tm, D_MODEL), row)],
        out_shape=[jax.ShapeDtypeStruct((N_TOK, D_MODEL), F32),
                   jax.ShapeDtypeStruct((N_TOK, D_MODEL), BF16)],
        compiler_params=_params("arbitrary"),
        name="matmul_residual",
    )(a, w, *x_args, mods, *norm_args)


def _mm_residual_split_kernel(ap_ref, as_ref, w_ref, x_ref, g_ref, o_ref):
    a = _pick_prompt_sample(ap_ref, as_ref, o_ref.shape[0])
    o_ref[...] = x_ref[...] + g_ref[...] * jnp.dot(a, w_ref[...], preferred_element_type=F32)


def _matmul_residual_split(a_prompt, a_sample, w, x, mods, gate_idx, tm=1024, tn=512):
    k = w.shape[0]
    return pl.pallas_call(
        _mm_residual_split_kernel,
        grid=(N_TOK // tm, D_MODEL // tn),
        in_specs=(_prompt_sample_specs(tm, k)
                  + [pl.BlockSpec((k, tn), lambda i, j: (0, j)),
                     pl.BlockSpec((tm, tn), lambda i, j: (i, j)),
                     _mod_spec(gate_idx, tm, tn, col=True)]),
        out_specs=pl.BlockSpec((tm, tn), lambda i, j: (i, j)),
        out_shape=jax.ShapeDtypeStruct((N_TOK, D_MODEL), F32),
        compiler_params=_params("arbitrary", "arbitrary"),
        name="matmul_residual_split",
    )(a_prompt, a_sample, w, x, mods)


def _gated_add_norm_kernel(x_ref, y_ref, g_ref, ng_ref, nsh_ref, nsc_ref, o_ref, h_ref):
    x = x_ref[...] + g_ref[...] * y_ref[...]
    o_ref[...] = x
    h_ref[...] = (_rms(x) * ng_ref[...] * (1.0 + nsc_ref[...]) + nsh_ref[...]).astype(h_ref.dtype)


def _gated_add(x, y, mods, gate_idx, next_norm, tm=1024):
    norm_specs, norm_args = _next_norm_specs(next_norm, tm)
    row = lambda i: (i, 0)
    return pl.pallas_call(
        _gated_add_norm_kernel,
        grid=(N_TOK // tm,),
        in_specs=[pl.BlockSpec((tm, D_MODEL), row), pl.BlockSpec((tm, D_MODEL), row),
                  _mod_spec(gate_idx, tm)] + norm_specs,
        out_specs=[pl.BlockSpec((tm, D_MODEL), row), pl.BlockSpec((tm, D_MODEL), row)],
        out_shape=[jax.ShapeDtypeStruct((N_TOK, D_MODEL), F32),
                   jax.ShapeDtypeStruct((N_TOK, D_MODEL), BF16)],
        compiler_params=_params("arbitrary"),
        name="gated_add",
    )(x, y, mods, *norm_args)


def _gated_add_final_kernel(x_ref, y_ref, g_ref, ng_ref, o_ref):
    o_ref[...] = _rms(x_ref[...] + g_ref[...] * y_ref[...]) * ng_ref[...]


def _gated_add_final(x, y, mods, gate_idx, gain, row0, n_rows, tm=1024):
    base = row0 // tm
    row = lambda i: (base + i, 0)
    return pl.pallas_call(
        _gated_add_final_kernel,
        grid=(n_rows // tm,),
        in_specs=[pl.BlockSpec((tm, D_MODEL), row), pl.BlockSpec((tm, D_MODEL), row),
                  pl.BlockSpec((None, None, 1, D_MODEL),
                               lambda i: (_cond_row(base + i, tm), gate_idx, 0, 0)),
                  pl.BlockSpec((1, D_MODEL), lambda i: (0, 0))],
        out_specs=pl.BlockSpec((tm, D_MODEL), lambda i: (i, 0)),
        out_shape=jax.ShapeDtypeStruct((n_rows, D_MODEL), F32),
        compiler_params=_params("arbitrary"),
        name="gated_add_final",
    )(x, y, mods, gain.reshape(1, D_MODEL))


def _log_sigmoid(x):
    return jnp.minimum(x, 0.0) - jnp.log(1.0 + jnp.exp(-jnp.abs(x)))


def _split_bf16(x):
    hi = x.astype(BF16)
    return hi, (x - hi.astype(F32)).astype(BF16)


def _dot_split(a, b):
    a_hi, a_lo = _split_bf16(a)
    b_hi, b_lo = _split_bf16(b)
    return (jnp.dot(a_hi, b_hi, preferred_element_type=F32)
            + jnp.dot(a_lo, b_hi, preferred_element_type=F32)
            + jnp.dot(a_hi, b_lo, preferred_element_type=F32))


def _gla_gate_kernel(h_ref, wgd_ref, w0_ref, w1_ref, b_ref, o_ref):
    gd = jnp.dot(h_ref[...], wgd_ref[...], preferred_element_type=F32)
    x0 = _dot_split(gd, w0_ref[...]) + b_ref[0:1, :]
    x1 = _dot_split(gd, w1_ref[...]) + b_ref[1:2, :]
    o_ref[:, :GLA_DK_TOT] = _log_sigmoid(x0) * (1.0 / GATE_TEMP)
    o_ref[:, GLA_DK_TOT:] = _log_sigmoid(x1) * (1.0 / GATE_TEMP)


def _gla_gates(h, w_gd, w0, w1, b_gate, tm=1024):
    return pl.pallas_call(
        _gla_gate_kernel,
        grid=(N_TOK // tm,),
        in_specs=[pl.BlockSpec((tm, D_MODEL), lambda i: (i, 0)),
                  pl.BlockSpec((D_MODEL, LANES), lambda i: (0, 0)),
                  pl.BlockSpec((LANES, GLA_DK_TOT), lambda i: (0, 0)),
                  pl.BlockSpec((LANES, GLA_DK_TOT), lambda i: (0, 0)),
                  pl.BlockSpec((2, GLA_DK_TOT), lambda i: (0, 0))],
        out_specs=pl.BlockSpec((tm, 2 * GLA_DK_TOT), lambda i: (i, 0)),
        out_shape=jax.ShapeDtypeStruct((N_TOK, 2 * GLA_DK_TOT), F32),
        compiler_params=_params("arbitrary"),
        name="gla_gates",
    )(h, w_gd, w0, w1, b_gate)


def _gla_keep_mask(causal):
    c = GLA_CHUNK
    row = lax.broadcasted_iota(jnp.int32, (c, c), 0)
    col = lax.broadcasted_iota(jnp.int32, (c, c), 1)
    return (col <= row) if causal else (col >= row)


def _gla_cum_decay(la_ref, causal):
    mask = _gla_keep_mask(causal).astype(BF16)
    la = la_ref[...]
    hi = la.astype(BF16)
    lo = (la - hi.astype(F32)).astype(BF16)
    return (jnp.dot(mask, hi, preferred_element_type=F32)
            + jnp.dot(mask, lo, preferred_element_type=F32))


def _gla_direction(q_ref, k_ref, v_ref, b, o_ref, s_ref, b_ref, d, causal, factorised):
    c = GLA_CHUNK
    keep = _gla_keep_mask(causal)
    end = c - 1 if causal else 0
    b_end = b[end:end + 1, :]
    q_dec = q_ref[...].astype(F32) * (GLA_DK ** -0.5) * jnp.exp(b)
    k_end = k_ref[...] * jnp.exp(b_end - b)
    e_end = jnp.exp(b_end)
    eye = (lax.broadcasted_iota(jnp.int32, (GLA_DK, GLA_DK), 0)
           == lax.broadcasted_iota(jnp.int32, (GLA_DK, GLA_DK), 1))
    if factorised:
        k_inv = k_ref[...] * jnp.exp(-b)
    else:
        b_ref[...] = b
        col = lax.broadcasted_iota(jnp.int32, (c, c), 1)
    for h in range(GLA_HEADS):
        ks = slice(h * GLA_DK, (h + 1) * GLA_DK)
        vs = slice(h * GLA_DV, (h + 1) * GLA_DV)
        state = s_ref[d, h]
        qh = q_dec[:, ks].astype(BF16)
        vh = v_ref[:, vs].astype(BF16)
        if factorised:
            scores = lax.dot_general(qh, k_inv[:, ks].astype(BF16), NT_DIMS,
                                     preferred_element_type=F32)
        else:
            def key_column(s, sc, ks=ks):
                base = pl.multiple_of((s // 16) * 16, 16)
                pick = lax.broadcasted_iota(jnp.int32, (16, GLA_DK), 0) == s % 16
                b_s = jnp.sum(jnp.where(pick, b_ref[pl.ds(base, 16), ks], 0.0),
                              axis=0, keepdims=True)
                k_s = jnp.sum(jnp.where(pick, k_ref[pl.ds(base, 16), ks].astype(F32), 0.0),
                              axis=0, keepdims=True)
                decay = jnp.exp(jnp.minimum(b_ref[:, ks] - b_s, 0.0))
                column = jnp.sum(q_ref[:, ks].astype(F32) * (GLA_DK ** -0.5) * k_s * decay,
                                 axis=1, keepdims=True)
                return jnp.where(col == s, column, sc)

            scores = lax.fori_loop(0, c, key_column, jnp.zeros((c, c), F32))
        scores = jnp.where(keep, scores, 0.0)
        o_ref[:, vs] = (jnp.dot(scores.astype(BF16), vh, preferred_element_type=F32)
                        + jnp.dot(qh, state.astype(BF16), preferred_element_type=F32)
                        ).astype(o_ref.dtype)
        e_col = jnp.sum(jnp.where(eye, jnp.broadcast_to(e_end[:, ks], (GLA_DK, GLA_DK)), 0.0),
                        axis=1, keepdims=True)
        s_ref[d, h] = state * e_col + lax.dot_general(
            k_end[:, ks].astype(BF16), vh, TN_DIMS, preferred_element_type=F32)


def _gla_scan_kernel(*refs, n_chunks, has_s0, write_state):
    qf, kf, vf, laf, qb, kb, vb, lab = refs[:8]
    rest = list(refs[8:])
    s0_ref = rest.pop(0) if has_s0 else None
    of_ref, ob_ref = rest.pop(0), rest.pop(0)
    sout_ref = rest.pop(0) if write_state else None
    s_ref, b_ref = rest
    i = pl.program_id(1)

    @pl.when(i == 0)
    def _():
        s_ref[...] = s0_ref[...] if has_s0 else jnp.zeros(s_ref.shape, F32)

    b_f = _gla_cum_decay(laf, True)
    b_b = _gla_cum_decay(lab, False)
    c = GLA_CHUNK
    total_decay = jnp.minimum(jnp.min(b_f[c - 1:c, :]), jnp.min(b_b[0:1, :]))
    factorisable = total_decay >= -GLA_MAX_FACTORISED_DECAY

    def step(factorised):
        _gla_direction(qf, kf, vf, b_f, of_ref, s_ref, b_ref.at[0], 0, True, factorised)
        _gla_direction(qb, kb, vb, b_b, ob_ref, s_ref, b_ref.at[1], 1, False, factorised)

    pl.when(factorisable)(functools.partial(step, True))
    pl.when(jnp.logical_not(factorisable))(functools.partial(step, False))

    if write_state:
        @pl.when(i == n_chunks - 1)
        def _():
            sout_ref[...] = s_ref[...]


def _gla_scan(proj, la, row0, n_batch, seq, s0):
    c = GLA_CHUNK
    n = seq // c
    base = row0 // c
    has_s0 = s0 is not None
    write_state = not has_s0
    fwd = lambda b, i: base + b * n + i
    bwd = lambda b, i: base + b * n + (n - 1 - i)

    def specs(blk):
        return [pl.BlockSpec((c, GLA_DK_TOT), lambda b, i: (blk(b, i), 0)),
                pl.BlockSpec((c, GLA_DK_TOT), lambda b, i: (blk(b, i), 1)),
                pl.BlockSpec((c, GLA_DV_TOT), lambda b, i: (blk(b, i), 1))]
    state_block = (None, 2, GLA_HEADS, GLA_DK, GLA_DV)
    state_spec = pl.BlockSpec(state_block, lambda b, i: (b, 0, 0, 0, 0))
    in_specs = (specs(fwd) + [pl.BlockSpec((c, GLA_DK_TOT), lambda b, i: (fwd(b, i), 0))]
                + specs(bwd) + [pl.BlockSpec((c, GLA_DK_TOT), lambda b, i: (bwd(b, i), 1))])
    args = [proj, proj, proj, la, proj, proj, proj, la]
    if has_s0:
        in_specs.append(state_spec)
        args.append(s0)
    out_rows = n_batch * seq
    out_specs = [pl.BlockSpec((c, GLA_DV_TOT), lambda b, i: (b * n + i, 0)),
                 pl.BlockSpec((c, GLA_DV_TOT), lambda b, i: (b * n + (n - 1 - i), 0))]
    out_shape = [jax.ShapeDtypeStruct((out_rows, GLA_DV_TOT), BF16)] * 2
    if write_state:
        out_specs.append(state_spec)
        out_shape.append(jax.ShapeDtypeStruct((n_batch, 2, GLA_HEADS, GLA_DK, GLA_DV), F32))
    return pl.pallas_call(
        functools.partial(_gla_scan_kernel, n_chunks=n, has_s0=has_s0, write_state=write_state),
        grid=(n_batch, n),
        in_specs=in_specs,
        out_specs=out_specs,
        out_shape=out_shape,
        scratch_shapes=[pltpu.VMEM((2, GLA_HEADS, GLA_DK, GLA_DV), F32),
                        pltpu.VMEM((2, c, GLA_DK_TOT), F32)],
        compiler_params=_params("arbitrary", "arbitrary"),
        name="gla_scan",
    )(*args)


def _gla_post_kernel(ofp_ref, ofs_ref, obp_ref, obs_ref, r_ref, g_ref, o_ref):
    tm = o_ref.shape[0]
    o = (_pick_prompt_sample(ofp_ref, ofs_ref, tm).astype(F32)
         + _pick_prompt_sample(obp_ref, obs_ref, tm).astype(F32))
    normed = jnp.concatenate(
        [_rms(o[:, h * GLA_DV:(h + 1) * GLA_DV]) for h in range(GLA_HEADS)], axis=1)
    o_ref[...] = (normed * g_ref[...] * _silu(r_ref[...].astype(F32))).astype(o_ref.dtype)


def _gla_post(of_p, of_s, ob_p, ob_s, proj, g_tiled, tm=1024):
    return pl.pallas_call(
        _gla_post_kernel,
        grid=(N_TOK // tm,),
        in_specs=(_prompt_sample_specs(tm, GLA_DV_TOT) + _prompt_sample_specs(tm, GLA_DV_TOT)
                  + [pl.BlockSpec((tm, GLA_DV_TOT), lambda i: (i, 2)),
                     pl.BlockSpec((1, GLA_DV_TOT), lambda i: (0, 0))]),
        out_specs=pl.BlockSpec((tm, GLA_DV_TOT), lambda i: (i, 0)),
        out_shape=jax.ShapeDtypeStruct((N_TOK, GLA_DV_TOT), BF16),
        compiler_params=_params("arbitrary"),
        name="gla_post",
    )(of_p, of_s, ob_p, ob_s, proj, g_tiled)


def _rope_pair(u, cs):
    t = u * cs
    return t + pltpu.roll(t, QK_ROPE, axis=1)


def _mla_down_kernel(a_ref, w_ref, qn_ref, kvn_ref, cs_ref, cq_ref, ckv_ref, kr_ref, krp_ref):
    acc = jnp.dot(a_ref[...], w_ref[...], preferred_element_type=F32)
    cq_ref[...] = (_rms(acc[:, :Q_LORA]) * qn_ref[...]).astype(cq_ref.dtype)
    ckv_ref[...] = _rms(acc[:, Q_LORA:Q_LORA + KV_LORA]) * kvn_ref[...]
    u = acc[:, Q_LORA + KV_LORA:]
    kr_ref[...] = u
    lane = lax.broadcasted_iota(jnp.int32, u.shape, 1)
    krp_ref[...] = jnp.where(lane < QK_ROPE, _rope_pair(u, cs_ref[...]), 0.0).astype(krp_ref.dtype)


def _mla_down(h, w_ext, q_norm, kv_norm, cs, tm=512):
    wn = w_ext.shape[1]
    row = lambda i: (i, 0)
    fixed = lambda i: (0, 0)
    return pl.pallas_call(
        _mla_down_kernel,
        grid=(N_TOK // tm,),
        in_specs=[pl.BlockSpec((tm, D_MODEL), row), pl.BlockSpec((D_MODEL, wn), fixed),
                  pl.BlockSpec((1, Q_LORA), fixed), pl.BlockSpec((1, KV_LORA), fixed),
                  pl.BlockSpec((tm, LANES), row)],
        out_specs=[pl.BlockSpec((tm, Q_LORA), row), pl.BlockSpec((tm, KV_LORA), row),
                   pl.BlockSpec((tm, LANES), row), pl.BlockSpec((tm, LANES), row)],
        out_shape=[jax.ShapeDtypeStruct((N_TOK, Q_LORA), BF16),
                   jax.ShapeDtypeStruct((N_TOK, KV_LORA), F32),
                   jax.ShapeDtypeStruct((N_TOK, LANES), F32),
                   jax.ShapeDtypeStruct((N_TOK, LANES), BF16)],
        compiler_params=_params("arbitrary"),
        name="mla_down",
    )(h, w_ext, q_norm.reshape(1, Q_LORA), kv_norm.reshape(1, KV_LORA), cs)


def _mla_q_kernel(a_ref, w_ref, cs_ref, o_ref):
    acc = jnp.dot(a_ref[...], w_ref[...], preferred_element_type=F32)
    cs = cs_ref[...]
    for h in range(MLA_HEADS):
        lo = h * 2 * LANES
        o_ref[:, lo:lo + LANES] = acc[:, lo:lo + LANES].astype(o_ref.dtype)
        o_ref[:, lo + LANES:lo + 2 * LANES] = _rope_pair(
            acc[:, lo + LANES:lo + 2 * LANES], cs).astype(o_ref.dtype)


def _mla_q(cq, w_ext, cs, tm=512):
    wn = w_ext.shape[1]
    return pl.pallas_call(
        _mla_q_kernel,
        grid=(N_TOK // tm,),
        in_specs=[pl.BlockSpec((tm, Q_LORA), lambda i: (i, 0)),
                  pl.BlockSpec((Q_LORA, wn), lambda i: (0, 0)),
                  pl.BlockSpec((tm, LANES), lambda i: (i, 0))],
        out_specs=pl.BlockSpec((tm, wn), lambda i: (i, 0)),
        out_shape=jax.ShapeDtypeStruct((N_TOK, wn), BF16),
        compiler_params=_params("arbitrary"),
        name="mla_q",
    )(cq, w_ext, cs)


def _attn_kernel(q_ref, *refs, heads):
    o_ref = refs[-1]
    sources = [(refs[i], refs[i + 1]) for i in range(0, len(refs) - 1, 2)]
    tq = q_ref.shape[0]
    c = MLA_SCALE * LOG2E
    hw = QK_NOPE + V_HEAD
    for h in range(heads):
        q = q_ref[:, h * hw:(h + 1) * hw]
        m = jnp.full((tq, 1), -jnp.inf, F32)
        l = jnp.zeros((tq, 1), F32)
        acc = jnp.zeros((tq, V_HEAD), F32)
        for kv_ref, kr_ref in sources:
            n_keys = kv_ref.shape[0]
            chunk = min(n_keys, ATTN_KEY_CHUNK)
            for c0 in range(0, n_keys, chunk):
                rows = slice(c0, c0 + chunk)
                k = jnp.concatenate([kv_ref[rows, h * hw:h * hw + QK_NOPE], kr_ref[rows, :]], axis=1)
                s = lax.dot_general(q, k, NT_DIMS, preferred_element_type=F32)
                m_new = jnp.maximum(m, jnp.max(s, axis=1, keepdims=True))
                alpha = jnp.exp2((m - m_new) * c)
                p = jnp.exp2((s - m_new) * c)
                l = alpha * l + jnp.sum(p, axis=1, keepdims=True)
                acc = alpha * acc + jnp.dot(p.astype(BF16),
                                            kv_ref[rows, h * hw + QK_NOPE:(h + 1) * hw],
                                            preferred_element_type=F32)
                m = m_new
        o_ref[:, h * V_HEAD:(h + 1) * V_HEAD] = (acc / l).astype(o_ref.dtype)


def _attention(q, sources, row0, n_batch, seq, tq, heads):
    nq = seq // tq
    base = row0 // tq
    hw = heads * (QK_NOPE + V_HEAD)
    in_specs = [pl.BlockSpec((tq, hw), lambda b, h, i: (base + b * nq + i, h))]
    args = [q]
    for kv, krp, first_row, n_keys in sources:
        first = first_row // n_keys
        in_specs += [pl.BlockSpec((n_keys, hw), lambda b, h, i, first=first: (first + b, h)),
                     pl.BlockSpec((n_keys, LANES), lambda b, h, i, first=first: (first + b, 0))]
        args += [kv, krp]
    return pl.pallas_call(
        functools.partial(_attn_kernel, heads=heads),
        grid=(n_batch, MLA_HEADS // heads, nq),
        in_specs=in_specs,
        out_specs=pl.BlockSpec((tq, heads * V_HEAD), lambda b, h, i: (b * nq + i, h)),
        out_shape=jax.ShapeDtypeStruct((n_batch * seq, MLA_HEADS * V_HEAD), BF16),
        compiler_params=_params("arbitrary", "arbitrary", "arbitrary"),
        name="mla_attention",
    )(*args)


def _rope_tables():
    half = QK_ROPE // 2
    pos = np.arange(DEC_SEQ)
    inv = ROPE_BASE ** (-np.arange(0, half, 2, dtype=np.float64) / half)
    ang = np.concatenate([(pos // GRID_W)[:, None] * inv, (pos % GRID_W)[:, None] * inv], axis=-1)
    ang = ang.astype(np.float32)
    cos = np.repeat(np.cos(ang), 2, axis=1)
    sin = np.repeat(np.sin(ang), 2, axis=1) * np.tile(np.array([-1.0, 1.0], np.float32), half)
    cs = np.concatenate([cos, sin], axis=1).astype(np.float32)
    ident = np.concatenate([np.ones((N_PROMPT, QK_ROPE), np.float32),
                            np.zeros((N_PROMPT, QK_ROPE), np.float32)], axis=1)
    return jnp.asarray(np.concatenate([ident, np.tile(cs, (DEC_BATCH, 1))], axis=0))


def kernel(x_prompt, x_sample, c, state_gla, cache_ckv, cache_krope, c_ctx, ada_w, ada_b, norm_mix, norm_ffn, norm_final, gla_w_in, gla_w_gate_up, gla_b_gate, gla_norm, gla_w_out, mla_w_down, mla_q_norm, mla_kv_norm, mla_w_uq, mla_w_ukv, mla_w_out, ffn_w_gate_up, ffn_w_down, moe_router, moe_w_gate_up, moe_w_down):
    x = (x_prompt.reshape(N_PROMPT, D_MODEL), x_sample.reshape(N_SAMPLE, D_MODEL))
    cond =jnp.concatenate([c_ctx[None], c, jnp.zeros((N_COND - 1 - DEC_BATCH, D_MODEL), F32)])
    mods_all = _ada_mods(cond, ada_w, ada_b).reshape(DEPTH, N_COND, 6, 1, D_MODEL)
    cs = _rope_tables()
    swap = np.arange(QK_ROPE) ^ 1

    new_gla, new_ckv, new_kr = [], [], []
    h = _modulate(x[0], x[1], norm_mix[0], mods_all[0], 0, 1)
    for layer in range(DEPTH):
        j = layer // 2
        mods = mods_all[layer]
        if layer % 2 == 0:
            w_in = gla_w_in[j].astype(BF16)
            proj = _matmul(h, w_in[:, :GLA_QKVR_W], BF16, tm=1024, tn=768)
            w_gd = jnp.pad(w_in[:, GLA_QKVR_W:], ((0, 0), (0, LANES - 2 * GATE_RANK)))
            pad_lo = jnp.zeros((GATE_RANK, GLA_DK_TOT), F32)
            pad_hi = jnp.zeros((LANES - 2 * GATE_RANK, GLA_DK_TOT), F32)
            w0 = jnp.concatenate([gla_w_gate_up[j, 0], pad_lo, pad_hi])
            w1 = jnp.concatenate([pad_lo, gla_w_gate_up[j, 1], pad_hi])
            la = _gla_gates(h, w_gd, w0, w1, gla_b_gate[j])
            of_p, ob_p, st = _gla_scan(proj, la, 0, BATCH, SEQ, None)
            of_s, ob_s = _gla_scan(proj, la, N_PROMPT, DEC_BATCH, DEC_SEQ, state_gla[:, j])
            new_gla.append(st)
            mixed = _gla_post(of_p, of_s, ob_p, ob_s, proj,
                              jnp.tile(gla_norm[j], GLA_HEADS).reshape(1, GLA_DV_TOT))
            x, h = _matmul_residual(mixed, gla_w_out[j].astype(BF16), x, mods, 2,
                                    (norm_ffn[layer], mods, 3, 4))
            act = _swiglu_up(h, ffn_w_gate_up[j].astype(BF16), tm=1024, tn=1408)
            x, h = _matmul_residual(act, ffn_w_down[j].astype(BF16), x, mods, 5,
                                    (norm_mix[layer + 1], mods_all[layer + 1], 0, 1))
        else:
            wd = mla_w_down[j]
            w_down = jnp.concatenate([wd, wd[:, Q_LORA + KV_LORA:][:, swap]], axis=1).astype(BF16)
            cq, ckv, kr, krp = _mla_down(h, w_down, mla_q_norm[j], mla_kv_norm[j], cs)
            wq = mla_w_uq[j].reshape(Q_LORA, MLA_HEADS, QK_NOPE + QK_ROPE)
            w_uq = jnp.concatenate([wq, wq[:, :, QK_NOPE:][:, :, swap]], axis=2)
            q = _mla_q(cq, w_uq.reshape(Q_LORA, -1).astype(BF16), cs)
            w_ukv = mla_w_ukv[j].astype(BF16)
            kv = _matmul(ckv.astype(BF16), w_ukv, BF16, tm=1024, tn=1024)
            ckv_ctx = cache_ckv[:, j].reshape(DEC_BATCH * PAST_LEN, KV_LORA).astype(BF16)
            kv_ctx = _matmul(ckv_ctx, w_ukv, BF16, tm=1024, tn=1024)
            kr_ctx = jnp.pad(cache_krope[:, j].reshape(DEC_BATCH * PAST_LEN, QK_ROPE),
                             ((0, 0), (0, LANES - QK_ROPE))).astype(BF16)
            o_p = _attention(q, [(kv, krp, 0, SEQ)], 0, BATCH, SEQ, tq=SEQ, heads=MLA_HEADS)
            o_s = _attention(q, [(kv_ctx, kr_ctx, 0, PAST_LEN), (kv, krp, N_PROMPT, DEC_SEQ)],
                             N_PROMPT, DEC_BATCH, DEC_SEQ, tq=1024, heads=1)
            new_ckv.append(ckv[:N_PROMPT].reshape(BATCH, SEQ, KV_LORA))
            new_kr.append(kr[:N_PROMPT, :QK_ROPE].reshape(BATCH, SEQ, QK_ROPE))
            x = _matmul_residual_split(o_p, o_s, mla_w_out[j].astype(BF16), x, mods, 2)

            router_pad = jnp.pad(moe_router[j], ((0, 0), (0, LANES - N_EXPERTS)))
            h, gates, slots, cnt = _modulate_router(x, norm_ffn[layer], mods, 3, 4, router_pad)
            counts = cnt[:, 0, :N_EXPERTS].astype(jnp.int32).reshape(-1)
            y = None
            for e in range(N_EXPERTS):
                y = _moe_expert(h, gates, slots, counts, moe_w_gate_up, moe_w_down, y, j, e)
            if layer + 1 < DEPTH:
                x, h = _gated_add(x, y, mods, 5, (norm_mix[layer + 1], mods_all[layer + 1], 0, 1))
            else:
                out_p = _gated_add_final(x, y, mods, 5, norm_final, 0, N_PROMPT)
                out_s = _gated_add_final(x, y, mods, 5, norm_final, N_PROMPT, N_SAMPLE)

    return (out_p.reshape(BATCH, SEQ, D_MODEL),
            out_s.reshape(DEC_BATCH, DEC_SEQ, D_MODEL),
            jnp.stack(new_gla, axis=1),
            jnp.stack(new_ckv, axis=1),
            jnp.stack(new_kr, axis=1))
```

```python
import functools

import jax
import jax.numpy as jnp
import numpy as np
from jax import lax
from jax.experimental import pallas as pl
from jax.experimental.pallas import tpu as pltpu

D_MODEL = 1024
BATCH = 16
SEQ = 256
DEPTH = 4
DEC_BATCH = 4
DEC_SEQ = 2048
PAST_LEN = 512
GRID_W = 64
GLA_HEADS = 4
GLA_DK = 128
GLA_DV = 256
GLA_DK_TOT = GLA_HEADS * GLA_DK
GLA_DV_TOT = GLA_HEADS * GLA_DV
GATE_RANK = 16
GATE_TEMP = 16.0
MLA_HEADS = 8
Q_LORA = 384
KV_LORA = 256
QK_NOPE = 128
QK_ROPE = 64
V_HEAD = 128
MLA_SCALE = (QK_NOPE + QK_ROPE) ** -0.5
ROPE_BASE = 10000.0
D_FF = 2816
N_EXPERTS = 8
D_EXPERT = 3584
EPS = 1e-6

N_PROMPT = BATCH * SEQ
N_SAMPLE = DEC_BATCH * DEC_SEQ
N_TOK = N_PROMPT + N_SAMPLE
N_COND = 8
LANES = 128
GLA_QKVR_W = 2 * GLA_DK_TOT + 2 * GLA_DV_TOT
GLA_CHUNK = 256
GLA_MAX_FACTORISED_DECAY = 80.0
ATTN_KEY_CHUNK = 512
MOE_BLOCK = 1024
MOE_TILE_UNIT = 64
MOE_W_CHUNKS = 14
LOG2E = 1.4426950408889634
VMEM_LIMIT = 56 * 1024 * 1024

F32 = jnp.float32
BF16 = jnp.bfloat16
NT_DIMS = (((1,), (1,)), ((), ()))
TN_DIMS = (((0,), (0,)), ((), ()))


def _params(*sem):
    return pltpu.CompilerParams(dimension_semantics=sem, vmem_limit_bytes=VMEM_LIMIT)


def _cond_row(i, tm):
    return jnp.where(i * tm < N_PROMPT, 0, 1 + jnp.maximum(i * tm - N_PROMPT, 0) // DEC_SEQ)


def _mod_spec(which, tm, tn=D_MODEL, col=None):
    if col is None:
        return pl.BlockSpec((None, None, 1, tn), lambda i, *_: (_cond_row(i, tm), which, 0, 0))
    return pl.BlockSpec((None, None, 1, tn), lambda i, j: (_cond_row(i, tm), which, 0, j))


def _silu(x):
    return x / (1.0 + jnp.exp(-x))


def _rms(x):
    return x * lax.rsqrt(jnp.mean(x * x, axis=-1, keepdims=True) + EPS)


def _ada_kernel(c_ref, w_ref, b_ref, o_ref):
    o_ref[...] = _dot_split(_silu(c_ref[...]), w_ref[...]) + b_ref[...]


def _ada_mods(cond, ada_w, ada_b):
    tn = 1536
    return pl.pallas_call(
        _ada_kernel,
        grid=(DEPTH, 6 * D_MODEL // tn),
        in_specs=[pl.BlockSpec((N_COND, D_MODEL), lambda l, j: (0, 0)),
                  pl.BlockSpec((None, D_MODEL, tn), lambda l, j: (l, 0, j)),
                  pl.BlockSpec((None, 1, tn), lambda l, j: (l, 0, j))],
        out_specs=pl.BlockSpec((None, N_COND, tn), lambda l, j: (l, 0, j)),
        out_shape=jax.ShapeDtypeStruct((DEPTH, N_COND, 6 * D_MODEL), F32),
        compiler_params=_params("arbitrary", "arbitrary"),
        name="ada_mods",
    )(cond, ada_w, ada_b.reshape(DEPTH, 1, 6 * D_MODEL))


def _prompt_sample_specs(tm, width):
    n_p = N_PROMPT // tm
    return [pl.BlockSpec((tm, width), lambda i, *_: (jnp.minimum(i, n_p - 1), 0)),
            pl.BlockSpec((tm, width), lambda i, *_: (jnp.maximum(i - n_p, 0), 0))]


def _pick_prompt_sample(p_ref, s_ref, tm):
    return jnp.where(pl.program_id(0) < N_PROMPT // tm, p_ref[...], s_ref[...])


def _modulate_kernel(xp_ref, xs_ref, g_ref, sh_ref, sc_ref, o_ref):
    y = _rms(_pick_prompt_sample(xp_ref, xs_ref, o_ref.shape[0])) * g_ref[...]
    o_ref[...] = (y * (1.0 + sc_ref[...]) + sh_ref[...]).astype(o_ref.dtype)


def _modulate(x_prompt, x_sample, g, mods, shift_idx, scale_idx, tm=1024):
    return pl.pallas_call(
        _modulate_kernel,
        grid=(N_TOK // tm,),
        in_specs=(_prompt_sample_specs(tm, D_MODEL)
                  + [pl.BlockSpec((1, D_MODEL), lambda i: (0, 0)),
                     _mod_spec(shift_idx, tm), _mod_spec(scale_idx, tm)]),
        out_specs=pl.BlockSpec((tm, D_MODEL), lambda i: (i, 0)),
        out_shape=jax.ShapeDtypeStruct((N_TOK, D_MODEL), BF16),
        compiler_params=_params("arbitrary"),
        name="modulate",
    )(x_prompt, x_sample, g.reshape(1, D_MODEL), mods, mods)


def _modulate_router_kernel(x_ref, g_ref, sh_ref, sc_ref, r_ref, o_ref, gates_ref, slot_ref, cnt_ref):
    y = _rms(x_ref[...]) * g_ref[...]
    h = y * (1.0 + sc_ref[...]) + sh_ref[...]
    o_ref[...] = h.astype(o_ref.dtype)
    logits = _dot_split(h, r_ref[...])
    lane = lax.broadcasted_iota(jnp.int32, logits.shape, 1)
    l1 = jnp.where(lane < N_EXPERTS, logits, -jnp.inf)
    m1 = jnp.max(l1, axis=1, keepdims=True)
    i1 = jnp.min(jnp.where(l1 == m1, lane, LANES), axis=1, keepdims=True)
    l2 = jnp.where(lane == i1, -jnp.inf, l1)
    m2 = jnp.max(l2, axis=1, keepdims=True)
    i2 = jnp.min(jnp.where(l2 == m2, lane, LANES), axis=1, keepdims=True)
    e = jnp.exp(m2 - m1)
    w1 = 1.0 / (1.0 + e)
    w2 = e / (1.0 + e)
    gates_ref[...] = jnp.where(lane == i1, w1, 0.0) + jnp.where(lane == i2, w2, 0.0)
    sel = (lane == i1) | (lane == i2)
    tb = logits.shape[0]
    earlier = (lax.broadcasted_iota(jnp.int32, (tb, tb), 1)
               < lax.broadcasted_iota(jnp.int32, (tb, tb), 0)).astype(BF16)
    before = jnp.dot(earlier, sel.astype(BF16), preferred_element_type=F32)
    slot_ref[...] = jnp.where(sel, before + 1.0, 0.0)
    cnt_ref[...] = jnp.broadcast_to(jnp.sum(sel.astype(F32), axis=0, keepdims=True), cnt_ref.shape)


def _modulate_router(x, g, mods, shift_idx, scale_idx, router_pad):
    tm = MOE_BLOCK
    nb = N_TOK // tm
    return pl.pallas_call(
        _modulate_router_kernel,
        grid=(nb,),
        in_specs=[pl.BlockSpec((tm, D_MODEL), lambda i: (i, 0)),
                  pl.BlockSpec((1, D_MODEL), lambda i: (0, 0)),
                  _mod_spec(shift_idx, tm), _mod_spec(scale_idx, tm),
                  pl.BlockSpec((D_MODEL, LANES), lambda i: (0, 0))],
        out_specs=[pl.BlockSpec((tm, D_MODEL), lambda i: (i, 0)),
                   pl.BlockSpec((tm, LANES), lambda i: (i, 0)),
                   pl.BlockSpec((tm, LANES), lambda i: (i, 0)),
                   pl.BlockSpec((None, 8, LANES), lambda i: (i, 0, 0))],
        out_shape=[jax.ShapeDtypeStruct((N_TOK, D_MODEL), BF16),
                   jax.ShapeDtypeStruct((N_TOK, LANES), F32),
                   jax.ShapeDtypeStruct((N_TOK, LANES), F32),
                   jax.ShapeDtypeStruct((nb, 8, LANES), F32)],
        compiler_params=_params("arbitrary"),
        name="modulate_router",
    )(x, g.reshape(1, D_MODEL), mods, mods, router_pad)


def _moe_expert_kernel(cnt_ref, h_ref, gates_ref, slot_ref, wgu_ref, wd_ref, *rest, expert, first):
    wgu_s, wd_s = rest[-2:]
    y_ref = rest[-3]
    step = pl.program_id(0)
    half = MOE_W_CHUNKS // 2

    wc = wgu_ref.shape[1]
    for c in range(MOE_W_CHUNKS):
        @pl.when(step == c)
        def _(c=c):
            wgu_s[:, c * wc:(c + 1) * wc] = wgu_ref[...].astype(BF16)

    @pl.when(step < half)
    def _():
        wd_s[pl.ds(pl.multiple_of(step * wc, wc), wc), :] = wd_ref[...].astype(BF16)

    @pl.when(step >= MOE_W_CHUNKS)
    def _():
        tb = h_ref.shape[0]
        gate_col = gates_ref[:, expert:expert + 1]
        slot_col = slot_ref[:, expert:expert + 1]
        count = cnt_ref[(step - MOE_W_CHUNKS) * N_EXPERTS + expert]
        cw = 256
        fh = D_EXPERT // 2

        def y_before(cols):
            return jnp.zeros((tb, cw), F32) if first else rest[0][:, cols]

        def tile(first_slot, rows, fresh):
            lane = lax.broadcasted_iota(jnp.int32, (tb, rows), 1).astype(F32)
            onehot = (slot_col == lane + (first_slot + 1).astype(F32)).astype(BF16)
            hg = lax.dot_general(onehot, h_ref[...], TN_DIMS,
                                 preferred_element_type=F32).astype(BF16)
            out = None
            for f0 in (0, fh):
                g = jnp.dot(hg, wgu_s[:, f0:f0 + fh], preferred_element_type=F32)
                u = jnp.dot(hg, wgu_s[:, D_EXPERT + f0:D_EXPERT + f0 + fh],
                            preferred_element_type=F32)
                part = jnp.dot((_silu(g) * u).astype(BF16), wd_s[f0:f0 + fh, :],
                               preferred_element_type=F32)
                out = part if out is None else out + part
            out = out.astype(BF16)
            for c0 in range(0, D_MODEL, cw):
                cols = slice(c0, c0 + cw)
                prev = jnp.where(fresh, y_before(cols), y_ref[:, cols])
                y_ref[:, cols] = prev + gate_col * jnp.dot(onehot, out[:, cols],
                                                           preferred_element_type=F32)

        unit = MOE_TILE_UNIT
        units = (count + (unit - 1)) // unit
        n_big = units // 4
        left = units - 4 * n_big

        def big_tile(k, carry):
            tile(k * (4 * unit), 4 * unit, k == 0)
            return carry

        lax.fori_loop(0, n_big, big_tile, 0)

        @pl.when(jnp.logical_and(left >= 1, left <= 2))
        def _():
            tile(n_big * (4 * unit), 2 * unit, n_big == 0)

        @pl.when(left == 3)
        def _():
            tile(n_big * (4 * unit), 3 * unit, n_big == 0)

        @pl.when(count == 0)
        def _():
            for c0 in range(0, D_MODEL, cw):
                y_ref[:, c0:c0 + cw] = y_before(slice(c0, c0 + cw))


def _moe_expert(h, gates, slots, counts, w_gu, w_down, y, layer, expert):
    tb = MOE_BLOCK
    nc = MOE_W_CHUNKS
    wc = 2 * D_EXPERT // nc
    first = y is None
    blk = lambda s, cnt: (jnp.maximum(s - nc, 0), 0)
    in_specs = [pl.BlockSpec((tb, D_MODEL), blk), pl.BlockSpec((tb, LANES), blk),
                pl.BlockSpec((tb, LANES), blk),
                pl.BlockSpec((None, None, D_MODEL, wc),
                             lambda s, cnt: (layer, expert, 0, jnp.minimum(s, nc - 1))),
                pl.BlockSpec((None, None, wc, D_MODEL),
                             lambda s, cnt: (layer, expert, jnp.minimum(s, nc // 2 - 1), 0))]
    args = [h, gates, slots, w_gu, w_down]
    if not first:
        in_specs.append(pl.BlockSpec((tb, D_MODEL), blk))
        args.append(y)
    return pl.pallas_call(
        functools.partial(_moe_expert_kernel, expert=expert, first=first),
        grid_spec=pltpu.PrefetchScalarGridSpec(
            num_scalar_prefetch=1, grid=(nc + N_TOK // tb,), in_specs=in_specs,
            out_specs=pl.BlockSpec((tb, D_MODEL), blk),
            scratch_shapes=[pltpu.VMEM((D_MODEL, 2 * D_EXPERT), BF16),
                            pltpu.VMEM((D_EXPERT, D_MODEL), BF16)]),
        out_shape=jax.ShapeDtypeStruct((N_TOK, D_MODEL), F32),
        compiler_params=_params("arbitrary"),
        name="moe_expert",
    )(counts, *args)


def _mm_kernel(a_ref, w_ref, o_ref):
    o_ref[...] = jnp.dot(a_ref[...], w_ref[...], preferred_element_type=F32).astype(o_ref.dtype)


def _matmul(a, w, out_dtype, tm, tn):
    m, k = a.shape
    n = w.shape[1]
    return pl.pallas_call(
        _mm_kernel,
        grid=(m // tm, n // tn),
        in_specs=[pl.BlockSpec((tm, k), lambda i, j: (i, 0)),
                  pl.BlockSpec((k, tn), lambda i, j: (0, j))],
        out_specs=pl.BlockSpec((tm, tn), lambda i, j: (i, j)),
        out_shape=jax.ShapeDtypeStruct((m, n), out_dtype),
        compiler_params=_params("arbitrary", "arbitrary"),
        name="matmul",
    )(a, w)


def _swiglu_kernel(a_ref, wg_ref, wu_ref, o_ref):
    a = a_ref[...]
    g = jnp.dot(a, wg_ref[...], preferred_element_type=F32)
    u = jnp.dot(a, wu_ref[...], preferred_element_type=F32)
    o_ref[...] = (_silu(g) * u).astype(o_ref.dtype)


def _swiglu_up(a, w_gu, tm, tn):
    m, k = a.shape
    f = w_gu.shape[1] // 2
    nj = f // tn
    return pl.pallas_call(
        _swiglu_kernel,
        grid=(m // tm, nj),
        in_specs=[pl.BlockSpec((tm, k), lambda i, j: (i, 0)),
                  pl.BlockSpec((k, tn), lambda i, j: (0, j)),
                  pl.BlockSpec((k, tn), lambda i, j: (0, j + nj))],
        out_specs=pl.BlockSpec((tm, tn), lambda i, j: (i, j)),
        out_shape=jax.ShapeDtypeStruct((m, f), BF16),
        compiler_params=_params("arbitrary", "arbitrary"),
        name="swiglu_up",
    )(a, w_gu, w_gu)


def _next_norm_specs(next_norm, tm):
    gain, mods, shift_idx, scale_idx = next_norm
    specs = [pl.BlockSpec((1, D_MODEL), lambda i, *_: (0, 0)),
             _mod_spec(shift_idx, tm), _mod_spec(scale_idx, tm)]
    return specs, [gain.reshape(1, D_MODEL), mods, mods]


def _mm_residual_norm_kernel(a_ref, w_ref, *refs, x_pair):
    if x_pair:
        x_in = _pick_prompt_sample(refs[0], refs[1], a_ref.shape[0])
        refs = refs[2:]
    else:
        x_in = refs[0][...]
        refs = refs[1:]
    g_ref, ng_ref, nsh_ref, nsc_ref, o_ref, h_ref = refs
    x = x_in + g_ref[...] * jnp.dot(a_ref[...], w_ref[...], preferred_element_type=F32)
    o_ref[...] = x
    h_ref[...] = (_rms(x) * ng_ref[...] * (1.0 + nsc_ref[...]) + nsh_ref[...]).astype(h_ref.dtype)


def _matmul_residual(a, w, x, mods, gate_idx, next_norm, tm=512):
    k = a.shape[1]
    norm_specs, norm_args = _next_norm_specs(next_norm, tm)
    row = lambda i: (i, 0)
    x_pair = isinstance(x, tuple)
    x_specs = _prompt_sample_specs(tm, D_MODEL) if x_pair else [pl.BlockSpec((tm, D_MODEL), row)]
    x_args = list(x) if x_pair else [x]
    return pl.pallas_call(
        functools.partial(_mm_residual_norm_kernel, x_pair=x_pair),
        grid=(N_TOK // tm,),
        in_specs=([pl.BlockSpec((tm, k), row), pl.BlockSpec((k, D_MODEL), lambda i: (0, 0))]
                  + x_specs + [_mod_spec(gate_idx, tm)] + norm_specs),
        out_specs=[pl.BlockSpec((tm, D_MODEL), row), pl.BlockSpec((tm, D_MODEL), row)],
        out_shape=[jax.ShapeDtypeStruct((N_TOK, D_MODEL), F32),
                   jax.ShapeDtypeStruct((N_TOK, D_MODEL), BF16)],
        compiler_params=_params("arbitrary"),
        name="matmul_residual",
    )(a, w, *x_args, mods, *norm_args)


def _mm_residual_split_kernel(ap_ref, as_ref, w_ref, x_ref, g_ref, o_ref):
    a = _pick_prompt_sample(ap_ref, as_ref, o_ref.shape[0])
    o_ref[...] = x_ref[...] + g_ref[...] * jnp.dot(a, w_ref[...], preferred_element_type=F32)


def _matmul_residual_split(a_prompt, a_sample, w, x, mods, gate_idx, tm=1024, tn=512):
    k = w.shape[0]
    return pl.pallas_call(
        _mm_residual_split_kernel,
        grid=(N_TOK // tm, D_MODEL // tn),
        in_specs=(_prompt_sample_specs(tm, k)
                  + [pl.BlockSpec((k, tn), lambda i, j: (0, j)),
                     pl.BlockSpec((tm, tn), lambda i, j: (i, j)),
                     _mod_spec(gate_idx, tm, tn, col=True)]),
        out_specs=pl.BlockSpec((tm, tn), lambda i, j: (i, j)),
        out_shape=jax.ShapeDtypeStruct((N_TOK, D_MODEL), F32),
        compiler_params=_params("arbitrary", "arbitrary"),
        name="matmul_residual_split",
    )(a_prompt, a_sample, w, x, mods)


def _gated_add_norm_kernel(x_ref, y_ref, g_ref, ng_ref, nsh_ref, nsc_ref, o_ref, h_ref):
    x = x_ref[...] + g_ref[...] * y_ref[...]
    o_ref[...] = x
    h_ref[...] = (_rms(x) * ng_ref[...] * (1.0 + nsc_ref[...]) + nsh_ref[...]).astype(h_ref.dtype)


def _gated_add(x, y, mods, gate_idx, next_norm, tm=1024):
    norm_specs, norm_args = _next_norm_specs(next_norm, tm)
    row = lambda i: (i, 0)
    return pl.pallas_call(
        _gated_add_norm_kernel,
        grid=(N_TOK // tm,),
        in_specs=[pl.BlockSpec((tm, D_MODEL), row), pl.BlockSpec((tm, D_MODEL), row),
                  _mod_spec(gate_idx, tm)] + norm_specs,
        out_specs=[pl.BlockSpec((tm, D_MODEL), row), pl.BlockSpec((tm, D_MODEL), row)],
        out_shape=[jax.ShapeDtypeStruct((N_TOK, D_MODEL), F32),
                   jax.ShapeDtypeStruct((N_TOK, D_MODEL), BF16)],
        compiler_params=_params("arbitrary"),
        name="gated_add",
    )(x, y, mods, *norm_args)


def _gated_add_final_kernel(x_ref, y_ref, g_ref, ng_ref, o_ref):
    o_ref[...] = _rms(x_ref[...] + g_ref[...] * y_ref[...]) * ng_ref[...]


def _gated_add_final(x, y, mods, gate_idx, gain, row0, n_rows, tm=1024):
    base = row0 // tm
    row = lambda i: (base + i, 0)
    return pl.pallas_call(
        _gated_add_final_kernel,
        grid=(n_rows // tm,),
        in_specs=[pl.BlockSpec((tm, D_MODEL), row), pl.BlockSpec((tm, D_MODEL), row),
                  pl.BlockSpec((None, None, 1, D_MODEL),
                               lambda i: (_cond_row(base + i, tm), gate_idx, 0, 0)),
                  pl.BlockSpec((1, D_MODEL), lambda i: (0, 0))],
        out_specs=pl.BlockSpec((tm, D_MODEL), lambda i: (i, 0)),
        out_shape=jax.ShapeDtypeStruct((n_rows, D_MODEL), F32),
        compiler_params=_params("arbitrary"),
        name="gated_add_final",
    )(x, y, mods, gain.reshape(1, D_MODEL))


def _log_sigmoid(x):
    return jnp.minimum(x, 0.0) - jnp.log(1.0 + jnp.exp(-jnp.abs(x)))


def _split_bf16(x):
    hi = x.astype(BF16)
    return hi, (x - hi.astype(F32)).astype(BF16)


def _dot_split(a, b):
    a_hi, a_lo = _split_bf16(a)
    b_hi, b_lo = _split_bf16(b)
    return (jnp.dot(a_hi, b_hi, preferred_element_type=F32)
            + jnp.dot(a_lo, b_hi, preferred_element_type=F32)
            + jnp.dot(a_hi, b_lo, preferred_element_type=F32))


def _gla_proj_kernel(h_ref, w_ref, wgd_ref, w0_ref, w1_ref, b_ref, proj_ref, la_ref):
    h = h_ref[...]
    proj_ref[...] = jnp.dot(h, w_ref[...], preferred_element_type=F32).astype(proj_ref.dtype)

    @pl.when(pl.program_id(1) == 0)
    def _():
        gd = jnp.dot(h, wgd_ref[...], preferred_element_type=F32)
        x0 = _dot_split(gd, w0_ref[...]) + b_ref[0:1, :]
        x1 = _dot_split(gd, w1_ref[...]) + b_ref[1:2, :]
        la_ref[:, :GLA_DK_TOT] = _log_sigmoid(x0) * (1.0 / GATE_TEMP)
        la_ref[:, GLA_DK_TOT:] = _log_sigmoid(x1) * (1.0 / GATE_TEMP)


def _gla_proj(h, w_qkvr, w_gd, w0, w1, b_gate, tm=1024, tn=768):
    fixed = lambda i, j: (0, 0)
    return pl.pallas_call(
        _gla_proj_kernel,
        grid=(N_TOK // tm, GLA_QKVR_W // tn),
        in_specs=[pl.BlockSpec((tm, D_MODEL), lambda i, j: (i, 0)),
                  pl.BlockSpec((D_MODEL, tn), lambda i, j: (0, j)),
                  pl.BlockSpec((D_MODEL, LANES), fixed),
                  pl.BlockSpec((LANES, GLA_DK_TOT), fixed),
                  pl.BlockSpec((LANES, GLA_DK_TOT), fixed),
                  pl.BlockSpec((2, GLA_DK_TOT), fixed)],
        out_specs=[pl.BlockSpec((tm, tn), lambda i, j: (i, j)),
                   pl.BlockSpec((tm, 2 * GLA_DK_TOT), lambda i, j: (i, 0))],
        out_shape=[jax.ShapeDtypeStruct((N_TOK, GLA_QKVR_W), BF16),
                   jax.ShapeDtypeStruct((N_TOK, 2 * GLA_DK_TOT), F32)],
        compiler_params=_params("arbitrary", "arbitrary"),
        name="gla_proj",
    )(h, w_qkvr, w_gd, w0, w1, b_gate)


def _gla_keep_mask(causal):
    c = GLA_CHUNK
    row = lax.broadcasted_iota(jnp.int32, (c, c), 0)
    col = lax.broadcasted_iota(jnp.int32, (c, c), 1)
    return (col <= row) if causal else (col >= row)


def _gla_cum_decay(la_ref, causal):
    mask = _gla_keep_mask(causal).astype(BF16)
    la = la_ref[...]
    hi = la.astype(BF16)
    lo = (la - hi.astype(F32)).astype(BF16)
    return (jnp.dot(mask, hi, preferred_element_type=F32)
            + jnp.dot(mask, lo, preferred_element_type=F32))


def _gla_direction(q_ref, k_ref, v_ref, b, o_ref, s_ref, b_ref, d, causal, factorised):
    c = GLA_CHUNK
    keep = _gla_keep_mask(causal)
    end = c - 1 if causal else 0
    b_end = b[end:end + 1, :]
    q_dec = q_ref[...].astype(F32) * (GLA_DK ** -0.5) * jnp.exp(b)
    k_end = k_ref[...] * jnp.exp(b_end - b)
    e_end = jnp.exp(b_end)
    eye = (lax.broadcasted_iota(jnp.int32, (GLA_DK, GLA_DK), 0)
           == lax.broadcasted_iota(jnp.int32, (GLA_DK, GLA_DK), 1))
    if factorised:
        k_inv = k_ref[...] * jnp.exp(-b)
    else:
        b_ref[...] = b
        col = lax.broadcasted_iota(jnp.int32, (c, c), 1)
    for h in range(GLA_HEADS):
        ks = slice(h * GLA_DK, (h + 1) * GLA_DK)
        vs = slice(h * GLA_DV, (h + 1) * GLA_DV)
        state = s_ref[d, h]
        qh = q_dec[:, ks].astype(BF16)
        vh = v_ref[:, vs].astype(BF16)
        if factorised:
            scores = lax.dot_general(qh, k_inv[:, ks].astype(BF16), NT_DIMS,
                                     preferred_element_type=F32)
        else:
            def key_column(s, sc, ks=ks):
                base = pl.multiple_of((s // 16) * 16, 16)
                pick = lax.broadcasted_iota(jnp.int32, (16, GLA_DK), 0) == s % 16
                b_s = jnp.sum(jnp.where(pick, b_ref[pl.ds(base, 16), ks], 0.0),
                              axis=0, keepdims=True)
                k_s = jnp.sum(jnp.where(pick, k_ref[pl.ds(base, 16), ks].astype(F32), 0.0),
                              axis=0, keepdims=True)
                decay = jnp.exp(jnp.minimum(b_ref[:, ks] - b_s, 0.0))
                column = jnp.sum(q_ref[:, ks].astype(F32) * (GLA_DK ** -0.5) * k_s * decay,
                                 axis=1, keepdims=True)
                return jnp.where(col == s, column, sc)

            scores = lax.fori_loop(0, c, key_column, jnp.zeros((c, c), F32))
        scores = jnp.where(keep, scores, 0.0)
        o_ref[:, vs] = (jnp.dot(scores.astype(BF16), vh, preferred_element_type=F32)
                        + jnp.dot(qh, state.astype(BF16), preferred_element_type=F32)
                        ).astype(o_ref.dtype)
        e_col = jnp.sum(jnp.where(eye, jnp.broadcast_to(e_end[:, ks], (GLA_DK, GLA_DK)), 0.0),
                        axis=1, keepdims=True)
        s_ref[d, h] = state * e_col + lax.dot_general(
            k_end[:, ks].astype(BF16), vh, TN_DIMS, preferred_element_type=F32)


def _gla_scan_kernel(*refs, n_chunks, has_s0, write_state):
    qf, kf, vf, laf, qb, kb, vb, lab = refs[:8]
    rest = list(refs[8:])
    s0_ref = rest.pop(0) if has_s0 else None
    of_ref, ob_ref = rest.pop(0), rest.pop(0)
    sout_ref = rest.pop(0) if write_state else None
    s_ref, b_ref = rest
    i = pl.program_id(1)

    @pl.when(i == 0)
    def _():
        s_ref[...] = s0_ref[...] if has_s0 else jnp.zeros(s_ref.shape, F32)

    b_f = _gla_cum_decay(laf, True)
    b_b = _gla_cum_decay(lab, False)
    c = GLA_CHUNK
    total_decay = jnp.minimum(jnp.min(b_f[c - 1:c, :]), jnp.min(b_b[0:1, :]))
    factorisable = total_decay >= -GLA_MAX_FACTORISED_DECAY

    def step(factorised):
        _gla_direction(qf, kf, vf, b_f, of_ref, s_ref, b_ref.at[0], 0, True, factorised)
        _gla_direction(qb, kb, vb, b_b, ob_ref, s_ref, b_ref.at[1], 1, False, factorised)

    pl.when(factorisable)(functools.partial(step, True))
    pl.when(jnp.logical_not(factorisable))(functools.partial(step, False))

    if write_state:
        @pl.when(i == n_chunks - 1)
        def _():
            sout_ref[...] = s_ref[...]


def _gla_scan(proj, la, row0, n_batch, seq, s0):
    c = GLA_CHUNK
    n = seq // c
    base = row0 // c
    has_s0 = s0 is not None
    write_state = not has_s0
    fwd = lambda b, i: base + b * n + i
    bwd = lambda b, i: base + b * n + (n - 1 - i)

    def specs(blk):
        return [pl.BlockSpec((c, GLA_DK_TOT), lambda b, i: (blk(b, i), 0)),
                pl.BlockSpec((c, GLA_DK_TOT), lambda b, i: (blk(b, i), 1)),
                pl.BlockSpec((c, GLA_DV_TOT), lambda b, i: (blk(b, i), 1))]
    state_block = (None, 2, GLA_HEADS, GLA_DK, GLA_DV)
    state_spec = pl.BlockSpec(state_block, lambda b, i: (b, 0, 0, 0, 0))
    in_specs = (specs(fwd) + [pl.BlockSpec((c, GLA_DK_TOT), lambda b, i: (fwd(b, i), 0))]
                + specs(bwd) + [pl.BlockSpec((c, GLA_DK_TOT), lambda b, i: (bwd(b, i), 1))])
    args = [proj, proj, proj, la, proj, proj, proj, la]
    if has_s0:
        in_specs.append(state_spec)
        args.append(s0)
    out_rows = n_batch * seq
    out_specs = [pl.BlockSpec((c, GLA_DV_TOT), lambda b, i: (b * n + i, 0)),
                 pl.BlockSpec((c, GLA_DV_TOT), lambda b, i: (b * n + (n - 1 - i), 0))]
    out_shape = [jax.ShapeDtypeStruct((out_rows, GLA_DV_TOT), BF16)] * 2
    if write_state:
        out_specs.append(state_spec)
        out_shape.append(jax.ShapeDtypeStruct((n_batch, 2, GLA_HEADS, GLA_DK, GLA_DV), F32))
    return pl.pallas_call(
        functools.partial(_gla_scan_kernel, n_chunks=n, has_s0=has_s0, write_state=write_state),
        grid=(n_batch, n),
        in_specs=in_specs,
        out_specs=out_specs,
        out_shape=out_shape,
        scratch_shapes=[pltpu.VMEM((2, GLA_HEADS, GLA_DK, GLA_DV), F32),
                        pltpu.VMEM((2, c, GLA_DK_TOT), F32)],
        compiler_params=_params("arbitrary", "arbitrary"),
        name="gla_scan",
    )(*args)


def _gla_post_kernel(ofp_ref, ofs_ref, obp_ref, obs_ref, r_ref, g_ref, o_ref):
    tm = o_ref.shape[0]
    o = (_pick_prompt_sample(ofp_ref, ofs_ref, tm).astype(F32)
         + _pick_prompt_sample(obp_ref, obs_ref, tm).astype(F32))
    normed = jnp.concatenate(
        [_rms(o[:, h * GLA_DV:(h + 1) * GLA_DV]) for h in range(GLA_HEADS)], axis=1)
    o_ref[...] = (normed * g_ref[...] * _silu(r_ref[...].astype(F32))).astype(o_ref.dtype)


def _gla_post(of_p, of_s, ob_p, ob_s, proj, g_tiled, tm=1024):
    return pl.pallas_call(
        _gla_post_kernel,
        grid=(N_TOK // tm,),
        in_specs=(_prompt_sample_specs(tm, GLA_DV_TOT) + _prompt_sample_specs(tm, GLA_DV_TOT)
                  + [pl.BlockSpec((tm, GLA_DV_TOT), lambda i: (i, 2)),
                     pl.BlockSpec((1, GLA_DV_TOT), lambda i: (0, 0))]),
        out_specs=pl.BlockSpec((tm, GLA_DV_TOT), lambda i: (i, 0)),
        out_shape=jax.ShapeDtypeStruct((N_TOK, GLA_DV_TOT), BF16),
        compiler_params=_params("arbitrary"),
        name="gla_post",
    )(of_p, of_s, ob_p, ob_s, proj, g_tiled)


def _rope_pair(u, cs):
    t = u * cs
    return t + pltpu.roll(t, QK_ROPE, axis=1)


def _mla_down_kernel(a_ref, w_ref, qn_ref, kvn_ref, cs_ref, wq_ref, q_ref, ckv_ref, kr_ref, krp_ref):
    acc = jnp.dot(a_ref[...], w_ref[...], preferred_element_type=F32)
    cs = cs_ref[...]
    ckv_ref[...] = _rms(acc[:, Q_LORA:Q_LORA + KV_LORA]) * kvn_ref[...]
    u = acc[:, Q_LORA + KV_LORA:]
    kr_ref[...] = u
    lane = lax.broadcasted_iota(jnp.int32, u.shape, 1)
    krp_ref[...] = jnp.where(lane < QK_ROPE, _rope_pair(u, cs), 0.0).astype(krp_ref.dtype)
    cq = (_rms(acc[:, :Q_LORA]) * qn_ref[...]).astype(BF16)
    q = jnp.dot(cq, wq_ref[...], preferred_element_type=F32)
    for h in range(MLA_HEADS):
        lo = h * 2 * LANES
        q_ref[:, lo:lo + LANES] = q[:, lo:lo + LANES].astype(q_ref.dtype)
        q_ref[:, lo + LANES:lo + 2 * LANES] = _rope_pair(
            q[:, lo + LANES:lo + 2 * LANES], cs).astype(q_ref.dtype)


def _mla_down(h, w_ext, q_norm, kv_norm, cs, wq_ext, tm=512):
    wn = w_ext.shape[1]
    qn = wq_ext.shape[1]
    row = lambda i: (i, 0)
    fixed = lambda i: (0, 0)
    return pl.pallas_call(
        _mla_down_kernel,
        grid=(N_TOK // tm,),
        in_specs=[pl.BlockSpec((tm, D_MODEL), row), pl.BlockSpec((D_MODEL, wn), fixed),
                  pl.BlockSpec((1, Q_LORA), fixed), pl.BlockSpec((1, KV_LORA), fixed),
                  pl.BlockSpec((tm, LANES), row), pl.BlockSpec((Q_LORA, qn), fixed)],
        out_specs=[pl.BlockSpec((tm, qn), row), pl.BlockSpec((tm, KV_LORA), row),
                   pl.BlockSpec((tm, LANES), row), pl.BlockSpec((tm, LANES), row)],
        out_shape=[jax.ShapeDtypeStruct((N_TOK, qn), BF16),
                   jax.ShapeDtypeStruct((N_TOK, KV_LORA), F32),
                   jax.ShapeDtypeStruct((N_TOK, LANES), F32),
                   jax.ShapeDtypeStruct((N_TOK, LANES), BF16)],
        compiler_params=_params("arbitrary"),
        name="mla_down",
    )(h, w_ext, q_norm.reshape(1, Q_LORA), kv_norm.reshape(1, KV_LORA), cs, wq_ext)


def _attn_kernel(q_ref, *refs, heads):
    o_ref = refs[-1]
    sources = [(refs[i], refs[i + 1]) for i in range(0, len(refs) - 1, 2)]
    tq = q_ref.shape[0]
    c = MLA_SCALE * LOG2E
    hw = QK_NOPE + V_HEAD
    for h in range(heads):
        q = q_ref[:, h * hw:(h + 1) * hw]
        m = jnp.full((tq, 1), -jnp.inf, F32)
        l = jnp.zeros((tq, 1), F32)
        acc = jnp.zeros((tq, V_HEAD), F32)
        for kv_ref, kr_ref in sources:
            n_keys = kv_ref.shape[0]
            chunk = min(n_keys, ATTN_KEY_CHUNK)
            for c0 in range(0, n_keys, chunk):
                rows = slice(c0, c0 + chunk)
                k = jnp.concatenate([kv_ref[rows, h * hw:h * hw + QK_NOPE], kr_ref[rows, :]], axis=1)
                s = lax.dot_general(q, k, NT_DIMS, preferred_element_type=F32)
                m_new = jnp.maximum(m, jnp.max(s, axis=1, keepdims=True))
                alpha = jnp.exp2((m - m_new) * c)
                p = jnp.exp2((s - m_new) * c)
                l = alpha * l + jnp.sum(p, axis=1, keepdims=True)
                acc = alpha * acc + jnp.dot(p.astype(BF16),
                                            kv_ref[rows, h * hw + QK_NOPE:(h + 1) * hw],
                                            preferred_element_type=F32)
                m = m_new
        o_ref[:, h * V_HEAD:(h + 1) * V_HEAD] = (acc / l).astype(o_ref.dtype)


def _attention(q, sources, row0, n_batch, seq, tq, heads):
    nq = seq // tq
    base = row0 // tq
    hw = heads * (QK_NOPE + V_HEAD)
    in_specs = [pl.BlockSpec((tq, hw), lambda b, h, i: (base + b * nq + i, h))]
    args = [q]
    for kv, krp, first_row, n_keys in sources:
        first = first_row // n_keys
        in_specs += [pl.BlockSpec((n_keys, hw), lambda b, h, i, first=first: (first + b, h)),
                     pl.BlockSpec((n_keys, LANES), lambda b, h, i, first=first: (first + b, 0))]
        args += [kv, krp]
    return pl.pallas_call(
        functools.partial(_attn_kernel, heads=heads),
        grid=(n_batch, MLA_HEADS // heads, nq),
        in_specs=in_specs,
        out_specs=pl.BlockSpec((tq, heads * V_HEAD), lambda b, h, i: (b * nq + i, h)),
        out_shape=jax.ShapeDtypeStruct((n_batch * seq, MLA_HEADS * V_HEAD), BF16),
        compiler_params=_params("arbitrary", "arbitrary", "arbitrary"),
        name="mla_attention",
    )(*args)


def _rope_tables():
    half = QK_ROPE // 2
    pos = np.arange(DEC_SEQ)
    inv = ROPE_BASE ** (-np.arange(0, half, 2, dtype=np.float64) / half)
    ang = np.concatenate([(pos // GRID_W)[:, None] * inv, (pos % GRID_W)[:, None] * inv], axis=-1)
    ang = ang.astype(np.float32)
    cos = np.repeat(np.cos(ang), 2, axis=1)
    sin = np.repeat(np.sin(ang), 2, axis=1) * np.tile(np.array([-1.0, 1.0], np.float32), half)
    cs = np.concatenate([cos, sin], axis=1).astype(np.float32)
    ident = np.concatenate([np.ones((N_PROMPT, QK_ROPE), np.float32),
                            np.zeros((N_PROMPT, QK_ROPE), np.float32)], axis=1)
    return jnp.asarray(np.concatenate([ident, np.tile(cs, (DEC_BATCH, 1))], axis=0))


def kernel(x_prompt, x_sample, c, state_gla, cache_ckv, cache_krope, c_ctx, ada_w, ada_b, norm_mix, norm_ffn, norm_final, gla_w_in, gla_w_gate_up, gla_b_gate, gla_norm, gla_w_out, mla_w_down, mla_q_norm, mla_kv_norm, mla_w_uq, mla_w_ukv, mla_w_out, ffn_w_gate_up, ffn_w_down, moe_router, moe_w_gate_up, moe_w_down):
    x = (x_prompt.reshape(N_PROMPT, D_MODEL), x_sample.reshape(N_SAMPLE, D_MODEL))
    cond =jnp.concatenate([c_ctx[None], c, jnp.zeros((N_COND - 1 - DEC_BATCH, D_MODEL), F32)])
    mods_all = _ada_mods(cond, ada_w, ada_b).reshape(DEPTH, N_COND, 6, 1, D_MODEL)
    cs = _rope_tables()
    swap = np.arange(QK_ROPE) ^ 1

    new_gla, new_ckv, new_kr = [], [], []
    h = _modulate(x[0], x[1], norm_mix[0], mods_all[0], 0, 1)
    for layer in range(DEPTH):
        j = layer // 2
        mods = mods_all[layer]
        if layer % 2 == 0:
            w_in = gla_w_in[j].astype(BF16)
            w_gd = jnp.pad(w_in[:, GLA_QKVR_W:], ((0, 0), (0, LANES - 2 * GATE_RANK)))
            pad_lo = jnp.zeros((GATE_RANK, GLA_DK_TOT), F32)
            pad_hi = jnp.zeros((LANES - 2 * GATE_RANK, GLA_DK_TOT), F32)
            w0 = jnp.concatenate([gla_w_gate_up[j, 0], pad_lo, pad_hi])
            w1 = jnp.concatenate([pad_lo, gla_w_gate_up[j, 1], pad_hi])
            proj, la = _gla_proj(h, w_in[:, :GLA_QKVR_W], w_gd, w0, w1, gla_b_gate[j])
            of_p, ob_p, st = _gla_scan(proj, la, 0, BATCH, SEQ, None)
            of_s, ob_s = _gla_scan(proj, la, N_PROMPT, DEC_BATCH, DEC_SEQ, state_gla[:, j])
            new_gla.append(st)
            mixed = _gla_post(of_p, of_s, ob_p, ob_s, proj,
                              jnp.tile(gla_norm[j], GLA_HEADS).reshape(1, GLA_DV_TOT))
            x, h = _matmul_residual(mixed, gla_w_out[j].astype(BF16), x, mods, 2,
                                    (norm_ffn[layer], mods, 3, 4))
            act = _swiglu_up(h, ffn_w_gate_up[j].astype(BF16), tm=1024, tn=1408)
            x, h = _matmul_residual(act, ffn_w_down[j].astype(BF16), x, mods, 5,
                                    (norm_mix[layer + 1], mods_all[layer + 1], 0, 1))
        else:
            wd = mla_w_down[j]
            w_down = jnp.concatenate([wd, wd[:, Q_LORA + KV_LORA:][:, swap]], axis=1).astype(BF16)
            wq = mla_w_uq[j].reshape(Q_LORA, MLA_HEADS, QK_NOPE + QK_ROPE)
            w_uq = jnp.concatenate([wq, wq[:, :, QK_NOPE:][:, :, swap]], axis=2)
            q, ckv, kr, krp = _mla_down(h, w_down, mla_q_norm[j], mla_kv_norm[j], cs,
                                        w_uq.reshape(Q_LORA, -1).astype(BF16))
            w_ukv = mla_w_ukv[j].astype(BF16)
            kv = _matmul(ckv.astype(BF16), w_ukv, BF16, tm=1024, tn=1024)
            ckv_ctx = cache_ckv[:, j].reshape(DEC_BATCH * PAST_LEN, KV_LORA).astype(BF16)
            kv_ctx = _matmul(ckv_ctx, w_ukv, BF16, tm=1024, tn=1024)
            kr_ctx = jnp.pad(cache_krope[:, j].reshape(DEC_BATCH * PAST_LEN, QK_ROPE),
                             ((0, 0), (0, LANES - QK_ROPE))).astype(BF16)
            o_p = _attention(q, [(kv, krp, 0, SEQ)], 0, BATCH, SEQ, tq=SEQ, heads=MLA_HEADS)
            o_s = _attention(q, [(kv_ctx, kr_ctx, 0, PAST_LEN), (kv, krp, N_PROMPT, DEC_SEQ)],
                             N_PROMPT, DEC_BATCH, DEC_SEQ, tq=1024, heads=1)
            new_ckv.append(ckv[:N_PROMPT].reshape(BATCH, SEQ, KV_LORA))
            new_kr.append(kr[:N_PROMPT, :QK_ROPE].reshape(BATCH, SEQ, QK_ROPE))
            x = _matmul_residual_split(o_p, o_s, mla_w_out[j].astype(BF16), x, mods, 2)

            router_pad = jnp.pad(moe_router[j], ((0, 0), (0, LANES - N_EXPERTS)))
            h, gates, slots, cnt = _modulate_router(x, norm_ffn[layer], mods, 3, 4, router_pad)
            counts = cnt[:, 0, :N_EXPERTS].astype(jnp.int32).reshape(-1)
            y = None
            for e in range(N_EXPERTS):
                y = _moe_expert(h, gates, slots, counts, moe_w_gate_up, moe_w_down, y, j, e)
            if layer + 1 < DEPTH:
                x, h = _gated_add(x, y, mods, 5, (norm_mix[layer + 1], mods_all[layer + 1], 0, 1))
            else:
                out_p = _gated_add_final(x, y, mods, 5, norm_final, 0, N_PROMPT)
                out_s = _gated_add_final(x, y, mods, 5, norm_final, N_PROMPT, N_SAMPLE)

    return (out_p.reshape(BATCH, SEQ, D_MODEL),
            out_s.reshape(DEC_BATCH, DEC_SEQ, D_MODEL),
            jnp.stack(new_gla, axis=1),
            jnp.stack(new_ckv, axis=1),
            jnp.stack(new_kr, axis=1))
```

```python
import functools

import jax
import jax.numpy as jnp
import numpy as np
from jax import lax
from jax.experimental import pallas as pl
from jax.experimental.pallas import tpu as pltpu

D_MODEL = 1024
BATCH = 16
SEQ = 256
DEPTH = 4
DEC_BATCH = 4
DEC_SEQ = 2048
PAST_LEN = 512
GRID_W = 64
GLA_HEADS = 4
GLA_DK = 128
GLA_DV = 256
GLA_DK_TOT = GLA_HEADS * GLA_DK
GLA_DV_TOT = GLA_HEADS * GLA_DV
GATE_RANK = 16
GATE_TEMP = 16.0
MLA_HEADS = 8
Q_LORA = 384
KV_LORA = 256
QK_NOPE = 128
QK_ROPE = 64
V_HEAD = 128
MLA_SCALE = (QK_NOPE + QK_ROPE) ** -0.5
ROPE_BASE = 10000.0
D_FF = 2816
N_EXPERTS = 8
D_EXPERT = 3584
EPS = 1e-6

N_PROMPT = BATCH * SEQ
N_SAMPLE = DEC_BATCH * DEC_SEQ
N_TOK = N_PROMPT + N_SAMPLE
N_COND = 8
LANES = 128
GLA_QKVR_W = 2 * GLA_DK_TOT + 2 * GLA_DV_TOT
GLA_CHUNK = 256
GLA_MAX_FACTORISED_DECAY = 80.0
ATTN_KEY_CHUNK = 512
MOE_BLOCK = 1024
MOE_TILE_UNIT = 64
MOE_W_CHUNKS = 14
LOG2E = 1.4426950408889634
VMEM_LIMIT = 56 * 1024 * 1024

F32 = jnp.float32
BF16 = jnp.bfloat16
NT_DIMS = (((1,), (1,)), ((), ()))
TN_DIMS = (((0,), (0,)), ((), ()))


def _params(*sem):
    return pltpu.CompilerParams(dimension_semantics=sem, vmem_limit_bytes=VMEM_LIMIT)


def _cond_row(i, tm):
    return jnp.where(i * tm < N_PROMPT, 0, 1 + jnp.maximum(i * tm - N_PROMPT, 0) // DEC_SEQ)


def _mod_spec(which, tm):
    return pl.BlockSpec((None, None, 1, D_MODEL), lambda i, *_: (_cond_row(i, tm), which, 0, 0))


def _silu(x):
    return x / (1.0 + jnp.exp(-x))


def _rms(x):
    return x * lax.rsqrt(jnp.mean(x * x, axis=-1, keepdims=True) + EPS)


def _ada_kernel(c_ref, w_ref, b_ref, o_ref):
    o_ref[...] = _dot_split(_silu(c_ref[...]), w_ref[...]) + b_ref[...]


def _ada_mods(cond, ada_w, ada_b):
    tn = 1536
    return pl.pallas_call(
        _ada_kernel,
        grid=(DEPTH, 6 * D_MODEL // tn),
        in_specs=[pl.BlockSpec((N_COND, D_MODEL), lambda l, j: (0, 0)),
                  pl.BlockSpec((None, D_MODEL, tn), lambda l, j: (l, 0, j)),
                  pl.BlockSpec((None, 1, tn), lambda l, j: (l, 0, j))],
        out_specs=pl.BlockSpec((None, N_COND, tn), lambda l, j: (l, 0, j)),
        out_shape=jax.ShapeDtypeStruct((DEPTH, N_COND, 6 * D_MODEL), F32),
        compiler_params=_params("arbitrary", "arbitrary"),
        name="ada_mods",
    )(cond, ada_w, ada_b.reshape(DEPTH, 1, 6 * D_MODEL))


def _prompt_sample_specs(tm, width):
    n_p = N_PROMPT // tm
    return [pl.BlockSpec((tm, width), lambda i, *_: (jnp.minimum(i, n_p - 1), 0)),
            pl.BlockSpec((tm, width), lambda i, *_: (jnp.maximum(i - n_p, 0), 0))]


def _pick_prompt_sample(p_ref, s_ref, tm):
    return jnp.where(pl.program_id(0) < N_PROMPT // tm, p_ref[...], s_ref[...])


def _modulate_kernel(xp_ref, xs_ref, g_ref, sh_ref, sc_ref, o_ref):
    y = _rms(_pick_prompt_sample(xp_ref, xs_ref, o_ref.shape[0])) * g_ref[...]
    o_ref[...] = (y * (1.0 + sc_ref[...]) + sh_ref[...]).astype(o_ref.dtype)


def _modulate(x_prompt, x_sample, g, mods, shift_idx, scale_idx, tm=1024):
    return pl.pallas_call(
        _modulate_kernel,
        grid=(N_TOK // tm,),
        in_specs=(_prompt_sample_specs(tm, D_MODEL)
                  + [pl.BlockSpec((1, D_MODEL), lambda i: (0, 0)),
                     _mod_spec(shift_idx, tm), _mod_spec(scale_idx, tm)]),
        out_specs=pl.BlockSpec((tm, D_MODEL), lambda i: (i, 0)),
        out_shape=jax.ShapeDtypeStruct((N_TOK, D_MODEL), BF16),
        compiler_params=_params("arbitrary"),
        name="modulate",
    )(x_prompt, x_sample, g.reshape(1, D_MODEL), mods, mods)


def _modulate_router_kernel(x_ref, g_ref, sh_ref, sc_ref, r_ref, o_ref, gates_ref, slot_ref, cnt_ref):
    y = _rms(x_ref[...]) * g_ref[...]
    h = y * (1.0 + sc_ref[...]) + sh_ref[...]
    o_ref[...] = h.astype(o_ref.dtype)
    logits = _dot_split(h, r_ref[...])
    lane = lax.broadcasted_iota(jnp.int32, logits.shape, 1)
    l1 = jnp.where(lane < N_EXPERTS, logits, -jnp.inf)
    m1 = jnp.max(l1, axis=1, keepdims=True)
    i1 = jnp.min(jnp.where(l1 == m1, lane, LANES), axis=1, keepdims=True)
    l2 = jnp.where(lane == i1, -jnp.inf, l1)
    m2 = jnp.max(l2, axis=1, keepdims=True)
    i2 = jnp.min(jnp.where(l2 == m2, lane, LANES), axis=1, keepdims=True)
    e = jnp.exp(m2 - m1)
    w1 = 1.0 / (1.0 + e)
    w2 = e / (1.0 + e)
    gates_ref[...] = jnp.where(lane == i1, w1, 0.0) + jnp.where(lane == i2, w2, 0.0)
    sel = (lane == i1) | (lane == i2)
    tb = logits.shape[0]
    earlier = (lax.broadcasted_iota(jnp.int32, (tb, tb), 1)
               < lax.broadcasted_iota(jnp.int32, (tb, tb), 0)).astype(BF16)
    before = jnp.dot(earlier, sel.astype(BF16), preferred_element_type=F32)
    slot_ref[...] = jnp.where(sel, before + 1.0, 0.0)
    cnt_ref[...] = jnp.broadcast_to(jnp.sum(sel.astype(F32), axis=0, keepdims=True), cnt_ref.shape)


def _modulate_router(x, g, mods, shift_idx, scale_idx, router_pad):
    tm = MOE_BLOCK
    nb = N_TOK // tm
    return pl.pallas_call(
        _modulate_router_kernel,
        grid=(nb,),
        in_specs=[pl.BlockSpec((tm, D_MODEL), lambda i: (i, 0)),
                  pl.BlockSpec((1, D_MODEL), lambda i: (0, 0)),
                  _mod_spec(shift_idx, tm), _mod_spec(scale_idx, tm),
                  pl.BlockSpec((D_MODEL, LANES), lambda i: (0, 0))],
        out_specs=[pl.BlockSpec((tm, D_MODEL), lambda i: (i, 0)),
                   pl.BlockSpec((tm, LANES), lambda i: (i, 0)),
                   pl.BlockSpec((tm, LANES), lambda i: (i, 0)),
                   pl.BlockSpec((None, 8, LANES), lambda i: (i, 0, 0))],
        out_shape=[jax.ShapeDtypeStruct((N_TOK, D_MODEL), BF16),
                   jax.ShapeDtypeStruct((N_TOK, LANES), F32),
                   jax.ShapeDtypeStruct((N_TOK, LANES), F32),
                   jax.ShapeDtypeStruct((nb, 8, LANES), F32)],
        compiler_params=_params("arbitrary"),
        name="modulate_router",
    )(x, g.reshape(1, D_MODEL), mods, mods, router_pad)


def _moe_expert_kernel(cnt_ref, h_ref, gates_ref, slot_ref, wgu_ref, wd_ref, *rest, expert, first):
    wgu_s, wd_s = rest[-2:]
    y_ref = rest[-3]
    step = pl.program_id(0)
    half = MOE_W_CHUNKS // 2

    wc = wgu_ref.shape[1]
    for c in range(MOE_W_CHUNKS):
        @pl.when(step == c)
        def _(c=c):
            wgu_s[:, c * wc:(c + 1) * wc] = wgu_ref[...].astype(BF16)

    @pl.when(step < half)
    def _():
        wd_s[pl.ds(pl.multiple_of(step * wc, wc), wc), :] = wd_ref[...].astype(BF16)

    @pl.when(step >= MOE_W_CHUNKS)
    def _():
        tb = h_ref.shape[0]
        gate_col = gates_ref[:, expert:expert + 1]
        slot_col = slot_ref[:, expert:expert + 1]
        count = cnt_ref[(step - MOE_W_CHUNKS) * N_EXPERTS + expert]
        cw = 256
        fh = D_EXPERT // 2

        def y_before(cols):
            return jnp.zeros((tb, cw), F32) if first else rest[0][:, cols]

        def tile(first_slot, rows, fresh):
            lane = lax.broadcasted_iota(jnp.int32, (tb, rows), 1).astype(F32)
            onehot = (slot_col == lane + (first_slot + 1).astype(F32)).astype(BF16)
            hg = lax.dot_general(onehot, h_ref[...], TN_DIMS,
                                 preferred_element_type=F32).astype(BF16)
            out = None
            for f0 in (0, fh):
                g = jnp.dot(hg, wgu_s[:, f0:f0 + fh], preferred_element_type=F32)
                u = jnp.dot(hg, wgu_s[:, D_EXPERT + f0:D_EXPERT + f0 + fh],
                            preferred_element_type=F32)
                part = jnp.dot((_silu(g) * u).astype(BF16), wd_s[f0:f0 + fh, :],
                               preferred_element_type=F32)
                out = part if out is None else out + part
            out = out.astype(BF16)
            for c0 in range(0, D_MODEL, cw):
                cols = slice(c0, c0 + cw)
                prev = jnp.where(fresh, y_before(cols), y_ref[:, cols])
                y_ref[:, cols] = prev + gate_col * jnp.dot(onehot, out[:, cols],
                                                           preferred_element_type=F32)

        unit = MOE_TILE_UNIT
        units = (count + (unit - 1)) // unit
        n_big = units // 4
        left = units - 4 * n_big

        def big_tile(k, carry):
            tile(k * (4 * unit), 4 * unit, k == 0)
            return carry

        lax.fori_loop(0, n_big, big_tile, 0)

        @pl.when(jnp.logical_and(left >= 1, left <= 2))
        def _():
            tile(n_big * (4 * unit), 2 * unit, n_big == 0)

        @pl.when(left == 3)
        def _():
            tile(n_big * (4 * unit), 3 * unit, n_big == 0)

        @pl.when(count == 0)
        def _():
            for c0 in range(0, D_MODEL, cw):
                y_ref[:, c0:c0 + cw] = y_before(slice(c0, c0 + cw))


def _moe_expert(h, gates, slots, counts, w_gu, w_down, y, layer, expert):
    tb = MOE_BLOCK
    nc = MOE_W_CHUNKS
    wc = 2 * D_EXPERT // nc
    first = y is None
    blk = lambda s, cnt: (jnp.maximum(s - nc, 0), 0)
    in_specs = [pl.BlockSpec((tb, D_MODEL), blk), pl.BlockSpec((tb, LANES), blk),
                pl.BlockSpec((tb, LANES), blk),
                pl.BlockSpec((None, None, D_MODEL, wc),
                             lambda s, cnt: (layer, expert, 0, jnp.minimum(s, nc - 1))),
                pl.BlockSpec((None, None, wc, D_MODEL),
                             lambda s, cnt: (layer, expert, jnp.minimum(s, nc // 2 - 1), 0))]
    args = [h, gates, slots, w_gu, w_down]
    if not first:
        in_specs.append(pl.BlockSpec((tb, D_MODEL), blk))
        args.append(y)
    return pl.pallas_call(
        functools.partial(_moe_expert_kernel, expert=expert, first=first),
        grid_spec=pltpu.PrefetchScalarGridSpec(
            num_scalar_prefetch=1, grid=(nc + N_TOK // tb,), in_specs=in_specs,
            out_specs=pl.BlockSpec((tb, D_MODEL), blk),
            scratch_shapes=[pltpu.VMEM((D_MODEL, 2 * D_EXPERT), BF16),
                            pltpu.VMEM((D_EXPERT, D_MODEL), BF16)]),
        out_shape=jax.ShapeDtypeStruct((N_TOK, D_MODEL), F32),
        compiler_params=_params("arbitrary"),
        name="moe_expert",
    )(counts, *args)


def _mm_kernel(a_ref, w_ref, o_ref):
    o_ref[...] = jnp.dot(a_ref[...], w_ref[...], preferred_element_type=F32).astype(o_ref.dtype)


def _matmul(a, w, out_dtype, tm, tn):
    m, k = a.shape
    n = w.shape[1]
    return pl.pallas_call(
        _mm_kernel,
        grid=(m // tm, n // tn),
        in_specs=[pl.BlockSpec((tm, k), lambda i, j: (i, 0)),
                  pl.BlockSpec((k, tn), lambda i, j: (0, j))],
        out_specs=pl.BlockSpec((tm, tn), lambda i, j: (i, j)),
        out_shape=jax.ShapeDtypeStruct((m, n), out_dtype),
        compiler_params=_params("arbitrary", "arbitrary"),
        name="matmul",
    )(a, w)


def _swiglu_kernel(a_ref, wg_ref, wu_ref, o_ref):
    a = a_ref[...]
    g = jnp.dot(a, wg_ref[...], preferred_element_type=F32)
    u = jnp.dot(a, wu_ref[...], preferred_element_type=F32)
    o_ref[...] = (_silu(g) * u).astype(o_ref.dtype)


def _swiglu_up(a, w_gu, tm, tn):
    m, k = a.shape
    f = w_gu.shape[1] // 2
    nj = f // tn
    return pl.pallas_call(
        _swiglu_kernel,
        grid=(m // tm, nj),
        in_specs=[pl.BlockSpec((tm, k), lambda i, j: (i, 0)),
                  pl.BlockSpec((k, tn), lambda i, j: (0, j)),
                  pl.BlockSpec((k, tn), lambda i, j: (0, j + nj))],
        out_specs=pl.BlockSpec((tm, tn), lambda i, j: (i, j)),
        out_shape=jax.ShapeDtypeStruct((m, f), BF16),
        compiler_params=_params("arbitrary", "arbitrary"),
        name="swiglu_up",
    )(a, w_gu, w_gu)


def _next_norm_specs(next_norm, tm):
    gain, mods, shift_idx, scale_idx = next_norm
    specs = [pl.BlockSpec((1, D_MODEL), lambda i, *_: (0, 0)),
             _mod_spec(shift_idx, tm), _mod_spec(scale_idx, tm)]
    return specs, [gain.reshape(1, D_MODEL), mods, mods]


def _mm_residual_norm_kernel(a_ref, w_ref, x_ref, g_ref, ng_ref, nsh_ref, nsc_ref, o_ref, h_ref):
    x = x_ref[...] + g_ref[...] * jnp.dot(a_ref[...], w_ref[...], preferred_element_type=F32)
    o_ref[...] = x
    h_ref[...] = (_rms(x) * ng_ref[...] * (1.0 + nsc_ref[...]) + nsh_ref[...]).astype(h_ref.dtype)


def _matmul_residual(a, w, x, mods, gate_idx, next_norm, tm=512):
    k = a.shape[1]
    norm_specs, norm_args = _next_norm_specs(next_norm, tm)
    row = lambda i: (i, 0)
    return pl.pallas_call(
        _mm_residual_norm_kernel,
        grid=(N_TOK // tm,),
        in_specs=[pl.BlockSpec((tm, k), row),
                  pl.BlockSpec((k, D_MODEL), lambda i: (0, 0)),
                  pl.BlockSpec((tm, D_MODEL), row),
                  _mod_spec(gate_idx, tm)] + norm_specs,
        out_specs=[pl.BlockSpec((tm, D_MODEL), row), pl.BlockSpec((tm, D_MODEL), row)],
        out_shape=[jax.ShapeDtypeStruct((N_TOK, D_MODEL), F32),
                   jax.ShapeDtypeStruct((N_TOK, D_MODEL), BF16)],
        compiler_params=_params("arbitrary"),
        name="matmul_residual",
    )(a, w, x, mods, *norm_args)


def _mm_residual_split_kernel(ap_ref, as_ref, w_ref, x_ref, g_ref, o_ref):
    a = _pick_prompt_sample(ap_ref, as_ref, o_ref.shape[0])
    o_ref[...] = x_ref[...] + g_ref[...] * jnp.dot(a, w_ref[...], preferred_element_type=F32)


def _matmul_residual_split(a_prompt, a_sample, w, x, mods, gate_idx, tm=512):
    k = w.shape[0]
    row = lambda i: (i, 0)
    return pl.pallas_call(
        _mm_residual_split_kernel,
        grid=(N_TOK // tm,),
        in_specs=(_prompt_sample_specs(tm, k)
                  + [pl.BlockSpec((k, D_MODEL), lambda i: (0, 0)),
                     pl.BlockSpec((tm, D_MODEL), row),
                     _mod_spec(gate_idx, tm)]),
        out_specs=pl.BlockSpec((tm, D_MODEL), row),
        out_shape=jax.ShapeDtypeStruct((N_TOK, D_MODEL), F32),
        compiler_params=_params("arbitrary"),
        name="matmul_residual_split",
    )(a_prompt, a_sample, w, x, mods)


def _gated_add_norm_kernel(x_ref, y_ref, g_ref, ng_ref, nsh_ref, nsc_ref, o_ref, h_ref):
    x = x_ref[...] + g_ref[...] * y_ref[...]
    o_ref[...] = x
    h_ref[...] = (_rms(x) * ng_ref[...] * (1.0 + nsc_ref[...]) + nsh_ref[...]).astype(h_ref.dtype)


def _gated_add(x, y, mods, gate_idx, next_norm, tm=1024):
    norm_specs, norm_args = _next_norm_specs(next_norm, tm)
    row = lambda i: (i, 0)
    return pl.pallas_call(
        _gated_add_norm_kernel,
        grid=(N_TOK // tm,),
        in_specs=[pl.BlockSpec((tm, D_MODEL), row), pl.BlockSpec((tm, D_MODEL), row),
                  _mod_spec(gate_idx, tm)] + norm_specs,
        out_specs=[pl.BlockSpec((tm, D_MODEL), row), pl.BlockSpec((tm, D_MODEL), row)],
        out_shape=[jax.ShapeDtypeStruct((N_TOK, D_MODEL), F32),
                   jax.ShapeDtypeStruct((N_TOK, D_MODEL), BF16)],
        compiler_params=_params("arbitrary"),
        name="gated_add",
    )(x, y, mods, *norm_args)


def _gated_add_final_kernel(x_ref, y_ref, g_ref, ng_ref, o_ref):
    o_ref[...] = _rms(x_ref[...] + g_ref[...] * y_ref[...]) * ng_ref[...]


def _gated_add_final(x, y, mods, gate_idx, gain, row0, n_rows, tm=1024):
    base = row0 // tm
    row = lambda i: (base + i, 0)
    return pl.pallas_call(
        _gated_add_final_kernel,
        grid=(n_rows // tm,),
        in_specs=[pl.BlockSpec((tm, D_MODEL), row), pl.BlockSpec((tm, D_MODEL), row),
                  pl.BlockSpec((None, None, 1, D_MODEL),
                               lambda i: (_cond_row(base + i, tm), gate_idx, 0, 0)),
                  pl.BlockSpec((1, D_MODEL), lambda i: (0, 0))],
        out_specs=pl.BlockSpec((tm, D_MODEL), lambda i: (i, 0)),
        out_shape=jax.ShapeDtypeStruct((n_rows, D_MODEL), F32),
        compiler_params=_params("arbitrary"),
        name="gated_add_final",
    )(x, y, mods, gain.reshape(1, D_MODEL))


def _log_sigmoid(x):
    return jnp.minimum(x, 0.0) - jnp.log(1.0 + jnp.exp(-jnp.abs(x)))


def _split_bf16(x):
    hi = x.astype(BF16)
    return hi, (x - hi.astype(F32)).astype(BF16)


def _dot_split(a, b):
    a_hi, a_lo = _split_bf16(a)
    b_hi, b_lo = _split_bf16(b)
    return (jnp.dot(a_hi, b_hi, preferred_element_type=F32)
            + jnp.dot(a_lo, b_hi, preferred_element_type=F32)
            + jnp.dot(a_hi, b_lo, preferred_element_type=F32))


def _gla_proj_kernel(h_ref, w_ref, wgd_ref, w0_ref, w1_ref, b_ref, proj_ref, la_ref):
    h = h_ref[...]
    proj_ref[...] = jnp.dot(h, w_ref[...], preferred_element_type=F32).astype(proj_ref.dtype)

    @pl.when(pl.program_id(1) == 0)
    def _():
        gd = jnp.dot(h, wgd_ref[...], preferred_element_type=F32)
        x0 = _dot_split(gd, w0_ref[...]) + b_ref[0:1, :]
        x1 = _dot_split(gd, w1_ref[...]) + b_ref[1:2, :]
        la_ref[:, :GLA_DK_TOT] = _log_sigmoid(x0) * (1.0 / GATE_TEMP)
        la_ref[:, GLA_DK_TOT:] = _log_sigmoid(x1) * (1.0 / GATE_TEMP)


def _gla_proj(h, w_qkvr, w_gd, w0, w1, b_gate, tm=1024, tn=1536):
    fixed = lambda i, j: (0, 0)
    return pl.pallas_call(
        _gla_proj_kernel,
        grid=(N_TOK // tm, GLA_QKVR_W // tn),
        in_specs=[pl.BlockSpec((tm, D_MODEL), lambda i, j: (i, 0)),
                  pl.BlockSpec((D_MODEL, tn), lambda i, j: (0, j)),
                  pl.BlockSpec((D_MODEL, LANES), fixed),
                  pl.BlockSpec((LANES, GLA_DK_TOT), fixed),
                  pl.BlockSpec((LANES, GLA_DK_TOT), fixed),
                  pl.BlockSpec((2, GLA_DK_TOT), fixed)],
        out_specs=[pl.BlockSpec((tm, tn), lambda i, j: (i, j)),
                   pl.BlockSpec((tm, 2 * GLA_DK_TOT), lambda i, j: (i, 0))],
        out_shape=[jax.ShapeDtypeStruct((N_TOK, GLA_QKVR_W), BF16),
                   jax.ShapeDtypeStruct((N_TOK, 2 * GLA_DK_TOT), F32)],
        compiler_params=_params("arbitrary", "arbitrary"),
        name="gla_proj",
    )(h, w_qkvr, w_gd, w0, w1, b_gate)


def _gla_keep_mask(causal):
    c = GLA_CHUNK
    row = lax.broadcasted_iota(jnp.int32, (c, c), 0)
    col = lax.broadcasted_iota(jnp.int32, (c, c), 1)
    return (col <= row) if causal else (col >= row)


def _gla_cum_decay(la_ref, causal):
    mask = _gla_keep_mask(causal).astype(BF16)
    la = la_ref[...]
    hi = la.astype(BF16)
    lo = (la - hi.astype(F32)).astype(BF16)
    return (jnp.dot(mask, hi, preferred_element_type=F32)
            + jnp.dot(mask, lo, preferred_element_type=F32))


def _gla_direction(q_ref, k_ref, v_ref, b, o_ref, s_ref, b_ref, d, causal, factorised):
    c = GLA_CHUNK
    keep = _gla_keep_mask(causal)
    end = c - 1 if causal else 0
    b_end = b[end:end + 1, :]
    q_dec = q_ref[...].astype(F32) * (GLA_DK ** -0.5) * jnp.exp(b)
    k_end = k_ref[...] * jnp.exp(b_end - b)
    e_end = jnp.exp(b_end)
    eye = (lax.broadcasted_iota(jnp.int32, (GLA_DK, GLA_DK), 0)
           == lax.broadcasted_iota(jnp.int32, (GLA_DK, GLA_DK), 1))
    if factorised:
        k_inv = k_ref[...] * jnp.exp(-b)
    else:
        b_ref[...] = b
        col = lax.broadcasted_iota(jnp.int32, (c, c), 1)
    for h in range(GLA_HEADS):
        ks = slice(h * GLA_DK, (h + 1) * GLA_DK)
        vs = slice(h * GLA_DV, (h + 1) * GLA_DV)
        state = s_ref[d, h]
        qh = q_dec[:, ks].astype(BF16)
        vh = v_ref[:, vs].astype(BF16)
        if factorised:
            scores = lax.dot_general(qh, k_inv[:, ks].astype(BF16), NT_DIMS,
                                     preferred_element_type=F32)
        else:
            def key_column(s, sc, ks=ks):
                base = pl.multiple_of((s // 16) * 16, 16)
                pick = lax.broadcasted_iota(jnp.int32, (16, GLA_DK), 0) == s % 16
                b_s = jnp.sum(jnp.where(pick, b_ref[pl.ds(base, 16), ks], 0.0),
                              axis=0, keepdims=True)
                k_s = jnp.sum(jnp.where(pick, k_ref[pl.ds(base, 16), ks].astype(F32), 0.0),
                              axis=0, keepdims=True)
                decay = jnp.exp(jnp.minimum(b_ref[:, ks] - b_s, 0.0))
                column = jnp.sum(q_ref[:, ks].astype(F32) * (GLA_DK ** -0.5) * k_s * decay,
                                 axis=1, keepdims=True)
                return jnp.where(col == s, column, sc)

            scores = lax.fori_loop(0, c, key_column, jnp.zeros((c, c), F32))
        scores = jnp.where(keep, scores, 0.0)
        o_ref[:, vs] = (jnp.dot(scores.astype(BF16), vh, preferred_element_type=F32)
                        + jnp.dot(qh, state.astype(BF16), preferred_element_type=F32)
                        ).astype(o_ref.dtype)
        e_col = jnp.sum(jnp.where(eye, jnp.broadcast_to(e_end[:, ks], (GLA_DK, GLA_DK)), 0.0),
                        axis=1, keepdims=True)
        s_ref[d, h] = state * e_col + lax.dot_general(
            k_end[:, ks].astype(BF16), vh, TN_DIMS, preferred_element_type=F32)


def _gla_scan_kernel(*refs, n_chunks, has_s0, write_state):
    qf, kf, vf, laf, qb, kb, vb, lab = refs[:8]
    rest = list(refs[8:])
    s0_ref = rest.pop(0) if has_s0 else None
    of_ref, ob_ref = rest.pop(0), rest.pop(0)
    sout_ref = rest.pop(0) if write_state else None
    s_ref, b_ref = rest
    i = pl.program_id(1)

    @pl.when(i == 0)
    def _():
        s_ref[...] = s0_ref[...] if has_s0 else jnp.zeros(s_ref.shape, F32)

    b_f = _gla_cum_decay(laf, True)
    b_b = _gla_cum_decay(lab, False)
    c = GLA_CHUNK
    total_decay = jnp.minimum(jnp.min(b_f[c - 1:c, :]), jnp.min(b_b[0:1, :]))
    factorisable = total_decay >= -GLA_MAX_FACTORISED_DECAY

    def step(factorised):
        _gla_direction(qf, kf, vf, b_f, of_ref, s_ref, b_ref.at[0], 0, True, factorised)
        _gla_direction(qb, kb, vb, b_b, ob_ref, s_ref, b_ref.at[1], 1, False, factorised)

    pl.when(factorisable)(functools.partial(step, True))
    pl.when(jnp.logical_not(factorisable))(functools.partial(step, False))

    if write_state:
        @pl.when(i == n_chunks - 1)
        def _():
            sout_ref[...] = s_ref[...]


def _gla_scan(proj, la, row0, n_batch, seq, s0):
    c = GLA_CHUNK
    n = seq // c
    base = row0 // c
    has_s0 = s0 is not None
    write_state = not has_s0
    fwd = lambda b, i: base + b * n + i
    bwd = lambda b, i: base + b * n + (n - 1 - i)

    def specs(blk):
        return [pl.BlockSpec((c, GLA_DK_TOT), lambda b, i: (blk(b, i), 0)),
                pl.BlockSpec((c, GLA_DK_TOT), lambda b, i: (blk(b, i), 1)),
                pl.BlockSpec((c, GLA_DV_TOT), lambda b, i: (blk(b, i), 1))]
    state_block = (None, 2, GLA_HEADS, GLA_DK, GLA_DV)
    state_spec = pl.BlockSpec(state_block, lambda b, i: (b, 0, 0, 0, 0))
    in_specs = (specs(fwd) + [pl.BlockSpec((c, GLA_DK_TOT), lambda b, i: (fwd(b, i), 0))]
                + specs(bwd) + [pl.BlockSpec((c, GLA_DK_TOT), lambda b, i: (bwd(b, i), 1))])
    args = [proj, proj, proj, la, proj, proj, proj, la]
    if has_s0:
        in_specs.append(state_spec)
        args.append(s0)
    out_rows = n_batch * seq
    out_specs = [pl.BlockSpec((c, GLA_DV_TOT), lambda b, i: (b * n + i, 0)),
                 pl.BlockSpec((c, GLA_DV_TOT), lambda b, i: (b * n + (n - 1 - i), 0))]
    out_shape = [jax.ShapeDtypeStruct((out_rows, GLA_DV_TOT), BF16)] * 2
    if write_state:
        out_specs.append(state_spec)
        out_shape.append(jax.ShapeDtypeStruct((n_batch, 2, GLA_HEADS, GLA_DK, GLA_DV), F32))
    return pl.pallas_call(
        functools.partial(_gla_scan_kernel, n_chunks=n, has_s0=has_s0, write_state=write_state),
        grid=(n_batch, n),
        in_specs=in_specs,
        out_specs=out_specs,
        out_shape=out_shape,
        scratch_shapes=[pltpu.VMEM((2, GLA_HEADS, GLA_DK, GLA_DV), F32),
                        pltpu.VMEM((2, c, GLA_DK_TOT), F32)],
        compiler_params=_params("arbitrary", "arbitrary"),
        name="gla_scan",
    )(*args)


def _gla_out_kernel(ofp_ref, ofs_ref, obp_ref, obs_ref, r_ref, gn_ref, w_ref, *refs, x_pair):
    tm = r_ref.shape[0]
    if x_pair:
        x_in = _pick_prompt_sample(refs[0], refs[1], tm)
        refs = refs[2:]
    else:
        x_in = refs[0][...]
        refs = refs[1:]
    g_ref, ng_ref, nsh_ref, nsc_ref, o_ref, h_ref = refs
    o = (_pick_prompt_sample(ofp_ref, ofs_ref, tm).astype(F32)
         + _pick_prompt_sample(obp_ref, obs_ref, tm).astype(F32))
    normed = jnp.concatenate(
        [_rms(o[:, h * GLA_DV:(h + 1) * GLA_DV]) for h in range(GLA_HEADS)], axis=1)
    mixed = (normed * gn_ref[...] * _silu(r_ref[...].astype(F32))).astype(BF16)
    x = x_in + g_ref[...] * jnp.dot(mixed, w_ref[...], preferred_element_type=F32)
    o_ref[...] = x
    h_ref[...] = (_rms(x) * ng_ref[...] * (1.0 + nsc_ref[...]) + nsh_ref[...]).astype(h_ref.dtype)


def _gla_out(of_p, of_s, ob_p, ob_s, proj, g_tiled, w, x, mods, gate_idx, next_norm, tm=512):
    norm_specs, norm_args = _next_norm_specs(next_norm, tm)
    row = lambda i: (i, 0)
    x_pair = isinstance(x, tuple)
    x_specs = _prompt_sample_specs(tm, D_MODEL) if x_pair else [pl.BlockSpec((tm, D_MODEL), row)]
    x_args = list(x) if x_pair else [x]
    r_block = 2 * GLA_DK_TOT // GLA_DV_TOT + 1
    return pl.pallas_call(
        functools.partial(_gla_out_kernel, x_pair=x_pair),
        grid=(N_TOK // tm,),
        in_specs=(_prompt_sample_specs(tm, GLA_DV_TOT) + _prompt_sample_specs(tm, GLA_DV_TOT)
                  + [pl.BlockSpec((tm, GLA_DV_TOT), lambda i: (i, r_block)),
                     pl.BlockSpec((1, GLA_DV_TOT), lambda i: (0, 0)),
                     pl.BlockSpec((GLA_DV_TOT, D_MODEL), lambda i: (0, 0))]
                  + x_specs + [_mod_spec(gate_idx, tm)] + norm_specs),
        out_specs=[pl.BlockSpec((tm, D_MODEL), row), pl.BlockSpec((tm, D_MODEL), row)],
        out_shape=[jax.ShapeDtypeStruct((N_TOK, D_MODEL), F32),
                   jax.ShapeDtypeStruct((N_TOK, D_MODEL), BF16)],
        compiler_params=_params("arbitrary"),
        name="gla_out",
    )(of_p, of_s, ob_p, ob_s, proj, g_tiled, w, *x_args, mods, *norm_args)


def _rope_pair(u, cs):
    t = u * cs
    return t + pltpu.roll(t, QK_ROPE, axis=1)


def _mla_down_kernel(a_ref, w_ref, qn_ref, kvn_ref, cs_ref, wq_ref, q_ref, ckv_ref, kr_ref, krp_ref):
    acc = jnp.dot(a_ref[...], w_ref[...], preferred_element_type=F32)
    cs = cs_ref[...]
    ckv_ref[...] = _rms(acc[:, Q_LORA:Q_LORA + KV_LORA]) * kvn_ref[...]
    u = acc[:, Q_LORA + KV_LORA:]
    kr_ref[...] = u
    lane = lax.broadcasted_iota(jnp.int32, u.shape, 1)
    krp_ref[...] = jnp.where(lane < QK_ROPE, _rope_pair(u, cs), 0.0).astype(krp_ref.dtype)
    cq = (_rms(acc[:, :Q_LORA]) * qn_ref[...]).astype(BF16)
    q = jnp.dot(cq, wq_ref[...], preferred_element_type=F32)
    for h in range(MLA_HEADS):
        lo = h * 2 * LANES
        q_ref[:, lo:lo + LANES] = q[:, lo:lo + LANES].astype(q_ref.dtype)
        q_ref[:, lo + LANES:lo + 2 * LANES] = _rope_pair(
            q[:, lo + LANES:lo + 2 * LANES], cs).astype(q_ref.dtype)


def _mla_down(h, w_ext, q_norm, kv_norm, cs, wq_ext, tm=512):
    wn = w_ext.shape[1]
    qn = wq_ext.shape[1]
    row = lambda i: (i, 0)
    fixed = lambda i: (0, 0)
    return pl.pallas_call(
        _mla_down_kernel,
        grid=(N_TOK // tm,),
        in_specs=[pl.BlockSpec((tm, D_MODEL), row), pl.BlockSpec((D_MODEL, wn), fixed),
                  pl.BlockSpec((1, Q_LORA), fixed), pl.BlockSpec((1, KV_LORA), fixed),
                  pl.BlockSpec((tm, LANES), row), pl.BlockSpec((Q_LORA, qn), fixed)],
        out_specs=[pl.BlockSpec((tm, qn), row), pl.BlockSpec((tm, KV_LORA), row),
                   pl.BlockSpec((tm, LANES), row), pl.BlockSpec((tm, LANES), row)],
        out_shape=[jax.ShapeDtypeStruct((N_TOK, qn), BF16),
                   jax.ShapeDtypeStruct((N_TOK, KV_LORA), F32),
                   jax.ShapeDtypeStruct((N_TOK, LANES), F32),
                   jax.ShapeDtypeStruct((N_TOK, LANES), BF16)],
        compiler_params=_params("arbitrary"),
        name="mla_down",
    )(h, w_ext, q_norm.reshape(1, Q_LORA), kv_norm.reshape(1, KV_LORA), cs, wq_ext)


def _attn_kernel(q_ref, *refs, heads):
    o_ref = refs[-1]
    sources = [(refs[i], refs[i + 1]) for i in range(0, len(refs) - 1, 2)]
    tq = q_ref.shape[0]
    c = MLA_SCALE * LOG2E
    hw = QK_NOPE + V_HEAD
    for h in range(heads):
        q = q_ref[:, h * hw:(h + 1) * hw]
        m = jnp.full((tq, 1), -jnp.inf, F32)
        l = jnp.zeros((tq, 1), F32)
        acc = jnp.zeros((tq, V_HEAD), F32)
        for kv_ref, kr_ref in sources:
            n_keys = kv_ref.shape[0]
            chunk = min(n_keys, ATTN_KEY_CHUNK)
            for c0 in range(0, n_keys, chunk):
                rows = slice(c0, c0 + chunk)
                k = jnp.concatenate([kv_ref[rows, h * hw:h * hw + QK_NOPE], kr_ref[rows, :]], axis=1)
                s = lax.dot_general(q, k, NT_DIMS, preferred_element_type=F32)
                m_new = jnp.maximum(m, jnp.max(s, axis=1, keepdims=True))
                alpha = jnp.exp2((m - m_new) * c)
                p = jnp.exp2((s - m_new) * c)
                l = alpha * l + jnp.sum(p, axis=1, keepdims=True)
                acc = alpha * acc + jnp.dot(p.astype(BF16),
                                            kv_ref[rows, h * hw + QK_NOPE:(h + 1) * hw],
                                            preferred_element_type=F32)
                m = m_new
        o_ref[:, h * V_HEAD:(h + 1) * V_HEAD] = (acc / l).astype(o_ref.dtype)


def _attention(q, sources, row0, n_batch, seq, tq, heads):
    nq = seq // tq
    base = row0 // tq
    hw = heads * (QK_NOPE + V_HEAD)
    in_specs = [pl.BlockSpec((tq, hw), lambda b, h, i: (base + b * nq + i, h))]
    args = [q]
    for kv, krp, first_row, n_keys in sources:
        first = first_row // n_keys
        in_specs += [pl.BlockSpec((n_keys, hw), lambda b, h, i, first=first: (first + b, h)),
                     pl.BlockSpec((n_keys, LANES), lambda b, h, i, first=first: (first + b, 0))]
        args += [kv, krp]
    return pl.pallas_call(
        functools.partial(_attn_kernel, heads=heads),
        grid=(n_batch, MLA_HEADS // heads, nq),
        in_specs=in_specs,
        out_specs=pl.BlockSpec((tq, heads * V_HEAD), lambda b, h, i: (b * nq + i, h)),
        out_shape=jax.ShapeDtypeStruct((n_batch * seq, MLA_HEADS * V_HEAD), BF16),
        compiler_params=_params("arbitrary", "arbitrary", "arbitrary"),
        name="mla_attention",
    )(*args)


def _rope_tables():
    half = QK_ROPE // 2
    pos = np.arange(DEC_SEQ)
    inv = ROPE_BASE ** (-np.arange(0, half, 2, dtype=np.float64) / half)
    ang = np.concatenate([(pos // GRID_W)[:, None] * inv, (pos % GRID_W)[:, None] * inv], axis=-1)
    ang = ang.astype(np.float32)
    cos = np.repeat(np.cos(ang), 2, axis=1)
    sin = np.repeat(np.sin(ang), 2, axis=1) * np.tile(np.array([-1.0, 1.0], np.float32), half)
    cs = np.concatenate([cos, sin], axis=1).astype(np.float32)
    ident = np.concatenate([np.ones((N_PROMPT, QK_ROPE), np.float32),
                            np.zeros((N_PROMPT, QK_ROPE), np.float32)], axis=1)
    return jnp.asarray(np.concatenate([ident, np.tile(cs, (DEC_BATCH, 1))], axis=0))


def kernel(x_prompt, x_sample, c, state_gla, cache_ckv, cache_krope, c_ctx, ada_w, ada_b, norm_mix, norm_ffn, norm_final, gla_w_in, gla_w_gate_up, gla_b_gate, gla_norm, gla_w_out, mla_w_down, mla_q_norm, mla_kv_norm, mla_w_uq, mla_w_ukv, mla_w_out, ffn_w_gate_up, ffn_w_down, moe_router, moe_w_gate_up, moe_w_down):
    x = (x_prompt.reshape(N_PROMPT, D_MODEL), x_sample.reshape(N_SAMPLE, D_MODEL))
    cond =jnp.concatenate([c_ctx[None], c, jnp.zeros((N_COND - 1 - DEC_BATCH, D_MODEL), F32)])
    mods_all = _ada_mods(cond, ada_w, ada_b).reshape(DEPTH, N_COND, 6, 1, D_MODEL)
    cs = _rope_tables()
    swap = np.arange(QK_ROPE) ^ 1

    new_gla, new_ckv, new_kr = [], [], []
    h = _modulate(x[0], x[1], norm_mix[0], mods_all[0], 0, 1)
    for layer in range(DEPTH):
        j = layer // 2
        mods = mods_all[layer]
        if layer % 2 == 0:
            w_in = gla_w_in[j].astype(BF16)
            w_gd = jnp.pad(w_in[:, GLA_QKVR_W:], ((0, 0), (0, LANES - 2 * GATE_RANK)))
            pad_lo = jnp.zeros((GATE_RANK, GLA_DK_TOT), F32)
            pad_hi = jnp.zeros((LANES - 2 * GATE_RANK, GLA_DK_TOT), F32)
            w0 = jnp.concatenate([gla_w_gate_up[j, 0], pad_lo, pad_hi])
            w1 = jnp.concatenate([pad_lo, gla_w_gate_up[j, 1], pad_hi])
            proj, la = _gla_proj(h, w_in[:, :GLA_QKVR_W], w_gd, w0, w1, gla_b_gate[j])
            of_p, ob_p, st = _gla_scan(proj, la, 0, BATCH, SEQ, None)
            of_s, ob_s = _gla_scan(proj, la, N_PROMPT, DEC_BATCH, DEC_SEQ, state_gla[:, j])
            new_gla.append(st)
            x, h = _gla_out(of_p, of_s, ob_p, ob_s, proj,
                            jnp.tile(gla_norm[j], GLA_HEADS).reshape(1, GLA_DV_TOT),
                            gla_w_out[j].astype(BF16), x, mods, 2, (norm_ffn[layer], mods, 3, 4))
            act = _swiglu_up(h, ffn_w_gate_up[j].astype(BF16), tm=1024, tn=1408)
            x, h = _matmul_residual(act, ffn_w_down[j].astype(BF16), x, mods, 5,
                                    (norm_mix[layer + 1], mods_all[layer + 1], 0, 1))
        else:
            wd = mla_w_down[j]
            w_down = jnp.concatenate([wd, wd[:, Q_LORA + KV_LORA:][:, swap]], axis=1).astype(BF16)
            wq = mla_w_uq[j].reshape(Q_LORA, MLA_HEADS, QK_NOPE + QK_ROPE)
            w_uq = jnp.concatenate([wq, wq[:, :, QK_NOPE:][:, :, swap]], axis=2)
            q, ckv, kr, krp = _mla_down(h, w_down, mla_q_norm[j], mla_kv_norm[j], cs,
                                        w_uq.reshape(Q_LORA, -1).astype(BF16))
            w_ukv = mla_w_ukv[j].astype(BF16)
            kv = _matmul(ckv.astype(BF16), w_ukv, BF16, tm=1024, tn=1024)
            ckv_ctx = cache_ckv[:, j].reshape(DEC_BATCH * PAST_LEN, KV_LORA).astype(BF16)
            kv_ctx = _matmul(ckv_ctx, w_ukv, BF16, tm=1024, tn=1024)
            kr_ctx = jnp.pad(cache_krope[:, j].reshape(DEC_BATCH * PAST_LEN, QK_ROPE),
                             ((0, 0), (0, LANES - QK_ROPE))).astype(BF16)
            o_p = _attention(q, [(kv, krp, 0, SEQ)], 0, BATCH, SEQ, tq=SEQ, heads=MLA_HEADS)
            o_s = _attention(q, [(kv_ctx, kr_ctx, 0, PAST_LEN), (kv, krp, N_PROMPT, DEC_SEQ)],
                             N_PROMPT, DEC_BATCH, DEC_SEQ, tq=1024, heads=1)
            new_ckv.append(ckv[:N_PROMPT].reshape(BATCH, SEQ, KV_LORA))
            new_kr.append(kr[:N_PROMPT, :QK_ROPE].reshape(BATCH, SEQ, QK_ROPE))
            x = _matmul_residual_split(o_p, o_s, mla_w_out[j].astype(BF16), x, mods, 2)

            router_pad = jnp.pad(moe_router[j], ((0, 0), (0, LANES - N_EXPERTS)))
            h, gates, slots, cnt = _modulate_router(x, norm_ffn[layer], mods, 3, 4, router_pad)
            counts = cnt[:, 0, :N_EXPERTS].astype(jnp.int32).reshape(-1)
            y = None
            for e in range(N_EXPERTS):
                y = _moe_expert(h, gates, slots, counts, moe_w_gate_up, moe_w_down, y, j, e)
            if layer + 1 < DEPTH:
                x, h = _gated_add(x, y, mods, 5, (norm_mix[layer + 1], mods_all[layer + 1], 0, 1))
            else:
                out_p = _gated_add_final(x, y, mods, 5, norm_final, 0, N_PROMPT)
                out_s = _gated_add_final(x, y, mods, 5, norm_final, N_PROMPT, N_SAMPLE)

    return (out_p.reshape(BATCH, SEQ, D_MODEL),
            out_s.reshape(DEC_BATCH, DEC_SEQ, D_MODEL),
            jnp.stack(new_gla, axis=1),
            jnp.stack(new_ckv, axis=1),
            jnp.stack(new_kr, axis=1))
```

```python
import functools

import jax
import jax.numpy as jnp
import numpy as np
from jax import lax
from jax.experimental import pallas as pl
from jax.experimental.pallas import tpu as pltpu

D_MODEL = 1024
BATCH = 16
SEQ = 256
DEPTH = 4
DEC_BATCH = 4
DEC_SEQ = 2048
PAST_LEN = 512
GRID_W = 64
GLA_HEADS = 4
GLA_DK = 128
GLA_DV = 256
GLA_DK_TOT = GLA_HEADS * GLA_DK
GLA_DV_TOT = GLA_HEADS * GLA_DV
GATE_RANK = 16
GATE_TEMP = 16.0
MLA_HEADS = 8
Q_LORA = 384
KV_LORA = 256
QK_NOPE = 128
QK_ROPE = 64
V_HEAD = 128
MLA_SCALE = (QK_NOPE + QK_ROPE) ** -0.5
ROPE_BASE = 10000.0
D_FF = 2816
N_EXPERTS = 8
D_EXPERT = 3584
EPS = 1e-6

N_PROMPT = BATCH * SEQ
N_SAMPLE = DEC_BATCH * DEC_SEQ
N_TOK = N_PROMPT + N_SAMPLE
N_COND = 8
LANES = 128
GLA_QKVR_W = 2 * GLA_DK_TOT + 2 * GLA_DV_TOT
GLA_CHUNK = 256
GLA_MAX_FACTORISED_DECAY = 80.0
ATTN_KEY_CHUNK = 512
MOE_BLOCK = 1024
MOE_TILE_UNIT = 64
MOE_W_CHUNKS = 14
LOG2E = 1.4426950408889634
VMEM_LIMIT = 56 * 1024 * 1024

F32 = jnp.float32
BF16 = jnp.bfloat16
NT_DIMS = (((1,), (1,)), ((), ()))
TN_DIMS = (((0,), (0,)), ((), ()))


def _params(*sem):
    return pltpu.CompilerParams(dimension_semantics=sem, vmem_limit_bytes=VMEM_LIMIT)


def _cond_row(i, tm):
    return jnp.where(i * tm < N_PROMPT, 0, 1 + jnp.maximum(i * tm - N_PROMPT, 0) // DEC_SEQ)


def _mod_spec(which, tm):
    return pl.BlockSpec((None, None, 1, D_MODEL), lambda i, *_: (_cond_row(i, tm), which, 0, 0))


def _silu(x):
    return x / (1.0 + jnp.exp(-x))


def _rms(x):
    return x * lax.rsqrt(jnp.mean(x * x, axis=-1, keepdims=True) + EPS)


def _ada_kernel(c_ref, w_ref, b_ref, o_ref):
    o_ref[...] = _dot_split(_silu(c_ref[...]), w_ref[...]) + b_ref[...]


def _ada_mods(cond, ada_w, ada_b):
    tn = 1536
    return pl.pallas_call(
        _ada_kernel,
        grid=(DEPTH, 6 * D_MODEL // tn),
        in_specs=[pl.BlockSpec((N_COND, D_MODEL), lambda l, j: (0, 0)),
                  pl.BlockSpec((None, D_MODEL, tn), lambda l, j: (l, 0, j)),
                  pl.BlockSpec((None, 1, tn), lambda l, j: (l, 0, j))],
        out_specs=pl.BlockSpec((None, N_COND, tn), lambda l, j: (l, 0, j)),
        out_shape=jax.ShapeDtypeStruct((DEPTH, N_COND, 6 * D_MODEL), F32),
        compiler_params=_params("arbitrary", "arbitrary"),
        name="ada_mods",
    )(cond, ada_w, ada_b.reshape(DEPTH, 1, 6 * D_MODEL))


def _prompt_sample_specs(tm, width):
    n_p = N_PROMPT // tm
    return [pl.BlockSpec((tm, width), lambda i, *_: (jnp.minimum(i, n_p - 1), 0)),
            pl.BlockSpec((tm, width), lambda i, *_: (jnp.maximum(i - n_p, 0), 0))]


def _pick_prompt_sample(p_ref, s_ref, tm):
    return jnp.where(pl.program_id(0) < N_PROMPT // tm, p_ref[...], s_ref[...])


def _modulate_kernel(xp_ref, xs_ref, g_ref, sh_ref, sc_ref, o_ref):
    y = _rms(_pick_prompt_sample(xp_ref, xs_ref, o_ref.shape[0])) * g_ref[...]
    o_ref[...] = (y * (1.0 + sc_ref[...]) + sh_ref[...]).astype(o_ref.dtype)


def _modulate(x_prompt, x_sample, g, mods, shift_idx, scale_idx, tm=1024):
    return pl.pallas_call(
        _modulate_kernel,
        grid=(N_TOK // tm,),
        in_specs=(_prompt_sample_specs(tm, D_MODEL)
                  + [pl.BlockSpec((1, D_MODEL), lambda i: (0, 0)),
                     _mod_spec(shift_idx, tm), _mod_spec(scale_idx, tm)]),
        out_specs=pl.BlockSpec((tm, D_MODEL), lambda i: (i, 0)),
        out_shape=jax.ShapeDtypeStruct((N_TOK, D_MODEL), BF16),
        compiler_params=_params("arbitrary"),
        name="modulate",
    )(x_prompt, x_sample, g.reshape(1, D_MODEL), mods, mods)


def _modulate_router_kernel(x_ref, g_ref, sh_ref, sc_ref, r_ref, o_ref, gates_ref, slot_ref, cnt_ref):
    y = _rms(x_ref[...]) * g_ref[...]
    h = y * (1.0 + sc_ref[...]) + sh_ref[...]
    o_ref[...] = h.astype(o_ref.dtype)
    logits = _dot_split(h, r_ref[...])
    lane = lax.broadcasted_iota(jnp.int32, logits.shape, 1)
    l1 = jnp.where(lane < N_EXPERTS, logits, -jnp.inf)
    m1 = jnp.max(l1, axis=1, keepdims=True)
    i1 = jnp.min(jnp.where(l1 == m1, lane, LANES), axis=1, keepdims=True)
    l2 = jnp.where(lane == i1, -jnp.inf, l1)
    m2 = jnp.max(l2, axis=1, keepdims=True)
    i2 = jnp.min(jnp.where(l2 == m2, lane, LANES), axis=1, keepdims=True)
    e = jnp.exp(m2 - m1)
    w1 = 1.0 / (1.0 + e)
    w2 = e / (1.0 + e)
    gates_ref[...] = jnp.where(lane == i1, w1, 0.0) + jnp.where(lane == i2, w2, 0.0)
    sel = (lane == i1) | (lane == i2)
    tb = logits.shape[0]
    earlier = (lax.broadcasted_iota(jnp.int32, (tb, tb), 1)
               < lax.broadcasted_iota(jnp.int32, (tb, tb), 0)).astype(BF16)
    before = jnp.dot(earlier, sel.astype(BF16), preferred_element_type=F32)
    slot_ref[...] = jnp.where(sel, before + 1.0, 0.0)
    cnt_ref[...] = jnp.broadcast_to(jnp.sum(sel.astype(F32), axis=0, keepdims=True), cnt_ref.shape)


def _modulate_router(x, g, mods, shift_idx, scale_idx, router_pad):
    tm = MOE_BLOCK
    nb = N_TOK // tm
    return pl.pallas_call(
        _modulate_router_kernel,
        grid=(nb,),
        in_specs=[pl.BlockSpec((tm, D_MODEL), lambda i: (i, 0)),
                  pl.BlockSpec((1, D_MODEL), lambda i: (0, 0)),
                  _mod_spec(shift_idx, tm), _mod_spec(scale_idx, tm),
                  pl.BlockSpec((D_MODEL, LANES), lambda i: (0, 0))],
        out_specs=[pl.BlockSpec((tm, D_MODEL), lambda i: (i, 0)),
                   pl.BlockSpec((tm, LANES), lambda i: (i, 0)),
                   pl.BlockSpec((tm, LANES), lambda i: (i, 0)),
                   pl.BlockSpec((None, 8, LANES), lambda i: (i, 0, 0))],
        out_shape=[jax.ShapeDtypeStruct((N_TOK, D_MODEL), BF16),
                   jax.ShapeDtypeStruct((N_TOK, LANES), F32),
                   jax.ShapeDtypeStruct((N_TOK, LANES), F32),
                   jax.ShapeDtypeStruct((nb, 8, LANES), F32)],
        compiler_params=_params("arbitrary"),
        name="modulate_router",
    )(x, g.reshape(1, D_MODEL), mods, mods, router_pad)


def _moe_expert_kernel(cnt_ref, h_ref, gates_ref, slot_ref, wgu_ref, wd_ref, *rest, expert, first):
    wgu_s, wd_s = rest[-2:]
    y_ref = rest[-3]
    step = pl.program_id(0)
    half = MOE_W_CHUNKS // 2

    wc = wgu_ref.shape[1]
    for c in range(MOE_W_CHUNKS):
        @pl.when(step == c)
        def _(c=c):
            wgu_s[:, c * wc:(c + 1) * wc] = wgu_ref[...].astype(BF16)

    @pl.when(step < half)
    def _():
        wd_s[pl.ds(pl.multiple_of(step * wc, wc), wc), :] = wd_ref[...].astype(BF16)

    @pl.when(step >= MOE_W_CHUNKS)
    def _():
        tb = h_ref.shape[0]
        gate_col = gates_ref[:, expert:expert + 1]
        slot_col = slot_ref[:, expert:expert + 1]
        count = cnt_ref[(step - MOE_W_CHUNKS) * N_EXPERTS + expert]
        cw = 256
        fh = D_EXPERT // 2

        def y_before(cols):
            return jnp.zeros((tb, cw), F32) if first else rest[0][:, cols]

        def tile(first_slot, rows, fresh):
            lane = lax.broadcasted_iota(jnp.int32, (tb, rows), 1).astype(F32)
            onehot = (slot_col == lane + (first_slot + 1).astype(F32)).astype(BF16)
            hg = lax.dot_general(onehot, h_ref[...], TN_DIMS,
                                 preferred_element_type=F32).astype(BF16)
            out = None
            for f0 in (0, fh):
                g = jnp.dot(hg, wgu_s[:, f0:f0 + fh], preferred_element_type=F32)
                u = jnp.dot(hg, wgu_s[:, D_EXPERT + f0:D_EXPERT + f0 + fh],
                            preferred_element_type=F32)
                part = jnp.dot((_silu(g) * u).astype(BF16), wd_s[f0:f0 + fh, :],
                               preferred_element_type=F32)
                out = part if out is None else out + part
            out = out.astype(BF16)
            for c0 in range(0, D_MODEL, cw):
                cols = slice(c0, c0 + cw)
                prev = jnp.where(fresh, y_before(cols), y_ref[:, cols])
                y_ref[:, cols] = prev + gate_col * jnp.dot(onehot, out[:, cols],
                                                           preferred_element_type=F32)

        unit = MOE_TILE_UNIT
        units = (count + (unit - 1)) // unit
        n_big = units // 4
        left = units - 4 * n_big

        def big_tile(k, carry):
            tile(k * (4 * unit), 4 * unit, k == 0)
            return carry

        lax.fori_loop(0, n_big, big_tile, 0)

        @pl.when(jnp.logical_and(left >= 1, left <= 2))
        def _():
            tile(n_big * (4 * unit), 2 * unit, n_big == 0)

        @pl.when(left == 3)
        def _():
            tile(n_big * (4 * unit), 3 * unit, n_big == 0)

        @pl.when(count == 0)
        def _():
            for c0 in range(0, D_MODEL, cw):
                y_ref[:, c0:c0 + cw] = y_before(slice(c0, c0 + cw))


def _moe_expert(h, gates, slots, counts, w_gu, w_down, y, layer, expert):
    tb = MOE_BLOCK
    nc = MOE_W_CHUNKS
    wc = 2 * D_EXPERT // nc
    first = y is None
    blk = lambda s, cnt: (jnp.maximum(s - nc, 0), 0)
    in_specs = [pl.BlockSpec((tb, D_MODEL), blk), pl.BlockSpec((tb, LANES), blk),
                pl.BlockSpec((tb, LANES), blk),
                pl.BlockSpec((None, None, D_MODEL, wc),
                             lambda s, cnt: (layer, expert, 0, jnp.minimum(s, nc - 1))),
                pl.BlockSpec((None, None, wc, D_MODEL),
                             lambda s, cnt: (layer, expert, jnp.minimum(s, nc // 2 - 1), 0))]
    args = [h, gates, slots, w_gu, w_down]
    if not first:
        in_specs.append(pl.BlockSpec((tb, D_MODEL), blk))
        args.append(y)
    return pl.pallas_call(
        functools.partial(_moe_expert_kernel, expert=expert, first=first),
        grid_spec=pltpu.PrefetchScalarGridSpec(
            num_scalar_prefetch=1, grid=(nc + N_TOK // tb,), in_specs=in_specs,
            out_specs=pl.BlockSpec((tb, D_MODEL), blk),
            scratch_shapes=[pltpu.VMEM((D_MODEL, 2 * D_EXPERT), BF16),
                            pltpu.VMEM((D_EXPERT, D_MODEL), BF16)]),
        out_shape=jax.ShapeDtypeStruct((N_TOK, D_MODEL), F32),
        compiler_params=_params("arbitrary"),
        name="moe_expert",
    )(counts, *args)


def _mm_kernel(a_ref, w_ref, o_ref):
    o_ref[...] = jnp.dot(a_ref[...], w_ref[...], preferred_element_type=F32).astype(o_ref.dtype)


def _matmul(a, w, out_dtype, tm, tn):
    m, k = a.shape
    n = w.shape[1]
    return pl.pallas_call(
        _mm_kernel,
        grid=(m // tm, n // tn),
        in_specs=[pl.BlockSpec((tm, k), lambda i, j: (i, 0)),
                  pl.BlockSpec((k, tn), lambda i, j: (0, j))],
        out_specs=pl.BlockSpec((tm, tn), lambda i, j: (i, j)),
        out_shape=jax.ShapeDtypeStruct((m, n), out_dtype),
        compiler_params=_params("arbitrary", "arbitrary"),
        name="matmul",
    )(a, w)


def _swiglu_kernel(a_ref, wg_ref, wu_ref, o_ref):
    a = a_ref[...]
    g = jnp.dot(a, wg_ref[...], preferred_element_type=F32)
    u = jnp.dot(a, wu_ref[...], preferred_element_type=F32)
    o_ref[...] = (_silu(g) * u).astype(o_ref.dtype)


def _swiglu_up(a, w_gu, layer, tm, tn):
    m, k = a.shape
    f = w_gu.shape[2] // 2
    nj = f // tn
    return pl.pallas_call(
        _swiglu_kernel,
        grid=(m // tm, nj),
        in_specs=[pl.BlockSpec((tm, k), lambda i, j: (i, 0)),
                  pl.BlockSpec((None, k, tn), lambda i, j: (layer, 0, j)),
                  pl.BlockSpec((None, k, tn), lambda i, j: (layer, 0, j + nj))],
        out_specs=pl.BlockSpec((tm, tn), lambda i, j: (i, j)),
        out_shape=jax.ShapeDtypeStruct((m, f), BF16),
        compiler_params=_params("arbitrary", "arbitrary"),
        name="swiglu_up",
    )(a, w_gu, w_gu)


def _next_norm_specs(next_norm, tm):
    gain, mods, shift_idx, scale_idx = next_norm
    specs = [pl.BlockSpec((1, D_MODEL), lambda i, *_: (0, 0)),
             _mod_spec(shift_idx, tm), _mod_spec(scale_idx, tm)]
    return specs, [gain.reshape(1, D_MODEL), mods, mods]


def _mm_residual_norm_kernel(a_ref, w_ref, x_ref, g_ref, ng_ref, nsh_ref, nsc_ref, o_ref, h_ref):
    x = x_ref[...] + g_ref[...] * jnp.dot(a_ref[...], w_ref[...], preferred_element_type=F32)
    o_ref[...] = x
    h_ref[...] = (_rms(x) * ng_ref[...] * (1.0 + nsc_ref[...]) + nsh_ref[...]).astype(h_ref.dtype)


def _matmul_residual(a, w, layer, x, mods, gate_idx, next_norm, tm=512):
    k = a.shape[1]
    norm_specs, norm_args = _next_norm_specs(next_norm, tm)
    row = lambda i: (i, 0)
    return pl.pallas_call(
        _mm_residual_norm_kernel,
        grid=(N_TOK // tm,),
        in_specs=[pl.BlockSpec((tm, k), row),
                  pl.BlockSpec((None, k, D_MODEL), lambda i: (layer, 0, 0)),
                  pl.BlockSpec((tm, D_MODEL), row),
                  _mod_spec(gate_idx, tm)] + norm_specs,
        out_specs=[pl.BlockSpec((tm, D_MODEL), row), pl.BlockSpec((tm, D_MODEL), row)],
        out_shape=[jax.ShapeDtypeStruct((N_TOK, D_MODEL), F32),
                   jax.ShapeDtypeStruct((N_TOK, D_MODEL), BF16)],
        compiler_params=_params("arbitrary"),
        name="matmul_residual",
    )(a, w, x, mods, *norm_args)


def _mm_residual_split_kernel(ap_ref, as_ref, w_ref, x_ref, g_ref, o_ref):
    a = _pick_prompt_sample(ap_ref, as_ref, o_ref.shape[0])
    o_ref[...] = x_ref[...] + g_ref[...] * jnp.dot(a, w_ref[...], preferred_element_type=F32)


def _matmul_residual_split(a_prompt, a_sample, w, x, mods, gate_idx, tm=512):
    k = w.shape[0]
    row = lambda i: (i, 0)
    return pl.pallas_call(
        _mm_residual_split_kernel,
        grid=(N_TOK // tm,),
        in_specs=(_prompt_sample_specs(tm, k)
                  + [pl.BlockSpec((k, D_MODEL), lambda i: (0, 0)),
                     pl.BlockSpec((tm, D_MODEL), row),
                     _mod_spec(gate_idx, tm)]),
        out_specs=pl.BlockSpec((tm, D_MODEL), row),
        out_shape=jax.ShapeDtypeStruct((N_TOK, D_MODEL), F32),
        compiler_params=_params("arbitrary"),
        name="matmul_residual_split",
    )(a_prompt, a_sample, w, x, mods)


def _gated_add_norm_kernel(x_ref, y_ref, g_ref, ng_ref, nsh_ref, nsc_ref, o_ref, h_ref):
    x = x_ref[...] + g_ref[...] * y_ref[...]
    o_ref[...] = x
    h_ref[...] = (_rms(x) * ng_ref[...] * (1.0 + nsc_ref[...]) + nsh_ref[...]).astype(h_ref.dtype)


def _gated_add(x, y, mods, gate_idx, next_norm, tm=1024):
    norm_specs, norm_args = _next_norm_specs(next_norm, tm)
    row = lambda i: (i, 0)
    return pl.pallas_call(
        _gated_add_norm_kernel,
        grid=(N_TOK // tm,),
        in_specs=[pl.BlockSpec((tm, D_MODEL), row), pl.BlockSpec((tm, D_MODEL), row),
                  _mod_spec(gate_idx, tm)] + norm_specs,
        out_specs=[pl.BlockSpec((tm, D_MODEL), row), pl.BlockSpec((tm, D_MODEL), row)],
        out_shape=[jax.ShapeDtypeStruct((N_TOK, D_MODEL), F32),
                   jax.ShapeDtypeStruct((N_TOK, D_MODEL), BF16)],
        compiler_params=_params("arbitrary"),
        name="gated_add",
    )(x, y, mods, *norm_args)


def _gated_add_final_kernel(x_ref, y_ref, g_ref, ng_ref, o_ref):
    o_ref[...] = _rms(x_ref[...] + g_ref[...] * y_ref[...]) * ng_ref[...]


def _gated_add_final(x, y, mods, gate_idx, gain, row0, n_rows, tm=1024):
    base = row0 // tm
    row = lambda i: (base + i, 0)
    return pl.pallas_call(
        _gated_add_final_kernel,
        grid=(n_rows // tm,),
        in_specs=[pl.BlockSpec((tm, D_MODEL), row), pl.BlockSpec((tm, D_MODEL), row),
                  pl.BlockSpec((None, None, 1, D_MODEL),
                               lambda i: (_cond_row(base + i, tm), gate_idx, 0, 0)),
                  pl.BlockSpec((1, D_MODEL), lambda i: (0, 0))],
        out_specs=pl.BlockSpec((tm, D_MODEL), lambda i: (i, 0)),
        out_shape=jax.ShapeDtypeStruct((n_rows, D_MODEL), F32),
        compiler_params=_params("arbitrary"),
        name="gated_add_final",
    )(x, y, mods, gain.reshape(1, D_MODEL))


def _log_sigmoid(x):
    return jnp.minimum(x, 0.0) - jnp.log(1.0 + jnp.exp(-jnp.abs(x)))


def _split_bf16(x):
    hi = x.astype(BF16)
    return hi, (x - hi.astype(F32)).astype(BF16)


def _dot_split(a, b):
    a_hi, a_lo = _split_bf16(a)
    b_hi, b_lo = _split_bf16(b)
    return (jnp.dot(a_hi, b_hi, preferred_element_type=F32)
            + jnp.dot(a_lo, b_hi, preferred_element_type=F32)
            + jnp.dot(a_hi, b_lo, preferred_element_type=F32))


def _gla_proj_kernel(h_ref, w_ref, wgd_ref, w0_ref, w1_ref, b_ref, proj_ref, la_ref):
    h = h_ref[...]
    proj_ref[...] = jnp.dot(h, w_ref[...], preferred_element_type=F32).astype(proj_ref.dtype)

    @pl.when(pl.program_id(1) == 0)
    def _():
        gd = jnp.dot(h, wgd_ref[...], preferred_element_type=F32)
        x0 = _dot_split(gd, w0_ref[...]) + b_ref[0:1, :]
        x1 = _dot_split(gd, w1_ref[...]) + b_ref[1:2, :]
        la_ref[:, :GLA_DK_TOT] = _log_sigmoid(x0) * (1.0 / GATE_TEMP)
        la_ref[:, GLA_DK_TOT:] = _log_sigmoid(x1) * (1.0 / GATE_TEMP)


def _gla_proj(h, w_in, layer, w_gd, w0, w1, b_gate, tm=1024, tn=1536):
    fixed = lambda i, j: (0, 0)
    return pl.pallas_call(
        _gla_proj_kernel,
        grid=(N_TOK // tm, GLA_QKVR_W // tn),
        in_specs=[pl.BlockSpec((tm, D_MODEL), lambda i, j: (i, 0)),
                  pl.BlockSpec((None, D_MODEL, tn), lambda i, j: (layer, 0, j)),
                  pl.BlockSpec((D_MODEL, LANES), fixed),
                  pl.BlockSpec((LANES, GLA_DK_TOT), fixed),
                  pl.BlockSpec((LANES, GLA_DK_TOT), fixed),
                  pl.BlockSpec((2, GLA_DK_TOT), fixed)],
        out_specs=[pl.BlockSpec((tm, tn), lambda i, j: (i, j)),
                   pl.BlockSpec((tm, 2 * GLA_DK_TOT), lambda i, j: (i, 0))],
        out_shape=[jax.ShapeDtypeStruct((N_TOK, GLA_QKVR_W), BF16),
                   jax.ShapeDtypeStruct((N_TOK, 2 * GLA_DK_TOT), F32)],
        compiler_params=_params("arbitrary", "arbitrary"),
        name="gla_proj",
    )(h, w_in, w_gd, w0, w1, b_gate)


def _gla_keep_mask(causal):
    c = GLA_CHUNK
    row = lax.broadcasted_iota(jnp.int32, (c, c), 0)
    col = lax.broadcasted_iota(jnp.int32, (c, c), 1)
    return (col <= row) if causal else (col >= row)


def _gla_cum_decay(la_ref, causal):
    mask = _gla_keep_mask(causal).astype(BF16)
    la = la_ref[...]
    hi = la.astype(BF16)
    lo = (la - hi.astype(F32)).astype(BF16)
    return (jnp.dot(mask, hi, preferred_element_type=F32)
            + jnp.dot(mask, lo, preferred_element_type=F32))


def _gla_direction(q_ref, k_ref, v_ref, b, o_ref, s_ref, b_ref, d, causal, factorised):
    c = GLA_CHUNK
    keep = _gla_keep_mask(causal)
    end = c - 1 if causal else 0
    b_end = b[end:end + 1, :]
    q_dec = q_ref[...].astype(F32) * (GLA_DK ** -0.5) * jnp.exp(b)
    k_end = k_ref[...] * jnp.exp(b_end - b)
    e_end = jnp.exp(b_end)
    eye = (lax.broadcasted_iota(jnp.int32, (GLA_DK, GLA_DK), 0)
           == lax.broadcasted_iota(jnp.int32, (GLA_DK, GLA_DK), 1))
    if factorised:
        k_inv = k_ref[...] * jnp.exp(-b)
    else:
        b_ref[...] = b
        col = lax.broadcasted_iota(jnp.int32, (c, c), 1)
    for h in range(GLA_HEADS):
        ks = slice(h * GLA_DK, (h + 1) * GLA_DK)
        vs = slice(h * GLA_DV, (h + 1) * GLA_DV)
        state = s_ref[d, h]
        qh = q_dec[:, ks].astype(BF16)
        vh = v_ref[:, vs].astype(BF16)
        if factorised:
            scores = lax.dot_general(qh, k_inv[:, ks].astype(BF16), NT_DIMS,
                                     preferred_element_type=F32)
        else:
            def key_column(s, sc, ks=ks):
                base = pl.multiple_of((s // 16) * 16, 16)
                pick = lax.broadcasted_iota(jnp.int32, (16, GLA_DK), 0) == s % 16
                b_s = jnp.sum(jnp.where(pick, b_ref[pl.ds(base, 16), ks], 0.0),
                              axis=0, keepdims=True)
                k_s = jnp.sum(jnp.where(pick, k_ref[pl.ds(base, 16), ks].astype(F32), 0.0),
                              axis=0, keepdims=True)
                decay = jnp.exp(jnp.minimum(b_ref[:, ks] - b_s, 0.0))
                column = jnp.sum(q_ref[:, ks].astype(F32) * (GLA_DK ** -0.5) * k_s * decay,
                                 axis=1, keepdims=True)
                return jnp.where(col == s, column, sc)

            scores = lax.fori_loop(0, c, key_column, jnp.zeros((c, c), F32))
        scores = jnp.where(keep, scores, 0.0)
        o_ref[:, vs] = (jnp.dot(scores.astype(BF16), vh, preferred_element_type=F32)
                        + jnp.dot(qh, state.astype(BF16), preferred_element_type=F32)
                        ).astype(o_ref.dtype)
        e_col = jnp.sum(jnp.where(eye, jnp.broadcast_to(e_end[:, ks], (GLA_DK, GLA_DK)), 0.0),
                        axis=1, keepdims=True)
        s_ref[d, h] = state * e_col + lax.dot_general(
            k_end[:, ks].astype(BF16), vh, TN_DIMS, preferred_element_type=F32)


def _gla_scan_kernel(*refs, n_chunks, has_s0, write_state):
    qf, kf, vf, laf, qb, kb, vb, lab = refs[:8]
    rest = list(refs[8:])
    s0_ref = rest.pop(0) if has_s0 else None
    of_ref, ob_ref = rest.pop(0), rest.pop(0)
    sout_ref = rest.pop(0) if write_state else None
    s_ref, b_ref = rest
    i = pl.program_id(1)

    @pl.when(i == 0)
    def _():
        s_ref[...] = s0_ref[...] if has_s0 else jnp.zeros(s_ref.shape, F32)

    b_f = _gla_cum_decay(laf, True)
    b_b = _gla_cum_decay(lab, False)
    c = GLA_CHUNK
    total_decay = jnp.minimum(jnp.min(b_f[c - 1:c, :]), jnp.min(b_b[0:1, :]))
    factorisable = total_decay >= -GLA_MAX_FACTORISED_DECAY

    def step(factorised):
        _gla_direction(qf, kf, vf, b_f, of_ref, s_ref, b_ref.at[0], 0, True, factorised)
        _gla_direction(qb, kb, vb, b_b, ob_ref, s_ref, b_ref.at[1], 1, False, factorised)

    pl.when(factorisable)(functools.partial(step, True))
    pl.when(jnp.logical_not(factorisable))(functools.partial(step, False))

    if write_state:
        @pl.when(i == n_chunks - 1)
        def _():
            sout_ref[...] = s_ref[...]


def _gla_scan(proj, la, row0, n_batch, seq, s0):
    c = GLA_CHUNK
    n = seq // c
    base = row0 // c
    has_s0 = s0 is not None
    write_state = not has_s0
    fwd = lambda b, i: base + b * n + i
    bwd = lambda b, i: base + b * n + (n - 1 - i)

    def specs(blk):
        return [pl.BlockSpec((c, GLA_DK_TOT), lambda b, i: (blk(b, i), 0)),
                pl.BlockSpec((c, GLA_DK_TOT), lambda b, i: (blk(b, i), 1)),
                pl.BlockSpec((c, GLA_DV_TOT), lambda b, i: (blk(b, i), 1))]
    state_block = (None, 2, GLA_HEADS, GLA_DK, GLA_DV)
    state_spec = pl.BlockSpec(state_block, lambda b, i: (b, 0, 0, 0, 0))
    in_specs = (specs(fwd) + [pl.BlockSpec((c, GLA_DK_TOT), lambda b, i: (fwd(b, i), 0))]
                + specs(bwd) + [pl.BlockSpec((c, GLA_DK_TOT), lambda b, i: (bwd(b, i), 1))])
    args = [proj, proj, proj, la, proj, proj, proj, la]
    if has_s0:
        in_specs.append(state_spec)
        args.append(s0)
    out_rows = n_batch * seq
    out_specs = [pl.BlockSpec((c, GLA_DV_TOT), lambda b, i: (b * n + i, 0)),
                 pl.BlockSpec((c, GLA_DV_TOT), lambda b, i: (b * n + (n - 1 - i), 0))]
    out_shape = [jax.ShapeDtypeStruct((out_rows, GLA_DV_TOT), BF16)] * 2
    if write_state:
        out_specs.append(state_spec)
        out_shape.append(jax.ShapeDtypeStruct((n_batch, 2, GLA_HEADS, GLA_DK, GLA_DV), F32))
    return pl.pallas_call(
        functools.partial(_gla_scan_kernel, n_chunks=n, has_s0=has_s0, write_state=write_state),
        grid=(n_batch, n),
        in_specs=in_specs,
        out_specs=out_specs,
        out_shape=out_shape,
        scratch_shapes=[pltpu.VMEM((2, GLA_HEADS, GLA_DK, GLA_DV), F32),
                        pltpu.VMEM((2, c, GLA_DK_TOT), F32)],
        compiler_params=_params("arbitrary", "arbitrary"),
        name="gla_scan",
    )(*args)


def _gla_out_kernel(ofp_ref, ofs_ref, obp_ref, obs_ref, r_ref, gn_ref, w_ref, *refs, x_pair):
    tm = r_ref.shape[0]
    if x_pair:
        x_in = _pick_prompt_sample(refs[0], refs[1], tm)
        refs = refs[2:]
    else:
        x_in = refs[0][...]
        refs = refs[1:]
    g_ref, ng_ref, nsh_ref, nsc_ref, o_ref, h_ref = refs
    o = (_pick_prompt_sample(ofp_ref, ofs_ref, tm).astype(F32)
         + _pick_prompt_sample(obp_ref, obs_ref, tm).astype(F32))
    normed = jnp.concatenate(
        [_rms(o[:, h * GLA_DV:(h + 1) * GLA_DV]) for h in range(GLA_HEADS)], axis=1)
    mixed = (normed * gn_ref[...] * _silu(r_ref[...].astype(F32))).astype(BF16)
    x = x_in + g_ref[...] * jnp.dot(mixed, w_ref[...], preferred_element_type=F32)
    o_ref[...] = x
    h_ref[...] = (_rms(x) * ng_ref[...] * (1.0 + nsc_ref[...]) + nsh_ref[...]).astype(h_ref.dtype)


def _gla_out(of_p, of_s, ob_p, ob_s, proj, g_tiled, w, x, mods, gate_idx, next_norm, tm=512):
    norm_specs, norm_args = _next_norm_specs(next_norm, tm)
    row = lambda i: (i, 0)
    x_pair = isinstance(x, tuple)
    x_specs = _prompt_sample_specs(tm, D_MODEL) if x_pair else [pl.BlockSpec((tm, D_MODEL), row)]
    x_args = list(x) if x_pair else [x]
    r_block = 2 * GLA_DK_TOT // GLA_DV_TOT + 1
    return pl.pallas_call(
        functools.partial(_gla_out_kernel, x_pair=x_pair),
        grid=(N_TOK // tm,),
        in_specs=(_prompt_sample_specs(tm, GLA_DV_TOT) + _prompt_sample_specs(tm, GLA_DV_TOT)
                  + [pl.BlockSpec((tm, GLA_DV_TOT), lambda i: (i, r_block)),
                     pl.BlockSpec((1, GLA_DV_TOT), lambda i: (0, 0)),
                     pl.BlockSpec((GLA_DV_TOT, D_MODEL), lambda i: (0, 0))]
                  + x_specs + [_mod_spec(gate_idx, tm)] + norm_specs),
        out_specs=[pl.BlockSpec((tm, D_MODEL), row), pl.BlockSpec((tm, D_MODEL), row)],
        out_shape=[jax.ShapeDtypeStruct((N_TOK, D_MODEL), F32),
                   jax.ShapeDtypeStruct((N_TOK, D_MODEL), BF16)],
        compiler_params=_params("arbitrary"),
        name="gla_out",
    )(of_p, of_s, ob_p, ob_s, proj, g_tiled, w, *x_args, mods, *norm_args)


def _rope_pair(u, cs):
    t = u * cs
    return t + pltpu.roll(t, QK_ROPE, axis=1)


def _mla_down_kernel(a_ref, w_ref, qn_ref, kvn_ref, cs_ref, wq_ref, wkv_ref,
                     q_ref, ckv_ref, kr_ref, krp_ref, kv_ref):
    acc = jnp.dot(a_ref[...], w_ref[...], preferred_element_type=F32)
    cs = cs_ref[...]
    ckv = _rms(acc[:, Q_LORA:Q_LORA + KV_LORA]) * kvn_ref[...]
    ckv_ref[...] = ckv
    kv_ref[...] = jnp.dot(ckv.astype(BF16), wkv_ref[...],
                          preferred_element_type=F32).astype(kv_ref.dtype)
    u = acc[:, Q_LORA + KV_LORA:]
    kr_ref[...] = u
    lane = lax.broadcasted_iota(jnp.int32, u.shape, 1)
    krp_ref[...] = jnp.where(lane < QK_ROPE, _rope_pair(u, cs), 0.0).astype(krp_ref.dtype)
    cq = (_rms(acc[:, :Q_LORA]) * qn_ref[...]).astype(BF16)
    q = jnp.dot(cq, wq_ref[...], preferred_element_type=F32)
    for h in range(MLA_HEADS):
        lo = h * 2 * LANES
        q_ref[:, lo:lo + LANES] = q[:, lo:lo + LANES].astype(q_ref.dtype)
        q_ref[:, lo + LANES:lo + 2 * LANES] = _rope_pair(
            q[:, lo + LANES:lo + 2 * LANES], cs).astype(q_ref.dtype)


def _mla_down(h, w_ext, q_norm, kv_norm, cs, wq_ext, w_ukv, tm=512):
    wn = w_ext.shape[1]
    qn = wq_ext.shape[1]
    kvn = w_ukv.shape[1]
    row = lambda i: (i, 0)
    fixed = lambda i: (0, 0)
    return pl.pallas_call(
        _mla_down_kernel,
        grid=(N_TOK // tm,),
        in_specs=[pl.BlockSpec((tm, D_MODEL), row), pl.BlockSpec((D_MODEL, wn), fixed),
                  pl.BlockSpec((1, Q_LORA), fixed), pl.BlockSpec((1, KV_LORA), fixed),
                  pl.BlockSpec((tm, LANES), row), pl.BlockSpec((Q_LORA, qn), fixed),
                  pl.BlockSpec((KV_LORA, kvn), fixed)],
        out_specs=[pl.BlockSpec((tm, qn), row), pl.BlockSpec((tm, KV_LORA), row),
                   pl.BlockSpec((tm, LANES), row), pl.BlockSpec((tm, LANES), row),
                   pl.BlockSpec((tm, kvn), row)],
        out_shape=[jax.ShapeDtypeStruct((N_TOK, qn), BF16),
                   jax.ShapeDtypeStruct((N_TOK, KV_LORA), F32),
                   jax.ShapeDtypeStruct((N_TOK, LANES), F32),
                   jax.ShapeDtypeStruct((N_TOK, LANES), BF16),
                   jax.ShapeDtypeStruct((N_TOK, kvn), BF16)],
        compiler_params=_params("arbitrary"),
        name="mla_down",
    )(h, w_ext, q_norm.reshape(1, Q_LORA), kv_norm.reshape(1, KV_LORA), cs, wq_ext, w_ukv)


def _attn_kernel(q_ref, *refs, heads):
    o_ref = refs[-1]
    sources = [(refs[i], refs[i + 1]) for i in range(0, len(refs) - 1, 2)]
    tq = q_ref.shape[0]
    c = MLA_SCALE * LOG2E
    hw = QK_NOPE + V_HEAD
    for h in range(heads):
        q = q_ref[:, h * hw:(h + 1) * hw]
        m = jnp.full((tq, 1), -jnp.inf, F32)
        l = jnp.zeros((tq, 1), F32)
        acc = jnp.zeros((tq, V_HEAD), F32)
        for kv_ref, kr_ref in sources:
            n_keys = kv_ref.shape[0]
            chunk = min(n_keys, ATTN_KEY_CHUNK)
            for c0 in range(0, n_keys, chunk):
                rows = slice(c0, c0 + chunk)
                k = jnp.concatenate([kv_ref[rows, h * hw:h * hw + QK_NOPE], kr_ref[rows, :]], axis=1)
                s = lax.dot_general(q, k, NT_DIMS, preferred_element_type=F32)
                m_new = jnp.maximum(m, jnp.max(s, axis=1, keepdims=True))
                alpha = jnp.exp2((m - m_new) * c)
                p = jnp.exp2((s - m_new) * c)
                l = alpha * l + jnp.sum(p, axis=1, keepdims=True)
                acc = alpha * acc + jnp.dot(p.astype(BF16),
                                            kv_ref[rows, h * hw + QK_NOPE:(h + 1) * hw],
                                            preferred_element_type=F32)
                m = m_new
        o_ref[:, h * V_HEAD:(h + 1) * V_HEAD] = (acc / l).astype(o_ref.dtype)


def _attention(q, sources, row0, n_batch, seq, tq, heads):
    nq = seq // tq
    base = row0 // tq
    hw = heads * (QK_NOPE + V_HEAD)
    in_specs = [pl.BlockSpec((tq, hw), lambda b, h, i: (base + b * nq + i, h))]
    args = [q]
    for kv, krp, first_row, n_keys in sources:
        first = first_row // n_keys
        in_specs += [pl.BlockSpec((n_keys, hw), lambda b, h, i, first=first: (first + b, h)),
                     pl.BlockSpec((n_keys, LANES), lambda b, h, i, first=first: (first + b, 0))]
        args += [kv, krp]
    return pl.pallas_call(
        functools.partial(_attn_kernel, heads=heads),
        grid=(n_batch, MLA_HEADS // heads, nq),
        in_specs=in_specs,
        out_specs=pl.BlockSpec((tq, heads * V_HEAD), lambda b, h, i: (b * nq + i, h)),
        out_shape=jax.ShapeDtypeStruct((n_batch * seq, MLA_HEADS * V_HEAD), BF16),
        compiler_params=_params("arbitrary", "arbitrary", "arbitrary"),
        name="mla_attention",
    )(*args)


def _rope_tables():
    half = QK_ROPE // 2
    pos = np.arange(DEC_SEQ)
    inv = ROPE_BASE ** (-np.arange(0, half, 2, dtype=np.float64) / half)
    ang = np.concatenate([(pos // GRID_W)[:, None] * inv, (pos % GRID_W)[:, None] * inv], axis=-1)
    ang = ang.astype(np.float32)
    cos = np.repeat(np.cos(ang), 2, axis=1)
    sin = np.repeat(np.sin(ang), 2, axis=1) * np.tile(np.array([-1.0, 1.0], np.float32), half)
    cs = np.concatenate([cos, sin], axis=1).astype(np.float32)
    ident = np.concatenate([np.ones((N_PROMPT, QK_ROPE), np.float32),
                            np.zeros((N_PROMPT, QK_ROPE), np.float32)], axis=1)
    return jnp.asarray(np.concatenate([ident, np.tile(cs, (DEC_BATCH, 1))], axis=0))


def kernel(x_prompt, x_sample, c, state_gla, cache_ckv, cache_krope, c_ctx, ada_w, ada_b, norm_mix, norm_ffn, norm_final, gla_w_in, gla_w_gate_up, gla_b_gate, gla_norm, gla_w_out, mla_w_down, mla_q_norm, mla_kv_norm, mla_w_uq, mla_w_ukv, mla_w_out, ffn_w_gate_up, ffn_w_down, moe_router, moe_w_gate_up, moe_w_down):
    x = (x_prompt.reshape(N_PROMPT, D_MODEL), x_sample.reshape(N_SAMPLE, D_MODEL))
    cond =jnp.concatenate([c_ctx[None], c, jnp.zeros((N_COND - 1 - DEC_BATCH, D_MODEL), F32)])
    mods_all = _ada_mods(cond, ada_w, ada_b).reshape(DEPTH, N_COND, 6, 1, D_MODEL)
    cs = _rope_tables()
    swap = np.arange(QK_ROPE) ^ 1
    gla_w_in_bf16 = gla_w_in.astype(BF16)
    ffn_w_gate_up_bf16 = ffn_w_gate_up.astype(BF16)
    ffn_w_down_bf16 = ffn_w_down.astype(BF16)

    new_gla, new_ckv, new_kr = [], [], []
    h = _modulate(x[0], x[1], norm_mix[0], mods_all[0], 0, 1)
    for layer in range(DEPTH):
        j = layer // 2
        mods = mods_all[layer]
        if layer % 2 == 0:
            w_gd = jnp.pad(gla_w_in[j, :, GLA_QKVR_W:].astype(BF16),
                           ((0, 0), (0, LANES - 2 * GATE_RANK)))
            pad_lo = jnp.zeros((GATE_RANK, GLA_DK_TOT), F32)
            pad_hi = jnp.zeros((LANES - 2 * GATE_RANK, GLA_DK_TOT), F32)
            w0 = jnp.concatenate([gla_w_gate_up[j, 0], pad_lo, pad_hi])
            w1 = jnp.concatenate([pad_lo, gla_w_gate_up[j, 1], pad_hi])
            proj, la = _gla_proj(h, gla_w_in_bf16, j, w_gd, w0, w1, gla_b_gate[j])
            of_p, ob_p, st = _gla_scan(proj, la, 0, BATCH, SEQ, None)
            of_s, ob_s = _gla_scan(proj, la, N_PROMPT, DEC_BATCH, DEC_SEQ, state_gla[:, j])
            new_gla.append(st)
            x, h = _gla_out(of_p, of_s, ob_p, ob_s, proj,
                            jnp.tile(gla_norm[j], GLA_HEADS).reshape(1, GLA_DV_TOT),
                            gla_w_out[j].astype(BF16), x, mods, 2, (norm_ffn[layer], mods, 3, 4))
            act = _swiglu_up(h, ffn_w_gate_up_bf16, j, tm=1024, tn=1408)
            x, h = _matmul_residual(act, ffn_w_down_bf16, j, x, mods, 5,
                                    (norm_mix[layer + 1], mods_all[layer + 1], 0, 1))
        else:
            wd = mla_w_down[j]
            w_down = jnp.concatenate([wd, wd[:, Q_LORA + KV_LORA:][:, swap]], axis=1).astype(BF16)
            wq = mla_w_uq[j].reshape(Q_LORA, MLA_HEADS, QK_NOPE + QK_ROPE)
            w_uq = jnp.concatenate([wq, wq[:, :, QK_NOPE:][:, :, swap]], axis=2)
            w_ukv = mla_w_ukv[j].astype(BF16)
            q, ckv, kr, krp, kv = _mla_down(h, w_down, mla_q_norm[j], mla_kv_norm[j], cs,
                                            w_uq.reshape(Q_LORA, -1).astype(BF16), w_ukv)
            ckv_ctx = cache_ckv[:, j].reshape(DEC_BATCH * PAST_LEN, KV_LORA).astype(BF16)
            kv_ctx = _matmul(ckv_ctx, w_ukv, BF16, tm=1024, tn=1024)
            kr_ctx = jnp.pad(cache_krope[:, j].reshape(DEC_BATCH * PAST_LEN, QK_ROPE),
                             ((0, 0), (0, LANES - QK_ROPE))).astype(BF16)
            o_p = _attention(q, [(kv, krp, 0, SEQ)], 0, BATCH, SEQ, tq=SEQ, heads=MLA_HEADS)
            o_s = _attention(q, [(kv_ctx, kr_ctx, 0, PAST_LEN), (kv, krp, N_PROMPT, DEC_SEQ)],
                             N_PROMPT, DEC_BATCH, DEC_SEQ, tq=1024, heads=1)
            new_ckv.append(ckv[:N_PROMPT].reshape(BATCH, SEQ, KV_LORA))
            new_kr.append(kr[:N_PROMPT, :QK_ROPE].reshape(BATCH, SEQ, QK_ROPE))
            x = _matmul_residual_split(o_p, o_s, mla_w_out[j].astype(BF16), x, mods, 2)

            router_pad = jnp.pad(moe_router[j], ((0, 0), (0, LANES - N_EXPERTS)))
            h, gates, slots, cnt = _modulate_router(x, norm_ffn[layer], mods, 3, 4, router_pad)
            counts = cnt[:, 0, :N_EXPERTS].astype(jnp.int32).reshape(-1)
            y = None
            for e in range(N_EXPERTS):
                y = _moe_expert(h, gates, slots, counts, moe_w_gate_up, moe_w_down, y, j, e)
            if layer + 1 < DEPTH:
                x, h = _gated_add(x, y, mods, 5, (norm_mix[layer + 1], mods_all[layer + 1], 0, 1))
            else:
                out_p = _gated_add_final(x, y, mods, 5, norm_final, 0, N_PROMPT)
                out_s = _gated_add_final(x, y, mods, 5, norm_final, N_PROMPT, N_SAMPLE)

    return (out_p.reshape(BATCH, SEQ, D_MODEL),
            out_s.reshape(DEC_BATCH, DEC_SEQ, D_MODEL),
            jnp.stack(new_gla, axis=1),
            jnp.stack(new_ckv, axis=1),
            jnp.stack(new_kr, axis=1))
```

```python
import functools

import jax
import jax.numpy as jnp
import numpy as np
from jax import lax
from jax.experimental import pallas as pl
from jax.experimental.pallas import tpu as pltpu

D_MODEL = 1024
BATCH = 16
SEQ = 256
DEPTH = 4
DEC_BATCH = 4
DEC_SEQ = 2048
PAST_LEN = 512
GRID_W = 64
GLA_HEADS = 4
GLA_DK = 128
GLA_DV = 256
GLA_DK_TOT = GLA_HEADS * GLA_DK
GLA_DV_TOT = GLA_HEADS * GLA_DV
GATE_RANK = 16
GATE_TEMP = 16.0
MLA_HEADS = 8
Q_LORA = 384
KV_LORA = 256
QK_NOPE = 128
QK_ROPE = 64
V_HEAD = 128
MLA_SCALE = (QK_NOPE + QK_ROPE) ** -0.5
ROPE_BASE = 10000.0
D_FF = 2816
N_EXPERTS = 8
D_EXPERT = 3584
EPS = 1e-6

N_PROMPT = BATCH * SEQ
N_SAMPLE = DEC_BATCH * DEC_SEQ
N_TOK = N_PROMPT + N_SAMPLE
N_COND = 8
LANES = 128
GLA_QKVR_W = 2 * GLA_DK_TOT + 2 * GLA_DV_TOT
GLA_CHUNK = 256
GLA_MAX_FACTORISED_DECAY = 80.0
ATTN_KEY_CHUNK = 512
MOE_BLOCK = 1024
MOE_TILE_UNIT = 64
MOE_W_CHUNKS = 14
LOG2E = 1.4426950408889634
VMEM_LIMIT = 56 * 1024 * 1024

F32 = jnp.float32
BF16 = jnp.bfloat16
NT_DIMS = (((1,), (1,)), ((), ()))
TN_DIMS = (((0,), (0,)), ((), ()))


def _params(*sem):
    return pltpu.CompilerParams(dimension_semantics=sem, vmem_limit_bytes=VMEM_LIMIT)


def _cond_row(i, tm):
    return jnp.where(i * tm < N_PROMPT, 0, 1 + jnp.maximum(i * tm - N_PROMPT, 0) // DEC_SEQ)


def _mod_spec(which, tm):
    return pl.BlockSpec((None, None, 1, D_MODEL), lambda i, *_: (_cond_row(i, tm), which, 0, 0))


def _silu(x):
    return x / (1.0 + jnp.exp(-x))


def _rms(x):
    return x * lax.rsqrt(jnp.mean(x * x, axis=-1, keepdims=True) + EPS)


def _ada_kernel(c_ref, w_ref, b_ref, o_ref):
    o_ref[...] = _dot_split(_silu(c_ref[...]), w_ref[...]) + b_ref[...]


def _ada_mods(cond, ada_w, ada_b):
    tn = 1536
    return pl.pallas_call(
        _ada_kernel,
        grid=(DEPTH, 6 * D_MODEL // tn),
        in_specs=[pl.BlockSpec((N_COND, D_MODEL), lambda l, j: (0, 0)),
                  pl.BlockSpec((None, D_MODEL, tn), lambda l, j: (l, 0, j)),
                  pl.BlockSpec((None, 1, tn), lambda l, j: (l, 0, j))],
        out_specs=pl.BlockSpec((None, N_COND, tn), lambda l, j: (l, 0, j)),
        out_shape=jax.ShapeDtypeStruct((DEPTH, N_COND, 6 * D_MODEL), F32),
        compiler_params=_params("arbitrary", "arbitrary"),
        name="ada_mods",
    )(cond, ada_w, ada_b.reshape(DEPTH, 1, 6 * D_MODEL))


def _prompt_sample_specs(tm, width):
    n_p = N_PROMPT // tm
    return [pl.BlockSpec((tm, width), lambda i, *_: (jnp.minimum(i, n_p - 1), 0)),
            pl.BlockSpec((tm, width), lambda i, *_: (jnp.maximum(i - n_p, 0), 0))]


def _pick_prompt_sample(p_ref, s_ref, tm):
    return jnp.where(pl.program_id(0) < N_PROMPT // tm, p_ref[...], s_ref[...])


def _modulate_kernel(xp_ref, xs_ref, g_ref, sh_ref, sc_ref, o_ref):
    y = _rms(_pick_prompt_sample(xp_ref, xs_ref, o_ref.shape[0])) * g_ref[...]
    o_ref[...] = (y * (1.0 + sc_ref[...]) + sh_ref[...]).astype(o_ref.dtype)


def _modulate(x_prompt, x_sample, g, mods, shift_idx, scale_idx, tm=1024):
    return pl.pallas_call(
        _modulate_kernel,
        grid=(N_TOK // tm,),
        in_specs=(_prompt_sample_specs(tm, D_MODEL)
                  + [pl.BlockSpec((1, D_MODEL), lambda i: (0, 0)),
                     _mod_spec(shift_idx, tm), _mod_spec(scale_idx, tm)]),
        out_specs=pl.BlockSpec((tm, D_MODEL), lambda i: (i, 0)),
        out_shape=jax.ShapeDtypeStruct((N_TOK, D_MODEL), BF16),
        compiler_params=_params("arbitrary"),
        name="modulate",
    )(x_prompt, x_sample, g.reshape(1, D_MODEL), mods, mods)


def _modulate_router_kernel(x_ref, g_ref, sh_ref, sc_ref, r_ref, o_ref, gates_ref, slot_ref, cnt_ref):
    y = _rms(x_ref[...]) * g_ref[...]
    h = y * (1.0 + sc_ref[...]) + sh_ref[...]
    o_ref[...] = h.astype(o_ref.dtype)
    logits = _dot_split(h, r_ref[...])
    lane = lax.broadcasted_iota(jnp.int32, logits.shape, 1)
    l1 = jnp.where(lane < N_EXPERTS, logits, -jnp.inf)
    m1 = jnp.max(l1, axis=1, keepdims=True)
    i1 = jnp.min(jnp.where(l1 == m1, lane, LANES), axis=1, keepdims=True)
    l2 = jnp.where(lane == i1, -jnp.inf, l1)
    m2 = jnp.max(l2, axis=1, keepdims=True)
    i2 = jnp.min(jnp.where(l2 == m2, lane, LANES), axis=1, keepdims=True)
    e = jnp.exp(m2 - m1)
    w1 = 1.0 / (1.0 + e)
    w2 = e / (1.0 + e)
    gates_ref[...] = jnp.where(lane == i1, w1, 0.0) + jnp.where(lane == i2, w2, 0.0)
    sel = (lane == i1) | (lane == i2)
    tb = logits.shape[0]
    earlier = (lax.broadcasted_iota(jnp.int32, (tb, tb), 1)
               < lax.broadcasted_iota(jnp.int32, (tb, tb), 0)).astype(BF16)
    before = jnp.dot(earlier, sel.astype(BF16), preferred_element_type=F32)
    slot_ref[...] = jnp.where(sel, before + 1.0, 0.0)
    cnt_ref[...] = jnp.broadcast_to(jnp.sum(sel.astype(F32), axis=0, keepdims=True), cnt_ref.shape)


def _modulate_router(x, g, mods, shift_idx, scale_idx, router_pad):
    tm = MOE_BLOCK
    nb = N_TOK // tm
    return pl.pallas_call(
        _modulate_router_kernel,
        grid=(nb,),
        in_specs=[pl.BlockSpec((tm, D_MODEL), lambda i: (i, 0)),
                  pl.BlockSpec((1, D_MODEL), lambda i: (0, 0)),
                  _mod_spec(shift_idx, tm), _mod_spec(scale_idx, tm),
                  pl.BlockSpec((D_MODEL, LANES), lambda i: (0, 0))],
        out_specs=[pl.BlockSpec((tm, D_MODEL), lambda i: (i, 0)),
                   pl.BlockSpec((tm, LANES), lambda i: (i, 0)),
                   pl.BlockSpec((tm, LANES), lambda i: (i, 0)),
                   pl.BlockSpec((None, 8, LANES), lambda i: (i, 0, 0))],
        out_shape=[jax.ShapeDtypeStruct((N_TOK, D_MODEL), BF16),
                   jax.ShapeDtypeStruct((N_TOK, LANES), F32),
                   jax.ShapeDtypeStruct((N_TOK, LANES), F32),
                   jax.ShapeDtypeStruct((nb, 8, LANES), F32)],
        compiler_params=_params("arbitrary"),
        name="modulate_router",
    )(x, g.reshape(1, D_MODEL), mods, mods, router_pad)


def _moe_expert_kernel(cnt_ref, h_ref, gates_ref, slot_ref, wgu_ref, wd_ref, *rest, expert, first):
    wgu_s, wd_s = rest[-2:]
    y_ref = rest[-3]
    step = pl.program_id(0)
    half = MOE_W_CHUNKS // 2

    wc = wgu_ref.shape[1]
    for c in range(MOE_W_CHUNKS):
        @pl.when(step == c)
        def _(c=c):
            wgu_s[:, c * wc:(c + 1) * wc] = wgu_ref[...].astype(BF16)

    @pl.when(step < half)
    def _():
        wd_s[pl.ds(pl.multiple_of(step * wc, wc), wc), :] = wd_ref[...].astype(BF16)

    @pl.when(step >= MOE_W_CHUNKS)
    def _():
        tb = h_ref.shape[0]
        gate_col = gates_ref[:, expert:expert + 1]
        slot_col = slot_ref[:, expert:expert + 1]
        count = cnt_ref[(step - MOE_W_CHUNKS) * N_EXPERTS + expert]
        cw = 256
        fh = D_EXPERT // 2

        def y_before(cols):
            return jnp.zeros((tb, cw), F32) if first else rest[0][:, cols]

        def tile(first_slot, rows, fresh):
            lane = lax.broadcasted_iota(jnp.int32, (tb, rows), 1).astype(F32)
            onehot = (slot_col == lane + (first_slot + 1).astype(F32)).astype(BF16)
            hg = lax.dot_general(onehot, h_ref[...], TN_DIMS,
                                 preferred_element_type=F32).astype(BF16)
            out = None
            for f0 in (0, fh):
                g = jnp.dot(hg, wgu_s[:, f0:f0 + fh], preferred_element_type=F32)
                u = jnp.dot(hg, wgu_s[:, D_EXPERT + f0:D_EXPERT + f0 + fh],
                            preferred_element_type=F32)
                part = jnp.dot((_silu(g) * u).astype(BF16), wd_s[f0:f0 + fh, :],
                               preferred_element_type=F32)
                out = part if out is None else out + part
            out = out.astype(BF16)
            for c0 in range(0, D_MODEL, cw):
                cols = slice(c0, c0 + cw)
                prev = jnp.where(fresh, y_before(cols), y_ref[:, cols])
                y_ref[:, cols] = prev + gate_col * jnp.dot(onehot, out[:, cols],
                                                           preferred_element_type=F32)

        unit = MOE_TILE_UNIT
        units = (count + (unit - 1)) // unit
        n_big = units // 4
        left = units - 4 * n_big

        def big_tile(k, carry):
            tile(k * (4 * unit), 4 * unit, k == 0)
            return carry

        lax.fori_loop(0, n_big, big_tile, 0)

        @pl.when(jnp.logical_and(left >= 1, left <= 2))
        def _():
            tile(n_big * (4 * unit), 2 * unit, n_big == 0)

        @pl.when(left == 3)
        def _():
            tile(n_big * (4 * unit), 3 * unit, n_big == 0)

        @pl.when(count == 0)
        def _():
            for c0 in range(0, D_MODEL, cw):
                y_ref[:, c0:c0 + cw] = y_before(slice(c0, c0 + cw))


def _moe_expert(h, gates, slots, counts, w_gu, w_down, y, layer, expert):
    tb = MOE_BLOCK
    nc = MOE_W_CHUNKS
    wc = 2 * D_EXPERT // nc
    first = y is None
    blk = lambda s, cnt: (jnp.maximum(s - nc, 0), 0)
    in_specs = [pl.BlockSpec((tb, D_MODEL), blk), pl.BlockSpec((tb, LANES), blk),
                pl.BlockSpec((tb, LANES), blk),
                pl.BlockSpec((None, None, D_MODEL, wc),
                             lambda s, cnt: (layer, expert, 0, jnp.minimum(s, nc - 1))),
                pl.BlockSpec((None, None, wc, D_MODEL),
                             lambda s, cnt: (layer, expert, jnp.minimum(s, nc // 2 - 1), 0))]
    args = [h, gates, slots, w_gu, w_down]
    if not first:
        in_specs.append(pl.BlockSpec((tb, D_MODEL), blk))
        args.append(y)
    return pl.pallas_call(
        functools.partial(_moe_expert_kernel, expert=expert, first=first),
        grid_spec=pltpu.PrefetchScalarGridSpec(
            num_scalar_prefetch=1, grid=(nc + N_TOK // tb,), in_specs=in_specs,
            out_specs=pl.BlockSpec((tb, D_MODEL), blk),
            scratch_shapes=[pltpu.VMEM((D_MODEL, 2 * D_EXPERT), BF16),
                            pltpu.VMEM((D_EXPERT, D_MODEL), BF16)]),
        out_shape=jax.ShapeDtypeStruct((N_TOK, D_MODEL), F32),
        compiler_params=_params("arbitrary"),
        name="moe_expert",
    )(counts, *args)


def _mm_kernel(a_ref, w_ref, o_ref):
    o_ref[...] = jnp.dot(a_ref[...], w_ref[...], preferred_element_type=F32).astype(o_ref.dtype)


def _matmul(a, w, out_dtype, tm, tn):
    m, k = a.shape
    n = w.shape[1]
    return pl.pallas_call(
        _mm_kernel,
        grid=(m // tm, n // tn),
        in_specs=[pl.BlockSpec((tm, k), lambda i, j: (i, 0)),
                  pl.BlockSpec((k, tn), lambda i, j: (0, j))],
        out_specs=pl.BlockSpec((tm, tn), lambda i, j: (i, j)),
        out_shape=jax.ShapeDtypeStruct((m, n), out_dtype),
        compiler_params=_params("arbitrary", "arbitrary"),
        name="matmul",
    )(a, w)


def _swiglu_kernel(a_ref, wg_ref, wu_ref, o_ref):
    a = a_ref[...]
    g = jnp.dot(a, wg_ref[...], preferred_element_type=F32)
    u = jnp.dot(a, wu_ref[...], preferred_element_type=F32)
    o_ref[...] = (_silu(g) * u).astype(o_ref.dtype)


def _swiglu_up(a, w_gu, layer, tm, tn):
    m, k = a.shape
    f = w_gu.shape[2] // 2
    nj = f // tn
    return pl.pallas_call(
        _swiglu_kernel,
        grid=(m // tm, nj),
        in_specs=[pl.BlockSpec((tm, k), lambda i, j: (i, 0)),
                  pl.BlockSpec((None, k, tn), lambda i, j: (layer, 0, j)),
                  pl.BlockSpec((None, k, tn), lambda i, j: (layer, 0, j + nj))],
        out_specs=pl.BlockSpec((tm, tn), lambda i, j: (i, j)),
        out_shape=jax.ShapeDtypeStruct((m, f), BF16),
        compiler_params=_params("arbitrary", "arbitrary"),
        name="swiglu_up",
    )(a, w_gu, w_gu)


def _next_norm_specs(next_norm, tm):
    gain, mods, shift_idx, scale_idx = next_norm
    specs = [pl.BlockSpec((1, D_MODEL), lambda i, *_: (0, 0)),
             _mod_spec(shift_idx, tm), _mod_spec(scale_idx, tm)]
    return specs, [gain.reshape(1, D_MODEL), mods, mods]


def _mm_residual_norm_kernel(a_ref, w_ref, x_ref, g_ref, ng_ref, nsh_ref, nsc_ref, o_ref, h_ref):
    x = x_ref[...] + g_ref[...] * jnp.dot(a_ref[...], w_ref[...], preferred_element_type=F32)
    o_ref[...] = x
    h_ref[...] = (_rms(x) * ng_ref[...] * (1.0 + nsc_ref[...]) + nsh_ref[...]).astype(h_ref.dtype)


def _matmul_residual(a, w, layer, x, mods, gate_idx, next_norm, tm=512):
    k = a.shape[1]
    norm_specs, norm_args = _next_norm_specs(next_norm, tm)
    row = lambda i: (i, 0)
    return pl.pallas_call(
        _mm_residual_norm_kernel,
        grid=(N_TOK // tm,),
        in_specs=[pl.BlockSpec((tm, k), row),
                  pl.BlockSpec((None, k, D_MODEL), lambda i: (layer, 0, 0)),
                  pl.BlockSpec((tm, D_MODEL), row),
                  _mod_spec(gate_idx, tm)] + norm_specs,
        out_specs=[pl.BlockSpec((tm, D_MODEL), row), pl.BlockSpec((tm, D_MODEL), row)],
        out_shape=[jax.ShapeDtypeStruct((N_TOK, D_MODEL), F32),
                   jax.ShapeDtypeStruct((N_TOK, D_MODEL), BF16)],
        compiler_params=_params("arbitrary"),
        name="matmul_residual",
    )(a, w, x, mods, *norm_args)


def _mm_residual_split_kernel(ap_ref, as_ref, w_ref, x_ref, g_ref, o_ref):
    a = _pick_prompt_sample(ap_ref, as_ref, o_ref.shape[0])
    o_ref[...] = x_ref[...] + g_ref[...] * jnp.dot(a, w_ref[...], preferred_element_type=F32)


def _matmul_residual_split(a_prompt, a_sample, w, x, mods, gate_idx, tm=512):
    k = w.shape[0]
    row = lambda i: (i, 0)
    return pl.pallas_call(
        _mm_residual_split_kernel,
        grid=(N_TOK // tm,),
        in_specs=(_prompt_sample_specs(tm, k)
                  + [pl.BlockSpec((k, D_MODEL), lambda i: (0, 0)),
                     pl.BlockSpec((tm, D_MODEL), row),
                     _mod_spec(gate_idx, tm)]),
        out_specs=pl.BlockSpec((tm, D_MODEL), row),
        out_shape=jax.ShapeDtypeStruct((N_TOK, D_MODEL), F32),
        compiler_params=_params("arbitrary"),
        name="matmul_residual_split",
    )(a_prompt, a_sample, w, x, mods)


def _gated_add_norm_kernel(x_ref, y_ref, g_ref, ng_ref, nsh_ref, nsc_ref, o_ref, h_ref):
    x = x_ref[...] + g_ref[...] * y_ref[...]
    o_ref[...] = x
    h_ref[...] = (_rms(x) * ng_ref[...] * (1.0 + nsc_ref[...]) + nsh_ref[...]).astype(h_ref.dtype)


def _gated_add(x, y, mods, gate_idx, next_norm, tm=1024):
    norm_specs, norm_args = _next_norm_specs(next_norm, tm)
    row = lambda i: (i, 0)
    return pl.pallas_call(
        _gated_add_norm_kernel,
        grid=(N_TOK // tm,),
        in_specs=[pl.BlockSpec((tm, D_MODEL), row), pl.BlockSpec((tm, D_MODEL), row),
                  _mod_spec(gate_idx, tm)] + norm_specs,
        out_specs=[pl.BlockSpec((tm, D_MODEL), row), pl.BlockSpec((tm, D_MODEL), row)],
        out_shape=[jax.ShapeDtypeStruct((N_TOK, D_MODEL), F32),
                   jax.ShapeDtypeStruct((N_TOK, D_MODEL), BF16)],
        compiler_params=_params("arbitrary"),
        name="gated_add",
    )(x, y, mods, *norm_args)


def _gated_add_final_kernel(x_ref, y_ref, g_ref, ng_ref, o_ref):
    o_ref[...] = _rms(x_ref[...] + g_ref[...] * y_ref[...]) * ng_ref[...]


def _gated_add_final(x, y, mods, gate_idx, gain, row0, n_rows, tm=1024):
    base = row0 // tm
    row = lambda i: (base + i, 0)
    return pl.pallas_call(
        _gated_add_final_kernel,
        grid=(n_rows // tm,),
        in_specs=[pl.BlockSpec((tm, D_MODEL), row), pl.BlockSpec((tm, D_MODEL), row),
                  pl.BlockSpec((None, None, 1, D_MODEL),
                               lambda i: (_cond_row(base + i, tm), gate_idx, 0, 0)),
                  pl.BlockSpec((1, D_MODEL), lambda i: (0, 0))],
        out_specs=pl.BlockSpec((tm, D_MODEL), lambda i: (i, 0)),
        out_shape=jax.ShapeDtypeStruct((n_rows, D_MODEL), F32),
        compiler_params=_params("arbitrary"),
        name="gated_add_final",
    )(x, y, mods, gain.reshape(1, D_MODEL))


def _log_sigmoid(x):
    return jnp.minimum(x, 0.0) - jnp.log(1.0 + jnp.exp(-jnp.abs(x)))


def _split_bf16(x):
    hi = x.astype(BF16)
    return hi, (x - hi.astype(F32)).astype(BF16)


def _dot_split(a, b):
    a_hi, a_lo = _split_bf16(a)
    b_hi, b_lo = _split_bf16(b)
    return (jnp.dot(a_hi, b_hi, preferred_element_type=F32)
            + jnp.dot(a_lo, b_hi, preferred_element_type=F32)
            + jnp.dot(a_hi, b_lo, preferred_element_type=F32))


def _gla_proj_kernel(h_ref, w_ref, wgd_ref, w0_ref, w1_ref, b_ref, proj_ref, la_ref):
    h = h_ref[...]
    proj_ref[...] = jnp.dot(h, w_ref[...], preferred_element_type=F32).astype(proj_ref.dtype)
    gd = jnp.dot(h, wgd_ref[...], preferred_element_type=F32)
    x0 = _dot_split(gd, w0_ref[...]) + b_ref[0:1, :]
    x1 = _dot_split(gd, w1_ref[...]) + b_ref[1:2, :]
    la_ref[:, :GLA_DK_TOT] = _log_sigmoid(x0) * (1.0 / GATE_TEMP)
    la_ref[:, GLA_DK_TOT:] = _log_sigmoid(x1) * (1.0 / GATE_TEMP)


def _gla_proj(h, w_in, layer, w_gd, w0, w1, b_gate, tm=512):
    fixed = lambda i: (0, 0)
    row = lambda i: (i, 0)
    return pl.pallas_call(
        _gla_proj_kernel,
        grid=(N_TOK // tm,),
        in_specs=[pl.BlockSpec((tm, D_MODEL), row),
                  pl.BlockSpec((None, D_MODEL, GLA_QKVR_W), lambda i: (layer, 0, 0)),
                  pl.BlockSpec((D_MODEL, LANES), fixed),
                  pl.BlockSpec((LANES, GLA_DK_TOT), fixed),
                  pl.BlockSpec((LANES, GLA_DK_TOT), fixed),
                  pl.BlockSpec((2, GLA_DK_TOT), fixed)],
        out_specs=[pl.BlockSpec((tm, GLA_QKVR_W), row),
                   pl.BlockSpec((tm, 2 * GLA_DK_TOT), row)],
        out_shape=[jax.ShapeDtypeStruct((N_TOK, GLA_QKVR_W), BF16),
                   jax.ShapeDtypeStruct((N_TOK, 2 * GLA_DK_TOT), F32)],
        compiler_params=_params("arbitrary"),
        name="gla_proj",
    )(h, w_in, w_gd, w0, w1, b_gate)


def _gla_keep_mask(causal):
    c = GLA_CHUNK
    row = lax.broadcasted_iota(jnp.int32, (c, c), 0)
    col = lax.broadcasted_iota(jnp.int32, (c, c), 1)
    return (col <= row) if causal else (col >= row)


def _gla_cum_decay(la_ref, causal):
    mask = _gla_keep_mask(causal).astype(BF16)
    la = la_ref[...]
    hi = la.astype(BF16)
    lo = (la - hi.astype(F32)).astype(BF16)
    return (jnp.dot(mask, hi, preferred_element_type=F32)
            + jnp.dot(mask, lo, preferred_element_type=F32))


def _gla_direction(q_ref, k_ref, v_ref, b, o_ref, s_ref, b_ref, d, causal, factorised):
    c = GLA_CHUNK
    keep = _gla_keep_mask(causal)
    end = c - 1 if causal else 0
    b_end = b[end:end + 1, :]
    q_dec = q_ref[...].astype(F32) * (GLA_DK ** -0.5) * jnp.exp(b)
    k_end = k_ref[...] * jnp.exp(b_end - b)
    e_end = jnp.exp(b_end)
    eye = (lax.broadcasted_iota(jnp.int32, (GLA_DK, GLA_DK), 0)
           == lax.broadcasted_iota(jnp.int32, (GLA_DK, GLA_DK), 1))
    if factorised:
        k_inv = k_ref[...] * jnp.exp(-b)
    else:
        b_ref[...] = b
        col = lax.broadcasted_iota(jnp.int32, (c, c), 1)
    for h in range(GLA_HEADS):
        ks = slice(h * GLA_DK, (h + 1) * GLA_DK)
        vs = slice(h * GLA_DV, (h + 1) * GLA_DV)
        state = s_ref[d, h]
        qh = q_dec[:, ks].astype(BF16)
        vh = v_ref[:, vs].astype(BF16)
        if factorised:
            scores = lax.dot_general(qh, k_inv[:, ks].astype(BF16), NT_DIMS,
                                     preferred_element_type=F32)
        else:
            def key_column(s, sc, ks=ks):
                base = pl.multiple_of((s // 16) * 16, 16)
                pick = lax.broadcasted_iota(jnp.int32, (16, GLA_DK), 0) == s % 16
                b_s = jnp.sum(jnp.where(pick, b_ref[pl.ds(base, 16), ks], 0.0),
                              axis=0, keepdims=True)
                k_s = jnp.sum(jnp.where(pick, k_ref[pl.ds(base, 16), ks].astype(F32), 0.0),
                              axis=0, keepdims=True)
                decay = jnp.exp(jnp.minimum(b_ref[:, ks] - b_s, 0.0))
                column = jnp.sum(q_ref[:, ks].astype(F32) * (GLA_DK ** -0.5) * k_s * decay,
                                 axis=1, keepdims=True)
                return jnp.where(col == s, column, sc)

            scores = lax.fori_loop(0, c, key_column, jnp.zeros((c, c), F32))
        scores = jnp.where(keep, scores, 0.0)
        o_ref[:, vs] = (jnp.dot(scores.astype(BF16), vh, preferred_element_type=F32)
                        + jnp.dot(qh, state.astype(BF16), preferred_element_type=F32)
                        ).astype(o_ref.dtype)
        e_col = jnp.sum(jnp.where(eye, jnp.broadcast_to(e_end[:, ks], (GLA_DK, GLA_DK)), 0.0),
                        axis=1, keepdims=True)
        s_ref[d, h] = state * e_col + lax.dot_general(
            k_end[:, ks].astype(BF16), vh, TN_DIMS, preferred_element_type=F32)


def _gla_scan_kernel(*refs, n_chunks, has_s0, write_state):
    qf, kf, vf, laf, qb, kb, vb, lab = refs[:8]
    rest = list(refs[8:])
    s0_ref = rest.pop(0) if has_s0 else None
    of_ref, ob_ref = rest.pop(0), rest.pop(0)
    sout_ref = rest.pop(0) if write_state else None
    s_ref, b_ref = rest
    i = pl.program_id(1)

    @pl.when(i == 0)
    def _():
        s_ref[...] = s0_ref[...] if has_s0 else jnp.zeros(s_ref.shape, F32)

    b_f = _gla_cum_decay(laf, True)
    b_b = _gla_cum_decay(lab, False)
    c = GLA_CHUNK
    total_decay = jnp.minimum(jnp.min(b_f[c - 1:c, :]), jnp.min(b_b[0:1, :]))
    factorisable = total_decay >= -GLA_MAX_FACTORISED_DECAY

    def step(factorised):
        _gla_direction(qf, kf, vf, b_f, of_ref, s_ref, b_ref.at[0], 0, True, factorised)
        _gla_direction(qb, kb, vb, b_b, ob_ref, s_ref, b_ref.at[1], 1, False, factorised)

    pl.when(factorisable)(functools.partial(step, True))
    pl.when(jnp.logical_not(factorisable))(functools.partial(step, False))

    if write_state:
        @pl.when(i == n_chunks - 1)
        def _():
            sout_ref[...] = s_ref[...]


def _gla_scan(proj, la, row0, n_batch, seq, s0):
    c = GLA_CHUNK
    n = seq // c
    base = row0 // c
    has_s0 = s0 is not None
    write_state = not has_s0
    fwd = lambda b, i: base + b * n + i
    bwd = lambda b, i: base + b * n + (n - 1 - i)

    def specs(blk):
        return [pl.BlockSpec((c, GLA_DK_TOT), lambda b, i: (blk(b, i), 0)),
                pl.BlockSpec((c, GLA_DK_TOT), lambda b, i: (blk(b, i), 1)),
                pl.BlockSpec((c, GLA_DV_TOT), lambda b, i: (blk(b, i), 1))]
    state_block = (None, 2, GLA_HEADS, GLA_DK, GLA_DV)
    state_spec = pl.BlockSpec(state_block, lambda b, i: (b, 0, 0, 0, 0))
    in_specs = (specs(fwd) + [pl.BlockSpec((c, GLA_DK_TOT), lambda b, i: (fwd(b, i), 0))]
                + specs(bwd) + [pl.BlockSpec((c, GLA_DK_TOT), lambda b, i: (bwd(b, i), 1))])
    args = [proj, proj, proj, la, proj, proj, proj, la]
    if has_s0:
        in_specs.append(state_spec)
        args.append(s0)
    out_rows = n_batch * seq
    out_specs = [pl.BlockSpec((c, GLA_DV_TOT), lambda b, i: (b * n + i, 0)),
                 pl.BlockSpec((c, GLA_DV_TOT), lambda b, i: (b * n + (n - 1 - i), 0))]
    out_shape = [jax.ShapeDtypeStruct((out_rows, GLA_DV_TOT), BF16)] * 2
    if write_state:
        out_specs.append(state_spec)
        out_shape.append(jax.ShapeDtypeStruct((n_batch, 2, GLA_HEADS, GLA_DK, GLA_DV), F32))
    return pl.pallas_call(
        functools.partial(_gla_scan_kernel, n_chunks=n, has_s0=has_s0, write_state=write_state),
        grid=(n_batch, n),
        in_specs=in_specs,
        out_specs=out_specs,
        out_shape=out_shape,
        scratch_shapes=[pltpu.VMEM((2, GLA_HEADS, GLA_DK, GLA_DV), F32),
                        pltpu.VMEM((2, c, GLA_DK_TOT), F32)],
        compiler_params=_params("arbitrary", "arbitrary"),
        name="gla_scan",
    )(*args)


def _gla_out_kernel(ofp_ref, ofs_ref, obp_ref, obs_ref, r_ref, gn_ref, w_ref, *refs, x_pair):
    tm = r_ref.shape[0]
    if x_pair:
        x_in = _pick_prompt_sample(refs[0], refs[1], tm)
        refs = refs[2:]
    else:
        x_in = refs[0][...]
        refs = refs[1:]
    g_ref, ng_ref, nsh_ref, nsc_ref, o_ref, h_ref = refs
    o = (_pick_prompt_sample(ofp_ref, ofs_ref, tm).astype(F32)
         + _pick_prompt_sample(obp_ref, obs_ref, tm).astype(F32))
    normed = jnp.concatenate(
        [_rms(o[:, h * GLA_DV:(h + 1) * GLA_DV]) for h in range(GLA_HEADS)], axis=1)
    mixed = (normed * gn_ref[...] * _silu(r_ref[...].astype(F32))).astype(BF16)
    x = x_in + g_ref[...] * jnp.dot(mixed, w_ref[...], preferred_element_type=F32)
    o_ref[...] = x
    h_ref[...] = (_rms(x) * ng_ref[...] * (1.0 + nsc_ref[...]) + nsh_ref[...]).astype(h_ref.dtype)


def _gla_out(of_p, of_s, ob_p, ob_s, proj, g_tiled, w, x, mods, gate_idx, next_norm, tm=512):
    norm_specs, norm_args = _next_norm_specs(next_norm, tm)
    row = lambda i: (i, 0)
    x_pair = isinstance(x, tuple)
    x_specs = _prompt_sample_specs(tm, D_MODEL) if x_pair else [pl.BlockSpec((tm, D_MODEL), row)]
    x_args = list(x) if x_pair else [x]
    r_block = 2 * GLA_DK_TOT // GLA_DV_TOT + 1
    return pl.pallas_call(
        functools.partial(_gla_out_kernel, x_pair=x_pair),
        grid=(N_TOK // tm,),
        in_specs=(_prompt_sample_specs(tm, GLA_DV_TOT) + _prompt_sample_specs(tm, GLA_DV_TOT)
                  + [pl.BlockSpec((tm, GLA_DV_TOT), lambda i: (i, r_block)),
                     pl.BlockSpec((1, GLA_DV_TOT), lambda i: (0, 0)),
                     pl.BlockSpec((GLA_DV_TOT, D_MODEL), lambda i: (0, 0))]
                  + x_specs + [_mod_spec(gate_idx, tm)] + norm_specs),
        out_specs=[pl.BlockSpec((tm, D_MODEL), row), pl.BlockSpec((tm, D_MODEL), row)],
        out_shape=[jax.ShapeDtypeStruct((N_TOK, D_MODEL), F32),
                   jax.ShapeDtypeStruct((N_TOK, D_MODEL), BF16)],
        compiler_params=_params("arbitrary"),
        name="gla_out",
    )(of_p, of_s, ob_p, ob_s, proj, g_tiled, w, *x_args, mods, *norm_args)


def _rope_pair(u, cs):
    t = u * cs
    return t + pltpu.roll(t, QK_ROPE, axis=1)


def _mla_down_kernel(a_ref, w_ref, qn_ref, kvn_ref, cs_ref, wq_ref, wkv_ref,
                     q_ref, ckv_ref, kr_ref, krp_ref, kv_ref):
    acc = jnp.dot(a_ref[...], w_ref[...], preferred_element_type=F32)
    cs = cs_ref[...]
    ckv = _rms(acc[:, Q_LORA:Q_LORA + KV_LORA]) * kvn_ref[...]
    ckv_ref[...] = ckv
    kv_ref[...] = jnp.dot(ckv.astype(BF16), wkv_ref[...],
                          preferred_element_type=F32).astype(kv_ref.dtype)
    u = acc[:, Q_LORA + KV_LORA:]
    kr_ref[...] = u
    lane = lax.broadcasted_iota(jnp.int32, u.shape, 1)
    krp_ref[...] = jnp.where(lane < QK_ROPE, _rope_pair(u, cs), 0.0).astype(krp_ref.dtype)
    cq = (_rms(acc[:, :Q_LORA]) * qn_ref[...]).astype(BF16)
    q = jnp.dot(cq, wq_ref[...], preferred_element_type=F32)
    for h in range(MLA_HEADS):
        lo = h * 2 * LANES
        q_ref[:, lo:lo + LANES] = q[:, lo:lo + LANES].astype(q_ref.dtype)
        q_ref[:, lo + LANES:lo + 2 * LANES] = _rope_pair(
            q[:, lo + LANES:lo + 2 * LANES], cs).astype(q_ref.dtype)


def _mla_down(h, w_ext, q_norm, kv_norm, cs, wq_ext, w_ukv, tm=512):
    wn = w_ext.shape[1]
    qn = wq_ext.shape[1]
    kvn = w_ukv.shape[1]
    row = lambda i: (i, 0)
    fixed = lambda i: (0, 0)
    return pl.pallas_call(
        _mla_down_kernel,
        grid=(N_TOK // tm,),
        in_specs=[pl.BlockSpec((tm, D_MODEL), row), pl.BlockSpec((D_MODEL, wn), fixed),
                  pl.BlockSpec((1, Q_LORA), fixed), pl.BlockSpec((1, KV_LORA), fixed),
                  pl.BlockSpec((tm, LANES), row), pl.BlockSpec((Q_LORA, qn), fixed),
                  pl.BlockSpec((KV_LORA, kvn), fixed)],
        out_specs=[pl.BlockSpec((tm, qn), row), pl.BlockSpec((tm, KV_LORA), row),
                   pl.BlockSpec((tm, LANES), row), pl.BlockSpec((tm, LANES), row),
                   pl.BlockSpec((tm, kvn), row)],
        out_shape=[jax.ShapeDtypeStruct((N_TOK, qn), BF16),
                   jax.ShapeDtypeStruct((N_TOK, KV_LORA), F32),
                   jax.ShapeDtypeStruct((N_TOK, LANES), F32),
                   jax.ShapeDtypeStruct((N_TOK, LANES), BF16),
                   jax.ShapeDtypeStruct((N_TOK, kvn), BF16)],
        compiler_params=_params("arbitrary"),
        name="mla_down",
    )(h, w_ext, q_norm.reshape(1, Q_LORA), kv_norm.reshape(1, KV_LORA), cs, wq_ext, w_ukv)


def _attn_kernel(q_ref, *refs, heads):
    o_ref = refs[-1]
    sources = [(refs[i], refs[i + 1]) for i in range(0, len(refs) - 1, 2)]
    tq = q_ref.shape[0]
    c = MLA_SCALE * LOG2E
    hw = QK_NOPE + V_HEAD
    for h in range(heads):
        q = q_ref[:, h * hw:(h + 1) * hw]
        m = jnp.full((tq, 1), -jnp.inf, F32)
        l = jnp.zeros((tq, 1), F32)
        acc = jnp.zeros((tq, V_HEAD), F32)
        for kv_ref, kr_ref in sources:
            n_keys = kv_ref.shape[0]
            chunk = min(n_keys, ATTN_KEY_CHUNK)
            for c0 in range(0, n_keys, chunk):
                rows = slice(c0, c0 + chunk)
                k = jnp.concatenate([kv_ref[rows, h * hw:h * hw + QK_NOPE], kr_ref[rows, :]], axis=1)
                s = lax.dot_general(q, k, NT_DIMS, preferred_element_type=F32)
                m_new = jnp.maximum(m, jnp.max(s, axis=1, keepdims=True))
                alpha = jnp.exp2((m - m_new) * c)
                p = jnp.exp2((s - m_new) * c)
                l = alpha * l + jnp.sum(p, axis=1, keepdims=True)
                acc = alpha * acc + jnp.dot(p.astype(BF16),
                                            kv_ref[rows, h * hw + QK_NOPE:(h + 1) * hw],
                                            preferred_element_type=F32)
                m = m_new
        o_ref[:, h * V_HEAD:(h + 1) * V_HEAD] = (acc / l).astype(o_ref.dtype)


def _attention(q, sources, row0, n_batch, seq, tq, heads):
    nq = seq // tq
    base = row0 // tq
    hw = heads * (QK_NOPE + V_HEAD)
    in_specs = [pl.BlockSpec((tq, hw), lambda b, h, i: (base + b * nq + i, h))]
    args = [q]
    for kv, krp, first_row, n_keys in sources:
        first = first_row // n_keys
        in_specs += [pl.BlockSpec((n_keys, hw), lambda b, h, i, first=first: (first + b, h)),
                     pl.BlockSpec((n_keys, LANES), lambda b, h, i, first=first: (first + b, 0))]
        args += [kv, krp]
    return pl.pallas_call(
        functools.partial(_attn_kernel, heads=heads),
        grid=(n_batch, MLA_HEADS // heads, nq),
        in_specs=in_specs,
        out_specs=pl.BlockSpec((tq, heads * V_HEAD), lambda b, h, i: (b * nq + i, h)),
        out_shape=jax.ShapeDtypeStruct((n_batch * seq, MLA_HEADS * V_HEAD), BF16),
        compiler_params=_params("arbitrary", "arbitrary", "arbitrary"),
        name="mla_attention",
    )(*args)


def _rope_tables():
    half = QK_ROPE // 2
    pos = np.arange(DEC_SEQ)
    inv = ROPE_BASE ** (-np.arange(0, half, 2, dtype=np.float64) / half)
    ang = np.concatenate([(pos // GRID_W)[:, None] * inv, (pos % GRID_W)[:, None] * inv], axis=-1)
    ang = ang.astype(np.float32)
    cos = np.repeat(np.cos(ang), 2, axis=1)
    sin = np.repeat(np.sin(ang), 2, axis=1) * np.tile(np.array([-1.0, 1.0], np.float32), half)
    cs = np.concatenate([cos, sin], axis=1).astype(np.float32)
    ident = np.concatenate([np.ones((N_PROMPT, QK_ROPE), np.float32),
                            np.zeros((N_PROMPT, QK_ROPE), np.float32)], axis=1)
    return jnp.asarray(np.concatenate([ident, np.tile(cs, (DEC_BATCH, 1))], axis=0))


def kernel(x_prompt, x_sample, c, state_gla, cache_ckv, cache_krope, c_ctx, ada_w, ada_b, norm_mix, norm_ffn, norm_final, gla_w_in, gla_w_gate_up, gla_b_gate, gla_norm, gla_w_out, mla_w_down, mla_q_norm, mla_kv_norm, mla_w_uq, mla_w_ukv, mla_w_out, ffn_w_gate_up, ffn_w_down, moe_router, moe_w_gate_up, moe_w_down):
    x = (x_prompt.reshape(N_PROMPT, D_MODEL), x_sample.reshape(N_SAMPLE, D_MODEL))
    cond =jnp.concatenate([c_ctx[None], c, jnp.zeros((N_COND - 1 - DEC_BATCH, D_MODEL), F32)])
    mods_all = _ada_mods(cond, ada_w, ada_b).reshape(DEPTH, N_COND, 6, 1, D_MODEL)
    cs = _rope_tables()
    swap = np.arange(QK_ROPE) ^ 1
    gla_w_in_bf16 = gla_w_in.astype(BF16)
    ffn_w_gate_up_bf16 = ffn_w_gate_up.astype(BF16)
    ffn_w_down_bf16 = ffn_w_down.astype(BF16)

    new_gla, new_ckv, new_kr = [], [], []
    h = _modulate(x[0], x[1], norm_mix[0], mods_all[0], 0, 1)
    for layer in range(DEPTH):
        j = layer // 2
        mods = mods_all[layer]
        if layer % 2 == 0:
            w_gd = jnp.pad(gla_w_in[j, :, GLA_QKVR_W:].astype(BF16),
                           ((0, 0), (0, LANES - 2 * GATE_RANK)))
            pad_lo = jnp.zeros((GATE_RANK, GLA_DK_TOT), F32)
            pad_hi = jnp.zeros((LANES - 2 * GATE_RANK, GLA_DK_TOT), F32)
            w0 = jnp.concatenate([gla_w_gate_up[j, 0], pad_lo, pad_hi])
            w1 = jnp.concatenate([pad_lo, gla_w_gate_up[j, 1], pad_hi])
            proj, la = _gla_proj(h, gla_w_in_bf16, j, w_gd, w0, w1, gla_b_gate[j])
            of_p, ob_p, st = _gla_scan(proj, la, 0, BATCH, SEQ, None)
            of_s, ob_s = _gla_scan(proj, la, N_PROMPT, DEC_BATCH, DEC_SEQ, state_gla[:, j])
            new_gla.append(st)
            x, h = _gla_out(of_p, of_s, ob_p, ob_s, proj,
                            jnp.tile(gla_norm[j], GLA_HEADS).reshape(1, GLA_DV_TOT),
                            gla_w_out[j].astype(BF16), x, mods, 2, (norm_ffn[layer], mods, 3, 4))
            act = _swiglu_up(h, ffn_w_gate_up_bf16, j, tm=1024, tn=1408)
            x, h = _matmul_residual(act, ffn_w_down_bf16, j, x, mods, 5,
                                    (norm_mix[layer + 1], mods_all[layer + 1], 0, 1))
        else:
            wd = mla_w_down[j]
            w_down = jnp.concatenate([wd, wd[:, Q_LORA + KV_LORA:][:, swap]], axis=1).astype(BF16)
            wq = mla_w_uq[j].reshape(Q_LORA, MLA_HEADS, QK_NOPE + QK_ROPE)
            w_uq = jnp.concatenate([wq, wq[:, :, QK_NOPE:][:, :, swap]], axis=2)
            w_ukv = mla_w_ukv[j].astype(BF16)
            q, ckv, kr, krp, kv = _mla_down(h, w_down, mla_q_norm[j], mla_kv_norm[j], cs,
                                            w_uq.reshape(Q_LORA, -1).astype(BF16), w_ukv)
            ckv_ctx = cache_ckv[:, j].reshape(DEC_BATCH * PAST_LEN, KV_LORA).astype(BF16)
            kv_ctx = _matmul(ckv_ctx, w_ukv, BF16, tm=1024, tn=1024)
            kr_ctx = jnp.pad(cache_krope[:, j].reshape(DEC_BATCH * PAST_LEN, QK_ROPE),
                             ((0, 0), (0, LANES - QK_ROPE))).astype(BF16)
            o_p = _attention(q, [(kv, krp, 0, SEQ)], 0, BATCH, SEQ, tq=SEQ, heads=MLA_HEADS)
            o_s = _attention(q, [(kv_ctx, kr_ctx, 0, PAST_LEN), (kv, krp, N_PROMPT, DEC_SEQ)],
                             N_PROMPT, DEC_BATCH, DEC_SEQ, tq=1024, heads=1)
            new_ckv.append(ckv[:N_PROMPT].reshape(BATCH, SEQ, KV_LORA))
            new_kr.append(kr[:N_PROMPT, :QK_ROPE].reshape(BATCH, SEQ, QK_ROPE))
            x = _matmul_residual_split(o_p, o_s, mla_w_out[j].astype(BF16), x, mods, 2)

            router_pad = jnp.pad(moe_router[j], ((0, 0), (0, LANES - N_EXPERTS)))
            h, gates, slots, cnt = _modulate_router(x, norm_ffn[layer], mods, 3, 4, router_pad)
            counts = cnt[:, 0, :N_EXPERTS].astype(jnp.int32).reshape(-1)
            y = None
            for e in range(N_EXPERTS):
                y = _moe_expert(h, gates, slots, counts, moe_w_gate_up, moe_w_down, y, j, e)
            if layer + 1 < DEPTH:
                x, h = _gated_add(x, y, mods, 5, (norm_mix[layer + 1], mods_all[layer + 1], 0, 1))
            else:
                out_p = _gated_add_final(x, y, mods, 5, norm_final, 0, N_PROMPT)
                out_s = _gated_add_final(x, y, mods, 5, norm_final, N_PROMPT, N_SAMPLE)

    return (out_p.reshape(BATCH, SEQ, D_MODEL),
            out_s.reshape(DEC_BATCH, DEC_SEQ, D_MODEL),
            jnp.stack(new_gla, axis=1),
            jnp.stack(new_ckv, axis=1),
            jnp.stack(new_kr, axis=1))
```
